```python
import functools
import jax, jax.numpy as jnp
from jax import lax
import numpy as np

D_MODEL = 1024
BATCH = 2
SEQ = 8192
DEPTH = 2
DEC_BATCH = 128
DEC_SEQ = 1
PAST_LEN = 2048
PAGE_SIZE = 128

HEAD_DIM = 64
QBLK = 128
NSA_HEADS = 8
NSA_KV_GROUPS = 2
NSA_REP = NSA_HEADS // NSA_KV_GROUPS
NSA_Q_W = NSA_HEADS * HEAD_DIM
NSA_KV_W = NSA_KV_GROUPS * HEAD_DIM
CMP_BLOCK = 32
SEL_BLOCK = 64
N_SEL = 16
NSA_WINDOW = 512
FORCE_SCORE = 1.0e4
D_RNN = D_MODEL // 2
RG_BLOCKS = 8
RG_BW = D_RNN // RG_BLOCKS
RG_C = 8.0
CONV_W = 4
CHUNK = 128
C_W = D_MODEL // 2
C_GROUPS = 8
C_GW = C_W // C_GROUPS
DIL_HEADS = 8
DIL_W = DIL_HEADS * HEAD_DIM
DIL_PATTERNS = ((128, 1), (512, 4), (2048, 16))
DIL_MAX = 2048
FFN_HIDDEN = -(-8 * D_MODEL // (3 * 256)) * 256
AB_SIZES = (NSA_Q_W, NSA_KV_W, NSA_KV_W, NSA_KV_W, NSA_KV_W, NSA_KV_W, NSA_KV_W, NSA_HEADS * 3, D_RNN, D_RNN)
CD_SIZES = (C_W, C_W, DIL_W, DIL_W, DIL_W)
IN_AB = NSA_Q_W + 6 * NSA_KV_W + NSA_HEADS * 3 + 2 * D_RNN
IN_CD = 2 * C_W + 3 * DIL_W
OUT_AB = NSA_Q_W + D_RNN
OUT_CD = C_W + DIL_W
RMS_EPS = 1e-6
NEG_INF = -1e30

kernel_name = "hybrid_nsa_rglru_gmlp_dilated_step"


def rms_norm(x, g):
    xf = x.astype(jnp.float32)
    y = xf * lax.rsqrt(jnp.mean(xf * xf, axis=-1, keepdims=True) + RMS_EPS)
    return (y * g.astype(jnp.float32)).astype(x.dtype)


def split_cols(z, sizes):
    offs = np.cumsum(sizes)[:-1].tolist()
    return jnp.split(z, offs, axis=-1)


def masked_softmax(s, mask):
    s = jnp.where(mask, s.astype(jnp.float32), NEG_INF)
    m = jnp.max(s, axis=-1, keepdims=True)
    e = jnp.where(mask, jnp.exp(s - m), 0.0)
    den = jnp.maximum(jnp.sum(e, axis=-1, keepdims=True), 1e-30)
    return e / den, (m + jnp.log(den))[..., 0]


def block_attend(q, k, v, mask):
    s = jnp.einsum('bnqgrd,bnkgd->bnqgrk', q, k) * (q.shape[-1] ** -0.5)
    p, _ = masked_softmax(s, mask[None, :, :, None, None, :])
    return jnp.einsum('bnqgrk,bnkgd->bnqgrd', p.astype(v.dtype), v)


def nsa_compress(rows, w1, w2, pos):
    B, L, G, hd = rows.shape
    blk = rows.reshape(B, L // CMP_BLOCK, CMP_BLOCK, G, hd) + pos[None, None, :, None, :]
    hid = jax.nn.gelu(jnp.einsum('bnlgd,lde->bnge', blk, w1))
    return jnp.einsum('bnge,ef->bngf', hid, w2)


def nsa_global_branches(q, q_pos, kc, vc, ks, vs):
    B, T, G, R, hd = q.shape
    NC = kc.shape[1]
    NB = ks.shape[1] // SEL_BLOCK
    scale = hd ** -0.5
    s = jnp.einsum('btgrd,bngd->btgrn', q, kc) * scale
    cmp_end = (jnp.arange(NC) + 1) * CMP_BLOCK - 1
    cmask = cmp_end[None, :] <= q_pos[:, None]
    p, _ = masked_softmax(s, cmask[None, :, None, None, :])
    o_cmp = jnp.einsum('btgrn,bngd->btgrd', p.astype(vc.dtype), vc)
    imp = p.sum(axis=3).reshape(B, T, G, NB, SEL_BLOCK // CMP_BLOCK).sum(-1)
    blk = jnp.arange(NB)
    forced = (blk[None, :] == 0) | (blk[None, :] == (q_pos // SEL_BLOCK)[:, None])
    valid = blk[None, :] * SEL_BLOCK <= q_pos[:, None]
    imp = jnp.where(forced[None, :, None, :], FORCE_SCORE, imp)
    imp = jnp.where(valid[None, :, None, :], imp, -1.0)
    n_sel = min(N_SEL, NB)
    _, idx = lax.top_k(imp, n_sel)
    idx_g = idx.transpose(0, 2, 1, 3)
    take = jax.vmap(jax.vmap(lambda blocks, i: blocks[i]))
    kb = ks.reshape(B, NB, SEL_BLOCK, G, hd).transpose(0, 3, 1, 2, 4)
    vb = vs.reshape(B, NB, SEL_BLOCK, G, hd).transpose(0, 3, 1, 2, 4)
    k_sel = take(kb, idx_g).reshape(B, G, T, n_sel * SEL_BLOCK, hd)
    v_sel = take(vb, idx_g).reshape(B, G, T, n_sel * SEL_BLOCK, hd)
    kpos = (idx_g[..., None] * SEL_BLOCK + jnp.arange(SEL_BLOCK)).reshape(B, G, T, n_sel * SEL_BLOCK)
    smask = (kpos <= q_pos[None, None, :, None]).transpose(0, 2, 1, 3)[:, :, :, None, :]
    s2 = jnp.einsum('btgrd,bgtkd->btgrk', q, k_sel) * scale
    p2, _ = masked_softmax(s2, smask)
    o_sel = jnp.einsum('btgrk,bgtkd->btgrd', p2.astype(v_sel.dtype), v_sel)
    return o_cmp, o_sel


def banded_window_attn(q, k, v):
    B, T, G, R, hd = q.shape
    nq, nb = T // QBLK, NSA_WINDOW // QBLK
    pad = ((0, 0), (NSA_WINDOW, 0), (0, 0), (0, 0))
    kp = jnp.pad(k, pad).reshape(B, nq + nb, QBLK, G, hd)
    vp = jnp.pad(v, pad).reshape(B, nq + nb, QBLK, G, hd)
    kband = jnp.concatenate([kp[:, j:j + nq] for j in range(nb + 1)], axis=2)
    vband = jnp.concatenate([vp[:, j:j + nq] for j in range(nb + 1)], axis=2)
    qpos = jnp.arange(nq)[:, None] * QBLK + jnp.arange(QBLK)[None, :]
    kpos = jnp.arange(nq)[:, None] * QBLK - NSA_WINDOW + jnp.arange((nb + 1) * QBLK)[None, :]
    dist = qpos[:, :, None] - kpos[:, None, :]
    mask = (kpos[:, None, :] >= 0) & (dist >= 0) & (dist <= NSA_WINDOW)
    o = block_attend(q.reshape(B, nq, QBLK, G, R, hd), kband, vband, mask)
    return o.reshape(B, T, G, R, hd)


def causal_conv(x, buf, w, b):
    T = x.shape[1]
    xp = jnp.concatenate([buf.astype(x.dtype), x], axis=1)
    y = b + sum(xp[:, i:i + T] * w[i] for i in range(CONV_W))
    return y, xp[:, T:]


def rg_lru(x, h0, wa, ba, wx, bx, lam):
    B, T, C = x.shape
    xb = x.reshape(B, T, RG_BLOCKS, RG_BW)
    r = jax.nn.sigmoid((jnp.einsum('btnc,ncd->btnd', xb, wa).reshape(B, T, C) + ba).astype(jnp.float32))
    i = jax.nn.sigmoid((jnp.einsum('btnc,ncd->btnd', xb, wx).reshape(B, T, C) + bx).astype(jnp.float32))
    log_a = -RG_C * r * jax.nn.softplus(-lam.astype(jnp.float32))
    a = jnp.exp(log_a)
    u = jnp.sqrt(-jnp.expm1(2.0 * log_a)) * (i * x.astype(jnp.float32))
    u = u.at[:, 0].add(a[:, 0] * h0.astype(jnp.float32))

    def combine(left, right):
        a_l, b_l = left
        a_r, b_r = right
        return a_l * a_r, a_r * b_l + b_r

    _, hs = lax.associative_scan(combine, (a, u), axis=1)
    return hs.astype(x.dtype), hs[:, -1].astype(h0.dtype)


def dilated_attn(q, q_pos, k, v, k_start):
    Lk = k.shape[1]
    scale = q.shape[-1] ** -0.5
    outs, lses = [], []
    for window, dil in DIL_PATTERNS:
        dist = jnp.arange(window // dil + 1, dtype=jnp.int32) * dil
        kpos = q_pos[:, None] - dist[None, :]
        row = kpos - k_start
        valid = (kpos >= 0) & (row >= 0)
        row = jnp.clip(row, 0, Lk - 1)
        kg, vg = k[:, row], v[:, row]
        s = jnp.einsum('bthd,btjhd->bthj', q, kg) * scale
        p, lse = masked_softmax(s, valid[None, :, None, :])
        outs.append(jnp.einsum('bthj,btjhd->bthd', p.astype(vg.dtype), vg))
        lses.append(lse)
    wts = jax.nn.softmax(jnp.stack(lses), axis=0)
    return jnp.sum(wts[..., None].astype(q.dtype) * jnp.stack(outs), axis=0)


def nsa_rglru_mixer(h, prm, past):
    B, T, _ = h.shape
    past_len = 0 if past is None else PAST_LEN
    q_pos = past_len + jnp.arange(T, dtype=jnp.int32)
    z = jnp.einsum('btd,de->bte', h, prm['w_in'])
    q, kc_t, vc_t, ks, vs, kw, vw, gl, xr, gr = split_cols(z, AB_SIZES)
    q = rms_norm(q.reshape(B, T, NSA_KV_GROUPS, NSA_REP, HEAD_DIM), prm['q_gain'])
    kv = lambda t: t.reshape(B, T, NSA_KV_GROUPS, HEAD_DIM)
    kc_t, vc_t, vs, vw = kv(kc_t), kv(vc_t), kv(vs), kv(vw)
    ks = rms_norm(kv(ks), prm['k_gain'][1])
    kw = rms_norm(kv(kw), prm['k_gain'][2])
    rows = jnp.stack([kc_t, vc_t, ks, vs], axis=2)
    win_rows = jnp.stack([kw, vw], axis=2)
    if past is None:
        full = rows
    else:
        past_rows = past['cache'][past['page_table']]
        full = jnp.concatenate([past_rows.reshape(B, -1, 4, NSA_KV_GROUPS, HEAD_DIM), rows], axis=1)
    L = full.shape[1]
    full = jnp.pad(full, ((0, 0), (0, -L % SEL_BLOCK), (0, 0), (0, 0), (0, 0)))
    kc = rms_norm(nsa_compress(full[:, :, 0], prm['cmp_w1'][0], prm['cmp_w2'][0], prm['cmp_pos'][0]), prm['k_gain'][0])
    vc = nsa_compress(full[:, :, 1], prm['cmp_w1'][1], prm['cmp_w2'][1], prm['cmp_pos'][1])
    ks_all, vs_all = full[:, :, 2], full[:, :, 3]
    if past is None:
        nq = T // QBLK
        qb = q.reshape(B, nq, QBLK, NSA_KV_GROUPS, NSA_REP, HEAD_DIM).swapaxes(0, 1)
        pb = q_pos.reshape(nq, QBLK)
        o_cmp, o_sel = lax.map(lambda a: nsa_global_branches(a[0], a[1], kc, vc, ks_all, vs_all), (qb, pb))
        o_cmp = o_cmp.swapaxes(0, 1).reshape(q.shape)
        o_sel = o_sel.swapaxes(0, 1).reshape(q.shape)
        o_win = banded_window_attn(q, kw, vw)
        new_win = win_rows[:, -min(NSA_WINDOW, T):]
    else:
        o_cmp, o_sel = nsa_global_branches(q, q_pos, kc, vc, ks_all, vs_all)
        wb = past['win'].shape[1]
        buf = jnp.concatenate([past['win'].astype(win_rows.dtype), win_rows], axis=1)
        kpos = PAST_LEN - wb + jnp.arange(wb + T)
        dist = q_pos[:, None] - kpos[None, :]
        mask = ((dist >= 0) & (dist <= NSA_WINDOW))[None]
        o_win = block_attend(q[:, None], buf[:, None, :, 0], buf[:, None, :, 1], mask)[:, 0]
        new_win = buf[:, -min(NSA_WINDOW, PAST_LEN + T):]
    g = jax.nn.sigmoid(gl.astype(jnp.float32)).reshape(B, T, NSA_KV_GROUPS, NSA_REP, 3).astype(h.dtype)
    o_nsa = (g[..., 0:1] * o_cmp + g[..., 1:2] * o_sel + g[..., 2:3] * o_win).reshape(B, T, NSA_Q_W)
    conv_buf = jnp.zeros((B, CONV_W - 1, D_RNN), h.dtype) if past is None else past['conv']
    h0 = jnp.zeros((B, D_RNN), h.dtype) if past is None else past['h']
    xc, new_conv = causal_conv(xr, conv_buf, prm['conv_w'], prm['conv_b'])
    hs, h_last = rg_lru(xc, h0, prm['wa'], prm['ba'], prm['wx'], prm['bx'], prm['lam'])
    o_rnn = hs * jax.nn.gelu(gr)
    out = jnp.einsum('bte,ed->btd', jnp.concatenate([o_nsa.astype(h.dtype), o_rnn], axis=-1), prm['w_out'])
    return out, (rows, new_win, h_last, new_conv)


def gmlp_dilated_mixer(h, prm, past):
    B, T, _ = h.shape
    past_len = 0 if past is None else PAST_LEN
    q_pos = past_len + jnp.arange(T, dtype=jnp.int32)
    z = jnp.einsum('btd,de->bte', h, prm['w_in'])
    u, v, q, k, vv = split_cols(z, CD_SIZES)
    u = jax.nn.gelu(u)
    v = rms_norm(jax.nn.gelu(v), prm['v_gain'])
    lc = min(T, CHUNK)
    ws = prm['ws'][:, :lc, :lc] * jnp.tril(jnp.ones((lc, lc), h.dtype))
    vch = v.reshape(B, T // lc, lc, C_GROUPS, C_GW)
    mixed = jnp.einsum('gts,bnsgc->bntgc', ws, vch) + prm['bs'][:, :lc].T[None, None, :, :, None]
    o_c = u * mixed.reshape(B, T, C_W)
    q = rms_norm(q.reshape(B, T, DIL_HEADS, HEAD_DIM), prm['q_gain'])
    k = rms_norm(k.reshape(B, T, DIL_HEADS, HEAD_DIM), prm['k_gain'])
    vv = vv.reshape(B, T, DIL_HEADS, HEAD_DIM)
    kv_rows = jnp.stack([k, vv], axis=2)
    if past is None:
        pad = ((0, 0), (DIL_MAX, 0), (0, 0), (0, 0))
        kp, vp = jnp.pad(k, pad), jnp.pad(vv, pad)
        nq = T // QBLK
        qb = q.reshape(B, nq, QBLK, DIL_HEADS, HEAD_DIM).swapaxes(0, 1)
        pb = q_pos.reshape(nq, QBLK)
        o_d = lax.map(lambda a: dilated_attn(a[0], a[1], kp, vp, -DIL_MAX), (qb, pb))
        o_d = o_d.swapaxes(0, 1).reshape(B, T, DIL_W)
        new_kv = kv_rows[:, -min(DIL_MAX, T):]
    else:
        wb = past['dil'].shape[1]
        buf = jnp.concatenate([past['dil'].astype(kv_rows.dtype), kv_rows], axis=1)
        o_d = dilated_attn(q, q_pos, buf[:, :, 0], buf[:, :, 1], PAST_LEN - wb).reshape(B, T, DIL_W)
        new_kv = buf[:, -min(DIL_MAX, PAST_LEN + T):]
    out = jnp.einsum('bte,ed->btd', jnp.concatenate([o_c, o_d.astype(h.dtype)], axis=-1), prm['w_out'])
    return out, (new_kv, v)


def residual_layer(x, c, ln_mix, ln_ffn, w_ada_l, b_ada_l, w_gate, w_up, w_down, mixer):
    mod = jnp.einsum('bd,de->be', jax.nn.silu(c), w_ada_l) + b_ada_l
    sh_m, sc_m, g_m, sh_f, sc_f, g_f = jnp.split(mod[:, None, :], 6, axis=-1)
    out, state = mixer(rms_norm(x, ln_mix) * (1 + sc_m) + sh_m)
    x = x + g_m * out.astype(x.dtype)
    hf = rms_norm(x, ln_ffn) * (1 + sc_f) + sh_f
    ffn = jnp.einsum('btf,fd->btd', jax.nn.silu(hf @ w_gate) * (hf @ w_up), w_down)
    return x + g_f * ffn, state


def setup_inputs(seed: int = 0) -> dict:
    key = jax.random.key(seed)
    keys = iter(jax.random.split(key, 64))

    def nrm(shape, scale):
        return jax.random.normal(next(keys), shape, jnp.float32) * scale

    def gain(shape):
        return 1.0 + nrm(shape, 0.1)

    n_even, n_odd = (DEPTH + 1) // 2, DEPTH // 2
    n_pages = PAST_LEN // PAGE_SIZE
    n_used = DEC_BATCH * n_pages
    n_pool = n_used + (n_used + 3) // 4
    page_table = jax.random.permutation(next(keys), n_pool)[:n_used].reshape(DEC_BATCH, n_pages).astype(jnp.int32)
    win_buf = min(NSA_WINDOW, PAST_LEN)
    dil_buf = min(DIL_MAX, PAST_LEN)
    a0 = jax.random.uniform(next(keys), (n_even, D_RNN), jnp.float32, 0.9, 0.999)
    return {
        "x_prompt": nrm((BATCH, SEQ, D_MODEL), 1.0),
        "x_sample": nrm((DEC_BATCH, DEC_SEQ, D_MODEL), 1.0),
        "cache_nsa_kv": nrm((n_even, n_pool, PAGE_SIZE, 4, NSA_KV_GROUPS, HEAD_DIM), 1.0),
        "state_nsa_win": nrm((n_even, DEC_BATCH, win_buf, 2, NSA_KV_GROUPS, HEAD_DIM), 1.0),
        "state_rglru_h": nrm((n_even, DEC_BATCH, D_RNN), 0.5),
        "state_rglru_conv": nrm((n_even, DEC_BATCH, CONV_W - 1, D_RNN), 1.0),
        "state_dil_kv": nrm((n_odd, DEC_BATCH, dil_buf, 2, DIL_HEADS, HEAD_DIM), 1.0),
        "page_table": page_table,
        "c_prompt": nrm((BATCH, D_MODEL), 1.0),
        "c_sample": nrm((DEC_BATCH, D_MODEL), 1.0),
        "norm_mix_g": gain((DEPTH, D_MODEL)),
        "norm_ffn_g": gain((DEPTH, D_MODEL)),
        "w_ada": nrm((DEPTH, D_MODEL, 6 * D_MODEL), 0.5 * D_MODEL ** -0.5),
        "b_ada": nrm((DEPTH, 6 * D_MODEL), 0.02),
        "w_ffn_gate": nrm((DEPTH, D_MODEL, FFN_HIDDEN), D_MODEL ** -0.5),
        "w_ffn_up": nrm((DEPTH, D_MODEL, FFN_HIDDEN), D_MODEL ** -0.5),
        "w_ffn_down": nrm((DEPTH, FFN_HIDDEN, D_MODEL), FFN_HIDDEN ** -0.5),
        "w_in_ab": nrm((n_even, D_MODEL, IN_AB), D_MODEL ** -0.5),
        "w_out_ab": nrm((n_even, OUT_AB, D_MODEL), OUT_AB ** -0.5),
        "nsa_q_gain": gain((n_even, HEAD_DIM)),
        "nsa_k_gain": gain((n_even, 3, HEAD_DIM)),
        "nsa_cmp_w1": nrm((n_even, 2, CMP_BLOCK, HEAD_DIM, HEAD_DIM), (CMP_BLOCK * HEAD_DIM) ** -0.5),
        "nsa_cmp_w2": nrm((n_even, 2, HEAD_DIM, HEAD_DIM), HEAD_DIM ** -0.5),
        "nsa_cmp_pos": nrm((n_even, 2, CMP_BLOCK, HEAD_DIM), 0.1),
        "rg_conv_w": nrm((n_even, CONV_W, D_RNN), CONV_W ** -0.5),
        "rg_conv_b": nrm((n_even, D_RNN), 0.02),
        "rg_wa": nrm((n_even, RG_BLOCKS, RG_BW, RG_BW), RG_BW ** -0.5),
        "rg_ba": nrm((n_even, D_RNN), 0.1),
        "rg_wx": nrm((n_even, RG_BLOCKS, RG_BW, RG_BW), RG_BW ** -0.5),
        "rg_bx": nrm((n_even, D_RNN), 0.1),
        "rg_lambda": jnp.log(a0) - jnp.log1p(-a0),
        "w_in_cd": nrm((n_odd, D_MODEL, IN_CD), D_MODEL ** -0.5),
        "w_out_cd": nrm((n_odd, OUT_CD, D_MODEL), OUT_CD ** -0.5),
        "gmlp_v_gain": gain((n_odd, C_W)),
        "gmlp_ws": nrm((n_odd, C_GROUPS, CHUNK, CHUNK), CHUNK ** -0.5),
        "gmlp_bs": 1.0 + nrm((n_odd, C_GROUPS, CHUNK), 0.1),
        "dil_q_gain": gain((n_odd, HEAD_DIM)),
        "dil_k_gain": gain((n_odd, HEAD_DIM)),
    }


def reference(x_prompt, x_sample, cache_nsa_kv, state_nsa_win, state_rglru_h, state_rglru_conv, state_dil_kv,
              page_table, c_prompt, c_sample, norm_mix_g, norm_ffn_g, w_ada, b_ada, w_ffn_gate, w_ffn_up,
              w_ffn_down, w_in_ab, w_out_ab, nsa_q_gain, nsa_k_gain, nsa_cmp_w1, nsa_cmp_w2, nsa_cmp_pos,
              rg_conv_w, rg_conv_b, rg_wa, rg_ba, rg_wx, rg_bx, rg_lambda, w_in_cd, w_out_cd, gmlp_v_gain,
              gmlp_ws, gmlp_bs, dil_q_gain, dil_k_gain):
    yp, ys = x_prompt, x_sample
    kv_p, kv_s, win_p, win_s, h_p, h_s, conv_p, conv_s, dil_p, dil_s, gv_s = ([] for _ in range(11))
    for layer in range(DEPTH):
        i = layer // 2
        common = (norm_mix_g[layer], norm_ffn_g[layer], w_ada[layer], b_ada[layer],
                  w_ffn_gate[layer], w_ffn_up[layer], w_ffn_down[layer])
        if layer % 2 == 0:
            prm = dict(w_in=w_in_ab[i], w_out=w_out_ab[i], q_gain=nsa_q_gain[i], k_gain=nsa_k_gain[i],
                       cmp_w1=nsa_cmp_w1[i], cmp_w2=nsa_cmp_w2[i], cmp_pos=nsa_cmp_pos[i],
                       conv_w=rg_conv_w[i], conv_b=rg_conv_b[i], wa=rg_wa[i], ba=rg_ba[i],
                       wx=rg_wx[i], bx=rg_bx[i], lam=rg_lambda[i])
            past = dict(cache=cache_nsa_kv[i], page_table=page_table, win=state_nsa_win[i],
                        h=state_rglru_h[i], conv=state_rglru_conv[i])
            yp, st = residual_layer(yp, c_prompt, *common, functools.partial(nsa_rglru_mixer, prm=prm, past=None))
            kv_p.append(st[0]); win_p.append(st[1]); h_p.append(st[2]); conv_p.append(st[3])
            ys, st = residual_layer(ys, c_sample, *common, functools.partial(nsa_rglru_mixer, prm=prm, past=past))
            kv_s.append(st[0]); win_s.append(st[1]); h_s.append(st[2]); conv_s.append(st[3])
        else:
            prm = dict(w_in=w_in_cd[i], w_out=w_out_cd[i], v_gain=gmlp_v_gain[i], ws=gmlp_ws[i], bs=gmlp_bs[i],
                       q_gain=dil_q_gain[i], k_gain=dil_k_gain[i])
            past = dict(dil=state_dil_kv[i])
            yp, st = residual_layer(yp, c_prompt, *common, functools.partial(gmlp_dilated_mixer, prm=prm, past=None))
            dil_p.append(st[0])
            ys, st = residual_layer(ys, c_sample, *common, functools.partial(gmlp_dilated_mixer, prm=prm, past=past))
            dil_s.append(st[0]); gv_s.append(st[1])
    return (yp, ys, jnp.stack(kv_p), jnp.stack(kv_s), jnp.stack(win_p), jnp.stack(win_s),
            jnp.stack(h_p), jnp.stack(h_s), jnp.stack(conv_p), jnp.stack(conv_s),
            jnp.stack(dil_p), jnp.stack(dil_s), jnp.stack(gv_s))
```

```python
import functools

import numpy as np
import jax
import jax.numpy as jnp
from jax import lax
from jax.experimental import pallas as pl
from jax.experimental.pallas import tpu as pltpu

F32 = jnp.float32
BF16 = jnp.bfloat16

LANES = 128
HEAD_DIM = 64
QBLK = 128
CMP_BLOCK = 32
SEL_BLOCK = 64
N_SEL = 16
NSA_WINDOW = 512
FORCE_SCORE = 1.0e4
DIL_PATTERNS = ((128, 1), (512, 4), (2048, 16))
DIL_MAX = 2048
RG_C = 8.0
RMS_EPS = 1e-6
NEG_INF = -1e30
VMEM_LIMIT = 56 * 1024 * 1024


def _cparams(*sem):
    return pltpu.CompilerParams(dimension_semantics=sem, vmem_limit_bytes=VMEM_LIMIT)


def _dot(a, b):
    return jnp.dot(a, b, preferred_element_type=F32)


def _dot_nt(a, b):
    return lax.dot_general(a, b, (((1,), (1,)), ((), ())), preferred_element_type=F32)


def _gelu(x):
    return 0.5 * x * (1.0 + jnp.tanh(np.sqrt(2.0 / np.pi) * (x + 0.044715 * (x * x * x))))


def _sigmoid(x):
    return 1.0 / (1.0 + jnp.exp(-x))


def _head_norm(z, gain):
    lo = lax.broadcasted_iota(jnp.int32, z.shape, 1) < HEAD_DIM
    z2 = z * z
    s_lo = jnp.sum(jnp.where(lo, z2, 0.0), axis=-1, keepdims=True)
    s_hi = jnp.sum(jnp.where(lo, 0.0, z2), axis=-1, keepdims=True)
    inv = lax.rsqrt(jnp.where(lo, s_lo, s_hi) * (1.0 / HEAD_DIM) + RMS_EPS)
    return z * inv * gain


def _ada_kernel(c_ref, w_ref, b_ref, o_ref):
    c = c_ref[...]
    s = c * _sigmoid(c)
    o_ref[0] = _dot(s.astype(BF16), w_ref[0]) + b_ref[0]


def ada_mod(c_all, w_ada, b_ada):
    m, d = c_all.shape
    nl, _, n = w_ada.shape
    tn = 1536
    return pl.pallas_call(
        _ada_kernel,
        grid=(nl, n // tn),
        in_specs=[pl.BlockSpec((m, d), lambda l, j: (0, 0)),
                  pl.BlockSpec((1, d, tn), lambda l, j: (l, 0, j)),
                  pl.BlockSpec((1, 1, tn), lambda l, j: (l, 0, j))],
        out_specs=pl.BlockSpec((1, m, tn), lambda l, j: (l, 0, j)),
        out_shape=jax.ShapeDtypeStruct((nl, m, n), F32),
        compiler_params=_cparams("arbitrary", "arbitrary"),
        name="ada_mod",
    )(c_all, w_ada, b_ada.reshape(nl, 1, n))


def _in_proj_kernel(x_ref, sc_ref, sh_ref, g_ref, w_ref, hg_ref, *out_refs, norm_blocks, outs):
    x = x_ref[0]
    ms = jnp.mean(x * x, axis=-1, keepdims=True)
    h = x * lax.rsqrt(ms + RMS_EPS) * g_ref[...]
    h = h * (1.0 + sc_ref[0]) + sh_ref[0]
    z = _dot(h.astype(BF16), w_ref[...])
    nblk = z.shape[1] // LANES
    blocks = []
    for j in range(nblk):
        zb = z[:, j * LANES:(j + 1) * LANES]
        if j in norm_blocks:
            zb = _head_norm(zb, hg_ref[:, j * LANES:(j + 1) * LANES])
        blocks.append(zb)
    for o_ref, (c0, width, _) in zip(out_refs, outs):
        for j in range(width // LANES):
            o_ref[0, :, j * LANES:(j + 1) * LANES] = blocks[c0 // LANES + j].astype(o_ref.dtype)


def in_proj(x, mod, sh_idx, sc_idx, g, w, head_gain, norm_blocks, outs, tm):
    bm, t, d = x.shape
    r = mod.shape[1]
    n = w.shape[1]
    tm = min(tm, t)
    if r == 1:
        mod_spec = lambda k: pl.BlockSpec((1, 1, d), lambda b, i: (b, 0, k))
    else:
        assert r == t and tm == t
        mod_spec = lambda k: pl.BlockSpec((1, tm, d), lambda b, i: (b, 0, k))
    kern = functools.partial(_in_proj_kernel, norm_blocks=tuple(norm_blocks), outs=tuple(outs))
    return pl.pallas_call(
        kern,
        grid=(bm, t // tm),
        in_specs=[pl.BlockSpec((1, tm, d), lambda b, i: (b, i, 0)),
                  mod_spec(sc_idx), mod_spec(sh_idx),
                  pl.BlockSpec((1, d), lambda b, i: (0, 0)),
                  pl.BlockSpec((d, n), lambda b, i: (0, 0)),
                  pl.BlockSpec((1, n), lambda b, i: (0, 0))],
        out_specs=[pl.BlockSpec((1, tm, wd), lambda b, i: (b, i, 0)) for (_, wd, _) in outs],
        out_shape=[jax.ShapeDtypeStruct((bm, t, wd), dt) for (_, wd, dt) in outs],
        compiler_params=_cparams("arbitrary", "arbitrary"),
        name="in_proj",
    )(x, mod, mod, g.reshape(1, d), w, head_gain)


def _out_ffn_kernel(ma_ref, mb_ref, x_ref, gm_ref, shf_ref, scf_ref, gf_ref, lnf_ref,
                    woa_ref, wob_ref, wg_ref, wu_ref, wd_ref, o_ref, *, hidden_chunk):
    mix = _dot(ma_ref[0], woa_ref[...]) + _dot(mb_ref[0], wob_ref[...])
    x1 = x_ref[0] + gm_ref[0] * mix
    ms = jnp.mean(x1 * x1, axis=-1, keepdims=True)
    hf = x1 * lax.rsqrt(ms + RMS_EPS) * lnf_ref[...]
    hf = (hf * (1.0 + scf_ref[0]) + shf_ref[0]).astype(BF16)
    hidden = wg_ref.shape[1]
    ffn = jnp.zeros(x1.shape, F32)
    for c0 in range(0, hidden, hidden_chunk):
        gt = _dot(hf, wg_ref[:, c0:c0 + hidden_chunk])
        up = _dot(hf, wu_ref[:, c0:c0 + hidden_chunk])
        act = (gt * _sigmoid(gt) * up).astype(BF16)
        ffn = ffn + _dot(act, wd_ref[c0:c0 + hidden_chunk, :])
    o_ref[0] = x1 + gf_ref[0] * ffn


def out_ffn(mix_a, mix_b, x, mod, ln_ffn, wo_a, wo_b, wg, wu, wd, tm):
    bm, t, d = x.shape
    r = mod.shape[1]
    tm = min(tm, t)
    ka, kb = mix_a.shape[2], mix_b.shape[2]
    hidden = wg.shape[1]
    if r == 1:
        mod_spec = lambda k: pl.BlockSpec((1, 1, d), lambda b, i: (b, 0, k))
    else:
        assert r == t and tm == t
        mod_spec = lambda k: pl.BlockSpec((1, tm, d), lambda b, i: (b, 0, k))
    const = lambda shape: pl.BlockSpec(shape, lambda b, i: (0,) * len(shape), pipeline_mode=pl.Buffered(1))
    kern = functools.partial(_out_ffn_kernel, hidden_chunk=hidden // 2)
    return pl.pallas_call(
        kern,
        grid=(bm, t // tm),
        in_specs=[pl.BlockSpec((1, tm, ka), lambda b, i: (b, i, 0)),
                  pl.BlockSpec((1, tm, kb), lambda b, i: (b, i, 0)),
                  pl.BlockSpec((1, tm, d), lambda b, i: (b, i, 0)),
                  mod_spec(2), mod_spec(3), mod_spec(4), mod_spec(5),
                  const((1, d)), const((ka, d)), const((kb, d)),
                  const((d, hidden)), const((d, hidden)), const((hidden, d))],
        out_specs=pl.BlockSpec((1, tm, d), lambda b, i: (b, i, 0)),
        out_shape=jax.ShapeDtypeStruct((bm, t, d), F32),
        compiler_params=_cparams("arbitrary", "arbitrary"),
        name="out_ffn",
    )(mix_a, mix_b, x, mod, mod, mod, mod, ln_ffn.reshape(1, d), wo_a, wo_b, wg, wu, wd)


def _compress_rows(load, nh, w1_ref, pos_ref, w2_ref):
    halves = []
    for parity in range(2):
        halves.append(jnp.concatenate(
            [load(parity * CMP_BLOCK + l, nh, 2 * CMP_BLOCK) for l in range(CMP_BLOCK)], axis=1))
    xs = (jnp.concatenate(halves, axis=0) + pos_ref[...]).astype(BF16)
    hid = _gelu(_dot(xs, w1_ref[...]))
    return _dot(hid.astype(BF16), w2_ref[...])


def _nsa_compress_kernel(rk_ref, rv_ref, w1k_ref, posk_ref, w2k_ref, w1v_ref, posv_ref, w2v_ref, kg_ref,
                         kc_ref, vc_ref):
    nh = kc_ref.shape[1] // 2
    ldk = lambda s, n, st: rk_ref[0, pl.ds(s, n, stride=st), :]
    ldv = lambda s, n, st: rv_ref[0, pl.ds(s, n, stride=st), :]
    kc = _compress_rows(ldk, nh, w1k_ref, posk_ref, w2k_ref)
    kc_ref[0] = _head_norm(kc, kg_ref[...]).astype(kc_ref.dtype)
    vc_ref[0] = _compress_rows(ldv, nh, w1v_ref, posv_ref, w2v_ref).astype(vc_ref.dtype)


def nsa_compress(rows, cw):
    b, t, _ = rows.shape
    nc = t // CMP_BLOCK
    const = lambda a: pl.BlockSpec(a.shape, lambda i: (0,) * a.ndim)
    ws = (cw["w1k"], cw["posk"], cw["w2k"], cw["w1v"], cw["posv"], cw["w2v"], cw["kg0"])
    return pl.pallas_call(
        _nsa_compress_kernel,
        grid=(b,),
        in_specs=[pl.BlockSpec((1, t, LANES), lambda i: (i, 0, 0)),
                  pl.BlockSpec((1, t, LANES), lambda i: (i, 0, 1))] + [const(a) for a in ws],
        out_specs=[pl.BlockSpec((1, nc, LANES), lambda i: (i, 0, 0))] * 2,
        out_shape=[jax.ShapeDtypeStruct((b, nc, LANES), BF16)] * 2,
        compiler_params=_cparams("arbitrary"),
        name="nsa_compress",
    )(rows, rows, *ws)


def _select_blocks(imp, blk_f, n_pick):
    sel = jnp.zeros(imp.shape, F32)
    work = imp
    for _ in range(n_pick):
        m = jnp.max(work, axis=-1, keepdims=True)
        idx = jnp.min(jnp.where(work == m, blk_f, 1e9), axis=-1, keepdims=True)
        pick = blk_f == idx
        sel = jnp.where(pick, 1.0, sel)
        work = jnp.where(pick, -2.0, work)
    return sel


def _nsa_prompt_kernel(q_ref, kc_ref, vc_ref, ks_ref, vs_ref, kw_ref, vw_ref, gl_ref, o_ref):
    qi = pl.program_id(1)
    t0 = qi * QBLK
    nc = kc_ref.shape[1]
    nb = nc // 2
    scale = HEAD_DIM ** -0.5
    rep = q_ref.shape[2] // LANES
    rows = 2 * rep * QBLK

    lane = lax.broadcasted_iota(jnp.int32, (QBLK, LANES), 1)
    trow = t0 + lax.broadcasted_iota(jnp.int32, (QBLK, LANES), 0)
    q = q_ref[0]
    qs = []
    for g in range(2):
        gm = (lane >= g * HEAD_DIM) & (lane < (g + 1) * HEAD_DIM)
        for r in range(rep):
            qs.append(jnp.where(gm, q[:, r * LANES:(r + 1) * LANES], jnp.zeros((), q.dtype)))
    qall = jnp.concatenate(qs, axis=0)

    s = _dot_nt(qall, kc_ref[0]) * scale
    cl = lax.broadcasted_iota(jnp.int32, (QBLK, nc), 1)
    cidx = jnp.where(cl < nb, 2 * cl, 2 * (cl - nb) + 1)
    tq_c = t0 + lax.broadcasted_iota(jnp.int32, (QBLK, nc), 0)
    cmask1 = ((cidx + 1) * CMP_BLOCK - 1) <= tq_c
    cmask = jnp.concatenate([cmask1] * (2 * rep), axis=0)
    s = jnp.where(cmask, s, NEG_INF)
    m = jnp.max(s, axis=-1, keepdims=True)
    e = jnp.where(cmask, jnp.exp(s - m), 0.0)
    den = jnp.maximum(jnp.sum(e, axis=-1, keepdims=True), 1e-30)
    p = e * (1.0 / den)
    o_cmp = _dot(p.astype(BF16), vc_ref[0])

    bl = lax.broadcasted_iota(jnp.int32, (QBLK, nb), 1)
    tq_b = t0 + lax.broadcasted_iota(jnp.int32, (QBLK, nb), 0)
    forced = (bl == 0) | (bl == tq_b // SEL_BLOCK)
    valid = bl * SEL_BLOCK <= tq_b
    blk_f = bl.astype(F32)
    sels = []
    for g in range(2):
        ps = p[(g * rep) * QBLK:(g * rep + 1) * QBLK]
        for r in range(1, rep):
            ps = ps + p[(g * rep + r) * QBLK:(g * rep + r + 1) * QBLK]
        imp = ps[:, :nb] + ps[:, nb:]
        imp = jnp.where(forced, FORCE_SCORE, imp)
        imp = jnp.where(valid, imp, -1.0)
        sels.append(_select_blocks(imp, blk_f, min(N_SEL, nb)).astype(BF16))

    erow = lax.broadcasted_iota(jnp.int32, (nb, QBLK), 0)
    ecol = lax.broadcasted_iota(jnp.int32, (nb, QBLK), 1) // SEL_BLOCK

    def flash(k_ref, v_ref, lo, hi, mask_fn):
        def body(kt, carry):
            m_i, l_i, acc = carry
            ks = k_ref[0, pl.ds(pl.multiple_of(kt * QBLK, QBLK), QBLK), :]
            vs = v_ref[0, pl.ds(pl.multiple_of(kt * QBLK, QBLK), QBLK), :]
            sc = _dot_nt(qall, ks) * scale
            mk = mask_fn(kt)
            sc = jnp.where(mk, sc, NEG_INF)
            m_new = jnp.maximum(m_i, jnp.max(sc, axis=-1, keepdims=True))
            alpha = jnp.exp(m_i - m_new)
            ee = jnp.where(mk, jnp.exp(sc - m_new), 0.0)
            l_new = alpha * l_i + jnp.sum(ee, axis=-1, keepdims=True)
            acc_new = alpha * acc + _dot(ee.astype(BF16), vs)
            return m_new, l_new, acc_new
        init = (jnp.full((rows, 1), NEG_INF, F32), jnp.zeros((rows, 1), F32), jnp.zeros((rows, LANES), F32))
        _, l_f, acc_f = lax.fori_loop(lo, hi, body, init)
        return acc_f * (1.0 / jnp.maximum(l_f, 1e-30))

    def sel_mask(kt):
        kpos = kt * QBLK + lane
        causal = kpos <= trow
        expand = (erow == 2 * kt + ecol).astype(BF16)
        parts = []
        for g in range(2):
            mg = (_dot(sels[g], expand) > 0.5) & causal
            parts += [mg] * rep
        return jnp.concatenate(parts, axis=0)

    def win_mask(kt):
        dist = trow - (kt * QBLK + lane)
        mw = (dist >= 0) & (dist <= NSA_WINDOW)
        return jnp.concatenate([mw] * (2 * rep), axis=0)

    o_sel = flash(ks_ref, vs_ref, 0, qi + 1, sel_mask)
    o_win = flash(kw_ref, vw_ref, jnp.maximum(qi - NSA_WINDOW // QBLK, 0), qi + 1, win_mask)

    gate = _sigmoid(gl_ref[0])
    outs = []
    for r in range(rep):
        og = []
        for g in range(2):
            c = (g * rep + r) * 3
            sl = slice((g * rep + r) * QBLK, (g * rep + r + 1) * QBLK)
            og.append(gate[:, c:c + 1] * o_cmp[sl] + gate[:, c + 1:c + 2] * o_sel[sl]
                      + gate[:, c + 2:c + 3] * o_win[sl])
        outs.append(jnp.where(lane < HEAD_DIM, og[0], og[1]))
    o_ref[0] = jnp.concatenate(outs, axis=1).astype(o_ref.dtype)


def nsa_prompt_attn(q_bf, kc, vc, rows_bf, win_bf, gl):
    b, t, qw = q_bf.shape
    nc = kc.shape[1]
    nq = t // QBLK
    full = lambda k: pl.BlockSpec((1, t, LANES), lambda i, j: (i, 0, k))
    return pl.pallas_call(
        _nsa_prompt_kernel,
        grid=(b, nq),
        in_specs=[pl.BlockSpec((1, QBLK, qw), lambda i, j: (i, j, 0)),
                  pl.BlockSpec((1, nc, LANES), lambda i, j: (i, 0, 0)),
                  pl.BlockSpec((1, nc, LANES), lambda i, j: (i, 0, 0)),
                  full(2), full(3), full(0), full(1),
                  pl.BlockSpec((1, QBLK, LANES), lambda i, j: (i, j, 0))],
        out_specs=pl.BlockSpec((1, QBLK, qw), lambda i, j: (i, j, 0)),
        out_shape=jax.ShapeDtypeStruct((b, t, qw), BF16),
        compiler_params=_cparams("arbitrary", "arbitrary"),
        name="nsa_prompt_attn",
    )(q_bf, kc, vc, rows_bf, rows_bf, win_bf, win_bf, gl)


def _rglru_gates(xc, wa_ref, ba_ref, wx_ref, bx_ref, lam_ref):
    xb = xc.astype(BF16)
    r = _sigmoid(_dot(xb, wa_ref[...]) + ba_ref[...])
    i = _sigmoid(_dot(xb, wx_ref[...]) + bx_ref[...])
    nl = -lam_ref[...]
    softplus = jnp.maximum(nl, 0.0) + jnp.log1p(jnp.exp(-jnp.abs(nl)))
    log_a = -RG_C * r * softplus
    a = jnp.exp(log_a)
    u = jnp.sqrt(-jnp.tanh(log_a) * (a * a + 1.0)) * (i * xc)
    return a, u


def _rglru_prompt_kernel(xr_ref, gr_ref, cw_ref, cb_ref, wa_ref, ba_ref, wx_ref, bx_ref, lam_ref,
                         o_ref, hl_ref, cl_ref, hcar, xcar):
    i = pl.program_id(1)
    tm, c = xr_ref.shape[1], xr_ref.shape[2]

    @pl.when(i == 0)
    def _():
        hcar[...] = jnp.zeros(hcar.shape, F32)
        xcar[...] = jnp.zeros(xcar.shape, F32)

    x = xr_ref[0]
    prev = xcar[...]
    row = lax.broadcasted_iota(jnp.int32, (tm, c), 0)
    xc = cb_ref[...] + cw_ref[3:4, :] * x
    for k in range(1, 4):
        cur = pltpu.roll(x, k, axis=0)
        old = jnp.tile(pltpu.roll(prev, k, axis=0), (tm // 8, 1))
        xc = xc + cw_ref[3 - k:4 - k, :] * jnp.where(row < k, old, cur)
    a, u = _rglru_gates(xc, wa_ref, ba_ref, wx_ref, bx_ref, lam_ref)

    s = 1
    while s < tm:
        a_sh = jnp.where(row >= s, pltpu.roll(a, s, axis=0), 1.0)
        u_sh = jnp.where(row >= s, pltpu.roll(u, s, axis=0), 0.0)
        u = a * u_sh + u
        a = a * a_sh
        s *= 2
    h = a * hcar[0:1, :] + u
    o_ref[0] = (h * _gelu(gr_ref[0])).astype(o_ref.dtype)
    hcar[...] = jnp.broadcast_to(h[tm - 1:tm, :], hcar.shape)
    xcar[...] = x[tm - 8:tm, :]
    hl_ref[0] = hcar[...]
    cl_ref[0] = xcar[...]


def rglru_prompt(xr, gr, rw, tm):
    b, t, c = xr.shape
    tm = min(tm, t)
    const = lambda a: pl.BlockSpec(a.shape, lambda i, j: (0,) * a.ndim)
    ws = (rw["conv_w"], rw["conv_b"], rw["wa"], rw["ba"], rw["wx"], rw["bx"], rw["lam"])
    tile = pl.BlockSpec((1, tm, c), lambda i, j: (i, j, 0))
    last = pl.BlockSpec((1, 8, c), lambda i, j: (i, 0, 0))
    return pl.pallas_call(
        _rglru_prompt_kernel,
        grid=(b, t // tm),
        in_specs=[tile, tile] + [const(a) for a in ws],
        out_specs=[tile, last, last],
        out_shape=[jax.ShapeDtypeStruct((b, t, c), BF16),
                   jax.ShapeDtypeStruct((b, 8, c), F32),
                   jax.ShapeDtypeStruct((b, 8, c), F32)],
        scratch_shapes=[pltpu.VMEM((8, c), F32), pltpu.VMEM((8, c), F32)],
        compiler_params=_cparams("arbitrary", "arbitrary"),
        name="rglru_prompt",
    )(xr, gr, *ws)


def _gmlp_v(v_raw, vg_ref):
    v = _gelu(v_raw)
    ms = jnp.mean(v * v, axis=-1, keepdims=True)
    return v * lax.rsqrt(ms + RMS_EPS) * vg_ref[...]


def _gmlp_prompt_kernel(u_ref, v_ref, vg_ref, ws_ref, bs_ref, o_ref):
    lc = u_ref.shape[1]
    v = _gmlp_v(v_ref[0], vg_ref).astype(BF16)
    tril = (lax.broadcasted_iota(jnp.int32, (lc, lc), 0) >= lax.broadcasted_iota(jnp.int32, (lc, lc), 1))
    lane = lax.broadcasted_iota(jnp.int32, (lc, LANES), 1)
    parts = []
    for j in range(v.shape[1] // LANES):
        vj = v[:, j * LANES:(j + 1) * LANES]
        w0 = jnp.where(tril, ws_ref[2 * j], 0.0).astype(BF16)
        w1 = jnp.where(tril, ws_ref[2 * j + 1], 0.0).astype(BF16)
        parts.append(jnp.where(lane < HEAD_DIM, _dot(w0, vj), _dot(w1, vj)))
    mixed = jnp.concatenate(parts, axis=1) + bs_ref[...]
    o_ref[0] = (_gelu(u_ref[0]) * mixed).astype(o_ref.dtype)


def gmlp_prompt(u, v, v_gain, ws, bs_exp):
    b, t, c = u.shape
    lc = ws.shape[1]
    tile = pl.BlockSpec((1, lc, c), lambda i, j: (i, j, 0))
    const = lambda a: pl.BlockSpec(a.shape, lambda i, j: (0,) * a.ndim)
    return pl.pallas_call(
        _gmlp_prompt_kernel,
        grid=(b, t // lc),
        in_specs=[tile, tile, const(v_gain), const(ws), const(bs_exp)],
        out_specs=tile,
        out_shape=jax.ShapeDtypeStruct((b, t, c), BF16),
        compiler_params=_cparams("arbitrary", "arbitrary"),
        name="gmlp_prompt",
    )(u, v, v_gain, ws, bs_exp)


def _dil_count(dist):
    cnt = jnp.zeros(dist.shape, F32)
    for window, dil in DIL_PATTERNS:
        hit = (dist >= 0) & (dist <= window) & ((dist & (dil - 1)) == 0)
        cnt = cnt + jnp.where(hit, 1.0, 0.0)
    return cnt


def _dil_prompt_kernel(q_ref, k_ref, v_ref, o_ref):
    qi = pl.program_id(2)
    t0 = qi * QBLK
    scale = HEAD_DIM ** -0.5
    lane = lax.broadcasted_iota(jnp.int32, (QBLK, LANES), 1)
    trow = t0 + lax.broadcasted_iota(jnp.int32, (QBLK, LANES), 0)
    q = q_ref[0]
    zero = jnp.zeros((), q.dtype)
    q2 = jnp.concatenate([jnp.where(lane < HEAD_DIM, q, zero), jnp.where(lane < HEAD_DIM, zero, q)], axis=0)

    def body(kt, carry):
        m_i, l_i, acc = carry
        ks = k_ref[0, pl.ds(pl.multiple_of(kt * QBLK, QBLK), QBLK), :]
        vs = v_ref[0, pl.ds(pl.multiple_of(kt * QBLK, QBLK), QBLK), :]
        sc = _dot_nt(q2, ks) * scale
        cnt1 = _dil_count(trow - (kt * QBLK + lane))
        cnt = jnp.concatenate([cnt1, cnt1], axis=0)
        mk = cnt > 0.0
        sc = jnp.where(mk, sc, NEG_INF)
        m_new = jnp.maximum(m_i, jnp.max(sc, axis=-1, keepdims=True))
        alpha = jnp.exp(m_i - m_new)
        ee = cnt * jnp.where(mk, jnp.exp(sc - m_new), 0.0)
        l_new = alpha * l_i + jnp.sum(ee, axis=-1, keepdims=True)
        acc_new = alpha * acc + _dot(ee.astype(BF16), vs)
        return m_new, l_new, acc_new

    init = (jnp.full((2 * QBLK, 1), NEG_INF, F32), jnp.zeros((2 * QBLK, 1), F32),
            jnp.zeros((2 * QBLK, LANES), F32))
    _, l_f, acc_f = lax.fori_loop(jnp.maximum(qi - DIL_MAX // QBLK, 0), qi + 1, body, init)
    o = acc_f * (1.0 / jnp.maximum(l_f, 1e-30))
    o_ref[0] = jnp.where(lane < HEAD_DIM, o[:QBLK], o[QBLK:]).astype(o_ref.dtype)


def dil_prompt_attn(q_bf, kv_bf):
    b, t, w = q_bf.shape
    npair = w // LANES
    return pl.pallas_call(
        _dil_prompt_kernel,
        grid=(b, npair, t // QBLK),
        in_specs=[pl.BlockSpec((1, QBLK, LANES), lambda i, p, j: (i, j, p)),
                  pl.BlockSpec((1, t, LANES), lambda i, p, j: (i, 0, p)),
                  pl.BlockSpec((1, t, LANES), lambda i, p, j: (i, 0, npair + p))],
        out_specs=pl.BlockSpec((1, QBLK, LANES), lambda i, p, j: (i, j, p)),
        out_shape=jax.ShapeDtypeStruct((b, t, w), BF16),
        compiler_params=_cparams("arbitrary", "arbitrary", "arbitrary"),
        name="dil_prompt_attn",
    )(q_bf, kv_bf, kv_bf)


NSA_G, NSA_REP = 2, 4
NSA_QW = NSA_G * NSA_REP * HEAD_DIM
QPERM = np.arange(NSA_QW).reshape(NSA_G, NSA_REP, HEAD_DIM).transpose(1, 0, 2).reshape(-1)


def _block_diag(blocks):
    n, a, b = blocks.shape
    out = jnp.zeros((n, a, n, b), blocks.dtype)
    out = out.at[jnp.arange(n), :, jnp.arange(n), :].set(blocks)
    return out.reshape(n * a, n * b)


def prep_layer0(p):
    w_in = p["w_in"]
    d = w_in.shape[0]
    n_gate = NSA_G * NSA_REP * 3
    kv_w = 6 * NSA_G * HEAD_DIM
    c_gl = NSA_QW + kv_w
    c_xr = c_gl + n_gate
    d_rnn = (w_in.shape[1] - c_xr) // 2
    cols = np.concatenate([QPERM, np.arange(NSA_QW, c_gl), np.arange(c_xr, c_xr + 2 * d_rnn),
                           np.arange(c_gl, c_xr)])
    w = jnp.concatenate([w_in[:, cols], jnp.zeros((d, LANES - n_gate), w_in.dtype)], axis=1).astype(BF16)
    n = w.shape[1]
    hg = jnp.ones((n,), F32)
    hg = hg.at[0:NSA_QW].set(jnp.tile(p["q_gain"], NSA_QW // HEAD_DIM))
    hg = hg.at[NSA_QW + 2 * LANES:NSA_QW + 3 * LANES].set(jnp.tile(p["k_gain"][1], 2))
    hg = hg.at[NSA_QW + 4 * LANES:NSA_QW + 5 * LANES].set(jnp.tile(p["k_gain"][2], 2))
    qb = NSA_QW // LANES
    c_rows, c_win, c_x, c_g, c_l = NSA_QW, NSA_QW + 4 * LANES, NSA_QW + 6 * LANES, NSA_QW + 6 * LANES + d_rnn, \
        NSA_QW + 6 * LANES + 2 * d_rnn
    outs = ((0, NSA_QW, BF16), (c_rows, 4 * LANES, F32), (c_rows, 4 * LANES, BF16), (c_win, 2 * LANES, F32),
            (c_win, 2 * LANES, BF16), (c_x, d_rnn, F32), (c_g, d_rnn, F32), (c_l, LANES, F32))
    cw = {}
    for c, nm in enumerate("kv"):
        w1 = p["cmp_w1"][c]
        big = jnp.zeros((CMP_BLOCK, NSA_G, HEAD_DIM, NSA_G, HEAD_DIM), F32)
        for g in range(NSA_G):
            big = big.at[:, g, :, g, :].set(w1)
        cw["w1" + nm] = big.reshape(CMP_BLOCK * LANES, LANES).astype(BF16)
        cw["pos" + nm] = jnp.tile(p["cmp_pos"][c], (1, NSA_G)).reshape(1, CMP_BLOCK * LANES)
        cw["w2" + nm] = _block_diag(jnp.stack([p["cmp_w2"][c]] * NSA_G)).astype(BF16)
    cw["kg0"] = jnp.tile(p["k_gain"][0], 2).reshape(1, LANES)
    rw = dict(conv_w=p["conv_w"], conv_b=p["conv_b"].reshape(1, -1),
              wa=_block_diag(p["wa"]).astype(BF16), ba=p["ba"].reshape(1, -1),
              wx=_block_diag(p["wx"]).astype(BF16), bx=p["bx"].reshape(1, -1), lam=p["lam"].reshape(1, -1))
    w_out = p["w_out"]
    return dict(w=w, hg=hg.reshape(1, n), norm_blocks=tuple(range(qb)) + (qb + 2, qb + 4), outs=outs,
                cw=cw, rw=rw, wo_a=w_out[QPERM].astype(BF16), wo_b=w_out[NSA_QW:].astype(BF16))


def prep_layer1(p):
    w = p["w_in"].astype(BF16)
    n = w.shape[1]
    c_w = p["v_gain"].shape[0]
    dil_w = (n - 2 * c_w) // 3
    hg = jnp.ones((n,), F32)
    hg = hg.at[2 * c_w:2 * c_w + dil_w].set(jnp.tile(p["q_gain"], dil_w // HEAD_DIM))
    hg = hg.at[2 * c_w + dil_w:2 * c_w + 2 * dil_w].set(jnp.tile(p["k_gain"], dil_w // HEAD_DIM))
    b0 = 2 * c_w // LANES
    nbq = dil_w // LANES
    outs = ((0, c_w, F32), (c_w, c_w, F32), (2 * c_w, dil_w, BF16), (2 * c_w + dil_w, 2 * dil_w, F32),
            (2 * c_w + dil_w, 2 * dil_w, BF16))
    w_out = p["w_out"]
    gw = c_w // p["ws"].shape[0]
    return dict(w=w, hg=hg.reshape(1, n), norm_blocks=tuple(range(b0, b0 + 2 * nbq)), outs=outs,
                v_gain=p["v_gain"].reshape(1, c_w), ws=p["ws"], bs_exp=jnp.repeat(p["bs"].T, gw, axis=1),
                ws_diag=jnp.repeat(p["ws"][:, 0, 0], gw).reshape(1, c_w),
                bs0=jnp.repeat(p["bs"][:, 0], gw).reshape(1, c_w),
                wo_a=w_out[:c_w].astype(BF16), wo_b=w_out[c_w:].astype(BF16))


def layer0_prompt(x, mod, ln_mix, ln_ffn, ffn_w, pp):
    b, t, _ = x.shape
    q_bf, rows, rows_bf, win, win_bf, xr, gr, gl = in_proj(
        x, mod, 0, 1, ln_mix, pp["w"], pp["hg"], pp["norm_blocks"], pp["outs"], tm=512)
    kc, vc = nsa_compress(rows, pp["cw"])
    o_nsa = nsa_prompt_attn(q_bf, kc, vc, rows_bf, win_bf, gl)
    o_rnn, h_last, conv_last = rglru_prompt(xr, gr, pp["rw"], tm=256)
    y = out_ffn(o_nsa, o_rnn, x, mod, ln_ffn, pp["wo_a"], pp["wo_b"], *ffn_w, tm=512)
    nwin = min(NSA_WINDOW, t)
    state = (rows.reshape(b, t, 4, NSA_G, HEAD_DIM), win[:, t - nwin:].reshape(b, nwin, 2, NSA_G, HEAD_DIM),
             h_last[:, 0], conv_last[:, 5:8])
    return y, state


def layer1_prompt(x, mod, ln_mix, ln_ffn, ffn_w, pp):
    b, t, _ = x.shape
    u, v, q_bf, kv, kv_bf = in_proj(x, mod, 0, 1, ln_mix, pp["w"], pp["hg"], pp["norm_blocks"], pp["outs"], tm=512)
    o_c = gmlp_prompt(u, v, pp["v_gain"], pp["ws"], pp["bs_exp"])
    o_d = dil_prompt_attn(q_bf, kv_bf)
    y = out_ffn(o_c, o_d, x, mod, ln_ffn, pp["wo_a"], pp["wo_b"], *ffn_w, tm=512)
    nkv = min(DIL_MAX, t)
    heads = kv.shape[2] // (2 * HEAD_DIM)
    return y, kv[:, t - nkv:].reshape(b, nkv, 2, heads, HEAD_DIM)


def _row_softmax_parts(s_list, mask_list, s_new):
    m = s_new[0]
    for sn in s_new[1:]:
        m = jnp.maximum(m, sn)
    for s, mk in zip(s_list, mask_list):
        sm = s if mk is None else jnp.where(mk, s, NEG_INF)
        m = jnp.maximum(m, jnp.max(sm, axis=-1, keepdims=True))
    es, den = [], 0.0
    for s, mk in zip(s_list, mask_list):
        e = jnp.exp(s - m)
        if mk is not None:
            e = jnp.where(mk, e, 0.0)
        es.append(e)
        den = den + jnp.sum(e, axis=-1, keepdims=True)
    en = [jnp.exp(sn - m) for sn in s_new]
    for e in en:
        den = den + e
    return es, en, 1.0 / den


def _nsa_decode_kernel(pt_ref, cache_ref, q_ref, rn_ref, wn_ref, gl_ref, sw_ref,
                       w1k_ref, posk_ref, w2k_ref, w1v_ref, posv_ref, w2v_ref, kg_ref, ex_ref,
                       o_ref, wo_ref, bufk, bufv, bufs, sem):
    b = pl.program_id(0)
    nbatch = pl.num_programs(0)
    slot = b % 2
    n_pages = pt_ref.shape[1]
    page = cache_ref.shape[1]
    past = n_pages * page
    nc = past // CMP_BLOCK
    nb = past // SEL_BLOCK
    scale = HEAD_DIM ** -0.5
    rep = q_ref.shape[2] // LANES
    nh = 2 * rep

    def copies(bb, sl):
        out = []
        for j in range(n_pages):
            pg = pt_ref[bb, j]
            dst = pl.ds(j * page, page)
            out.append(pltpu.make_async_copy(cache_ref.at[pg, :, pl.ds(0, LANES)], bufk.at[sl, dst], sem.at[sl, 0, j]))
            out.append(pltpu.make_async_copy(cache_ref.at[pg, :, pl.ds(LANES, LANES)], bufv.at[sl, dst],
                                             sem.at[sl, 1, j]))
            out.append(pltpu.make_async_copy(cache_ref.at[pg, :, pl.ds(2 * LANES, 2 * LANES)], bufs.at[sl, dst],
                                             sem.at[sl, 2, j]))
        return out

    @pl.when(b == 0)
    def _():
        for c in copies(0, 0):
            c.start()

    @pl.when(b + 1 < nbatch)
    def _():
        for c in copies(b + 1, 1 - slot):
            c.start()

    for c in copies(b, slot):
        c.wait()

    lane = lax.broadcasted_iota(jnp.int32, (nh, LANES), 1)
    hrow = lax.broadcasted_iota(jnp.int32, (nh, LANES), 0)
    q = q_ref[0]
    qrows = []
    for g in range(2):
        for r in range(rep):
            qrows.append(q[:, r * LANES:(r + 1) * LANES])
    qm = jnp.concatenate(qrows, axis=0)
    qm = jnp.where((lane // HEAD_DIM) == (hrow // rep), qm, jnp.zeros((), qm.dtype))
    qf = qm.astype(F32)

    def padded(c2):
        z = jnp.zeros((LANES - nc // 2, LANES), F32)
        return jnp.concatenate([c2[:nc // 2], z, c2[nc // 2:], z], axis=0)
    ldk = lambda s, n, st: bufk[slot, pl.ds(s, n, stride=st), :]
    ldv = lambda s, n, st: bufv[slot, pl.ds(s, n, stride=st), :]
    kc = _head_norm(_compress_rows(ldk, nc // 2, w1k_ref, posk_ref, w2k_ref), kg_ref[...])
    vc = _compress_rows(ldv, nc // 2, w1v_ref, posv_ref, w2v_ref)
    kcp = padded(kc).astype(BF16)
    vcp = padded(vc).astype(BF16)
    s_c = _dot_nt(qm, kcp) * scale
    cl = lax.broadcasted_iota(jnp.int32, (nh, 2 * LANES), 1)
    cmask = (cl % LANES) < nc // 2
    s_c = jnp.where(cmask, s_c, NEG_INF)
    e_c = jnp.where(cmask, jnp.exp(s_c - jnp.max(s_c, axis=-1, keepdims=True)), 0.0)
    p_c = e_c * (1.0 / jnp.maximum(jnp.sum(e_c, axis=-1, keepdims=True), 1e-30))
    o_cmp = _dot(p_c.astype(BF16), vcp)

    pp = p_c[:, :LANES] + p_c[:, LANES:]
    imps = []
    for g in range(2):
        ig = pp[g * rep:g * rep + 1]
        for r in range(1, rep):
            ig = ig + pp[g * rep + r:g * rep + r + 1]
        imps += [ig] * rep
    imp = jnp.concatenate(imps, axis=0)
    imp = jnp.where((lane == 0) | (lane == nb), FORCE_SCORE, imp)
    imp = jnp.where(lane <= nb, imp, -3.0)
    sel = _select_blocks(imp, lane.astype(F32), min(N_SEL, nb + 1))
    selm = _dot(sel.astype(BF16), ex_ref[...]) > 0.5

    rn = rn_ref[0]
    s_sel = _dot_nt(qm, bufs[slot, :, pl.ds(0, LANES)].astype(BF16)) * scale
    s_sel_new = jnp.sum(qf * rn[:, 2 * LANES:3 * LANES], axis=-1, keepdims=True) * scale
    (e_s,), (e_sn,), inv_s = _row_softmax_parts([s_sel], [selm], [s_sel_new])
    o_sel = (_dot(e_s.astype(BF16), bufs[slot, :, pl.ds(LANES, LANES)].astype(BF16))
             + e_sn * rn[:, 3 * LANES:4 * LANES]) * inv_s

    sw = sw_ref[0]
    wn = wn_ref[0]
    s_w = _dot_nt(qm, sw[:, :LANES].astype(BF16)) * scale
    s_w_new = jnp.sum(qf * wn[:, :LANES], axis=-1, keepdims=True) * scale
    (e_w,), (e_wn,), inv_w = _row_softmax_parts([s_w], [None], [s_w_new])
    o_win = (_dot(e_w.astype(BF16), sw[:, LANES:].astype(BF16)) + e_wn * wn[:, LANES:]) * inv_w

    gate = _sigmoid(gl_ref[0])
    o = gate[:, 0:1] * o_cmp + gate[:, 1:2] * o_sel + gate[:, 2:3] * o_win
    l1 = lax.broadcasted_iota(jnp.int32, (1, LANES), 1)
    o_ref[0] = jnp.concatenate([jnp.where(l1 < HEAD_DIM, o[r:r + 1], o[rep + r:rep + r + 1])
                                for r in range(rep)], axis=1).astype(o_ref.dtype)

    wb = sw.shape[0]
    wrow = lax.broadcasted_iota(jnp.int32, sw.shape, 0)
    wo_ref[0] = jnp.where(wrow == wb - 1, wn, pltpu.roll(sw, wb - 1, axis=0))


def nsa_decode(page_table, cache, q_bf, rows_new, win_new, gl3, state_win, cw):
    b, n_pages = page_table.shape
    page = cache.shape[1]
    past = n_pages * page
    nb = past // SEL_BLOCK
    assert past % SEL_BLOCK == 0 and nb < LANES and past // CMP_BLOCK <= 2 * LANES
    wb = state_win.shape[1]
    assert wb <= NSA_WINDOW
    expand = (jnp.arange(LANES)[:, None] == (jnp.arange(past) // SEL_BLOCK)[None, :]).astype(BF16)
    ws = (cw["w1k"], cw["posk"], cw["w2k"], cw["w1v"], cw["posv"], cw["w2v"], cw["kg0"], expand)
    per_b = lambda a: pl.BlockSpec((1,) + a.shape[1:], lambda i, pt: (i, 0, 0))
    const = lambda a: pl.BlockSpec(a.shape, lambda i, pt: (0,) * a.ndim)
    grid_spec = pltpu.PrefetchScalarGridSpec(
        num_scalar_prefetch=1,
        grid=(b,),
        in_specs=[pl.BlockSpec(memory_space=pl.ANY), per_b(q_bf), per_b(rows_new), per_b(win_new), per_b(gl3),
                  per_b(state_win)] + [const(a) for a in ws],
        out_specs=[pl.BlockSpec((1, 1, q_bf.shape[2]), lambda i, pt: (i, 0, 0)),
                   pl.BlockSpec((1, wb, state_win.shape[2]), lambda i, pt: (i, 0, 0))],
        scratch_shapes=[pltpu.VMEM((2, past, LANES), F32), pltpu.VMEM((2, past, LANES), F32),
                        pltpu.VMEM((2, past, 2 * LANES), F32), pltpu.SemaphoreType.DMA((2, 3, n_pages))],
    )
    return pl.pallas_call(
        _nsa_decode_kernel,
        grid_spec=grid_spec,
        out_shape=[jax.ShapeDtypeStruct((b, 1, q_bf.shape[2]), BF16),
                   jax.ShapeDtypeStruct(state_win.shape, F32)],
        compiler_params=_cparams("arbitrary"),
        name="nsa_decode",
    )(page_table, cache, q_bf, rows_new, win_new, gl3, state_win, *ws)


def _rglru_decode_kernel(xr_ref, gr_ref, cs_ref, h0_ref, cw_ref, cb_ref, wa_ref, ba_ref, wx_ref, bx_ref, lam_ref,
                         o_ref, h_ref, cn_ref):
    x = xr_ref[...]
    xc = cb_ref[...] + cw_ref[3:4, :] * x
    for k in range(3):
        xc = xc + cw_ref[k:k + 1, :] * cs_ref[k]
    a, u = _rglru_gates(xc, wa_ref, ba_ref, wx_ref, bx_ref, lam_ref)
    h = a * h0_ref[...] + u
    h_ref[...] = h
    o_ref[...] = (h * _gelu(gr_ref[...])).astype(o_ref.dtype)
    cn_ref[0] = cs_ref[1]
    cn_ref[1] = cs_ref[2]
    cn_ref[2] = x


def rglru_decode(xr, gr, conv_t, h0, rw):
    ws = (rw["conv_w"], rw["conv_b"], rw["wa"], rw["ba"], rw["wx"], rw["bx"], rw["lam"])
    return pl.pallas_call(
        _rglru_decode_kernel,
        out_shape=[jax.ShapeDtypeStruct(xr.shape, BF16), jax.ShapeDtypeStruct(xr.shape, F32),
                   jax.ShapeDtypeStruct(conv_t.shape, F32)],
        compiler_params=pltpu.CompilerParams(vmem_limit_bytes=VMEM_LIMIT),
        name="rglru_decode",
    )(xr, gr, conv_t, h0, *ws)


def _gmlp_decode_kernel(u_ref, v_ref, vg_ref, wd_ref, b0_ref, o_ref, vn_ref):
    v = _gmlp_v(v_ref[...], vg_ref)
    vn_ref[...] = v
    o_ref[...] = (_gelu(u_ref[...]) * (wd_ref[...] * v + b0_ref[...])).astype(o_ref.dtype)


def gmlp_decode(u, v, v_gain, ws_diag, bs0):
    return pl.pallas_call(
        _gmlp_decode_kernel,
        out_shape=[jax.ShapeDtypeStruct(u.shape, BF16), jax.ShapeDtypeStruct(u.shape, F32)],
        compiler_params=pltpu.CompilerParams(vmem_limit_bytes=VMEM_LIMIT),
        name="gmlp_decode",
    )(u, v, v_gain, ws_diag, bs0)


def _dil_decode_kernel(q_ref, kvn_ref, st_ref, *rest):
    n_pat = len(DIL_PATTERNS)
    pat_refs, (o_ref, so_ref) = rest[:n_pat], rest[n_pat:]
    wb = st_ref.shape[1]
    w = q_ref.shape[2]
    nh = w // HEAD_DIM
    scale = HEAD_DIM ** -0.5
    lane = lax.broadcasted_iota(jnp.int32, (nh, w), 1)
    hrow = lax.broadcasted_iota(jnp.int32, (nh, w), 0)
    own = (lane // HEAD_DIM) == hrow
    q = q_ref[0]
    qm = jnp.where(own, jnp.concatenate([q] * nh, axis=0), jnp.zeros((), q.dtype))
    kvn = kvn_ref[0]
    s_new = jnp.sum(qm.astype(F32) * kvn[:, :w], axis=-1, keepdims=True) * scale
    s_list, v_list = [], []
    for p_ref in pat_refs:
        s_list.append(_dot_nt(qm, p_ref[0, :, pl.ds(0, w)].astype(BF16)) * scale)
        v_list.append(p_ref[0, :, pl.ds(w, w)].astype(BF16))
    es, en, inv = _row_softmax_parts(s_list, [None] * n_pat, [s_new] * n_pat)
    acc = sum(en) * kvn[:, w:]
    for e, v in zip(es, v_list):
        acc = acc + _dot(e.astype(BF16), v)
    o = acc * inv
    o_ref[0] = jnp.sum(jnp.where(own, o, 0.0), axis=0, keepdims=True).astype(o_ref.dtype)

    chunk = 256
    for c0 in range(0, wb, chunk):
        if c0 + chunk < wb:
            so_ref[0, pl.ds(c0, chunk), :] = st_ref[0, pl.ds(c0 + 1, chunk), :]
        else:
            so_ref[0, pl.ds(c0, chunk - 1), :] = st_ref[0, pl.ds(c0 + 1, chunk - 1), :]
            so_ref[0, pl.ds(wb - 1, 1), :] = kvn


def dil_decode(q_bf, kv_new, state):
    b, wb, w2 = state.shape
    assert wb == DIL_MAX
    w = w2 // 2
    views, view_specs = [], []
    for window, dil in DIL_PATTERNS:
        n = window // dil
        assert (wb // dil) % n == 0
        views.append(state.reshape(b, wb // dil, dil * w2))
        view_specs.append(pl.BlockSpec((1, n, w2), lambda i, blk=wb // dil // n - 1: (i, blk, 0)))
    return pl.pallas_call(
        _dil_decode_kernel,
        grid=(b,),
        in_specs=[pl.BlockSpec((1, 1, w), lambda i: (i, 0, 0)),
                  pl.BlockSpec((1, 1, w2), lambda i: (i, 0, 0)),
                  pl.BlockSpec((1, wb, w2), lambda i: (i, 0, 0))] + view_specs,
        out_specs=[pl.BlockSpec((1, 1, w), lambda i: (i, 0, 0)),
                   pl.BlockSpec((1, wb, w2), lambda i: (i, 0, 0))],
        out_shape=[jax.ShapeDtypeStruct((b, 1, w), BF16), jax.ShapeDtypeStruct(state.shape, F32)],
        compiler_params=_cparams("arbitrary"),
        name="dil_decode",
    )(q_bf, kv_new, state, *views)


def layer0_sample(x, mod, ln_mix, ln_ffn, ffn_w, pp, cache, page_table, state_win, state_h, state_conv):
    b, _, d = x.shape
    xs = x.reshape(1, b, d)
    q_bf, rows, _, win, _, xr, gr, gl = in_proj(
        xs, mod, 0, 1, ln_mix, pp["w"], pp["hg"], pp["norm_blocks"], pp["outs"], tm=b)
    n_gate = NSA_G * NSA_REP * 3
    gl3 = gl[0, :, :n_gate].reshape(b, NSA_G * NSA_REP, 3)
    wb = state_win.shape[1]
    o_nsa, win_out = nsa_decode(
        page_table, cache.reshape(cache.shape[0], cache.shape[1], -1), q_bf.reshape(b, 1, -1),
        rows.reshape(b, 1, -1), win.reshape(b, 1, -1), gl3, state_win.reshape(b, wb, -1), pp["cw"])
    o_rnn, h_new, conv_new = rglru_decode(xr[0], gr[0], state_conv.transpose(1, 0, 2), state_h, pp["rw"])
    y = out_ffn(o_nsa.reshape(1, b, -1), o_rnn[None], xs, mod, ln_ffn, pp["wo_a"], pp["wo_b"], *ffn_w, tm=b)
    state = (rows.reshape(b, 1, 4, NSA_G, HEAD_DIM), win_out.reshape(b, wb, 2, NSA_G, HEAD_DIM), h_new,
             conv_new.transpose(1, 0, 2))
    return y.reshape(b, 1, d), state


def layer1_sample(x, mod, ln_mix, ln_ffn, ffn_w, pp, state_dil):
    b, _, d = x.shape
    xs = x.reshape(1, b, d)
    u, v, q_bf, kv, _ = in_proj(xs, mod, 0, 1, ln_mix, pp["w"], pp["hg"], pp["norm_blocks"], pp["outs"], tm=b)
    o_c, v_n = gmlp_decode(u[0], v[0], pp["v_gain"], pp["ws_diag"], pp["bs0"])
    wb, heads = state_dil.shape[1], state_dil.shape[3]
    o_d, dil_out = dil_decode(q_bf.reshape(b, 1, -1), kv.reshape(b, 1, -1), state_dil.reshape(b, wb, -1))
    y = out_ffn(o_c[None], o_d.reshape(1, b, -1), xs, mod, ln_ffn, pp["wo_a"], pp["wo_b"], *ffn_w, tm=b)
    return y.reshape(b, 1, d), (dil_out.reshape(b, wb, 2, heads, HEAD_DIM), v_n.reshape(b, 1, -1))


def kernel(x_prompt, x_sample, cache_nsa_kv, state_nsa_win, state_rglru_h, state_rglru_conv, state_dil_kv, page_table, c_prompt, c_sample, norm_mix_g, norm_ffn_g, w_ada, b_ada, w_ffn_gate, w_ffn_up, w_ffn_down, w_in_ab, w_out_ab, nsa_q_gain, nsa_k_gain, nsa_cmp_w1, nsa_cmp_w2, nsa_cmp_pos, rg_conv_w, rg_conv_b, rg_wa, rg_ba, rg_wx, rg_bx, rg_lambda, w_in_cd, w_out_cd, gmlp_v_gain, gmlp_ws, gmlp_bs, dil_q_gain, dil_k_gain):
    depth = norm_mix_g.shape[0]
    bp, bs = x_prompt.shape[0], x_sample.shape[0]
    pad = -(bp + bs) % 8
    c_all = jnp.concatenate([c_prompt, c_sample, jnp.zeros((pad, c_prompt.shape[1]), F32)], axis=0)
    mod_all = ada_mod(c_all, w_ada.astype(BF16), b_ada)
    yp, ys = x_prompt, x_sample
    kv_p, kv_s, win_p, win_s, h_p, h_s, conv_p, conv_s, dil_p, dil_s, gv_s = ([] for _ in range(11))
    for layer in range(depth):
        i = layer // 2
        mod_p = mod_all[layer, :bp, None, :]
        mod_s = mod_all[layer, None, bp:bp + bs, :]
        ffn_w = (w_ffn_gate[layer].astype(BF16), w_ffn_up[layer].astype(BF16), w_ffn_down[layer].astype(BF16))
        if layer % 2 == 0:
            pp = prep_layer0(dict(w_in=w_in_ab[i], w_out=w_out_ab[i], q_gain=nsa_q_gain[i], k_gain=nsa_k_gain[i],
                                  cmp_w1=nsa_cmp_w1[i], cmp_w2=nsa_cmp_w2[i], cmp_pos=nsa_cmp_pos[i],
                                  conv_w=rg_conv_w[i], conv_b=rg_conv_b[i], wa=rg_wa[i], ba=rg_ba[i],
                                  wx=rg_wx[i], bx=rg_bx[i], lam=rg_lambda[i]))
            yp, st = layer0_prompt(yp, mod_p, norm_mix_g[layer], norm_ffn_g[layer], ffn_w, pp)
            kv_p.append(st[0]); win_p.append(st[1]); h_p.append(st[2]); conv_p.append(st[3])
            ys, st = layer0_sample(ys, mod_s, norm_mix_g[layer], norm_ffn_g[layer], ffn_w, pp, cache_nsa_kv[i],
                                   page_table, state_nsa_win[i], state_rglru_h[i], state_rglru_conv[i])
            kv_s.append(st[0]); win_s.append(st[1]); h_s.append(st[2]); conv_s.append(st[3])
        else:
            pp = prep_layer1(dict(w_in=w_in_cd[i], w_out=w_out_cd[i], v_gain=gmlp_v_gain[i], ws=gmlp_ws[i],
                                  bs=gmlp_bs[i], q_gain=dil_q_gain[i], k_gain=dil_k_gain[i]))
            yp, st = layer1_prompt(yp, mod_p, norm_mix_g[layer], norm_ffn_g[layer], ffn_w, pp)
            dil_p.append(st)
            ys, st = layer1_sample(ys, mod_s, norm_mix_g[layer], norm_ffn_g[layer], ffn_w, pp, state_dil_kv[i])
            dil_s.append(st[0]); gv_s.append(st[1])
    return (yp, ys, jnp.stack(kv_p), jnp.stack(kv_s), jnp.stack(win_p), jnp.stack(win_s),
            jnp.stack(h_p), jnp.stack(h_s), jnp.stack(conv_p), jnp.stack(conv_s),
            jnp.stack(dil_p), jnp.stack(dil_s), jnp.stack(gv_s))
```

```python
import functools

import numpy as np
import jax
import jax.numpy as jnp
from jax import lax
from jax.experimental import pallas as pl
from jax.experimental.pallas import tpu as pltpu

F32 = jnp.float32
BF16 = jnp.bfloat16

LANES = 128
HEAD_DIM = 64
QBLK = 128
CMP_BLOCK = 32
SEL_BLOCK = 64
N_SEL = 16
NSA_WINDOW = 512
FORCE_SCORE = 1.0e4
DIL_PATTERNS = ((128, 1), (512, 4), (2048, 16))
DIL_MAX = 2048
RG_C = 8.0
RMS_EPS = 1e-6
NEG_INF = -1e30
VMEM_LIMIT = 56 * 1024 * 1024


def _cparams(*sem):
    return pltpu.CompilerParams(dimension_semantics=sem, vmem_limit_bytes=VMEM_LIMIT)


def _dot(a, b):
    return jnp.dot(a, b, preferred_element_type=F32)


def _dot_nt(a, b):
    return lax.dot_general(a, b, (((1,), (1,)), ((), ())), preferred_element_type=F32)


def _gelu(x):
    return 0.5 * x * (1.0 + jnp.tanh(np.sqrt(2.0 / np.pi) * (x + 0.044715 * (x * x * x))))


def _sigmoid(x):
    return 1.0 / (1.0 + jnp.exp(-x))


def _head_norm(z, gain):
    lo = lax.broadcasted_iota(jnp.int32, z.shape, 1) < HEAD_DIM
    z2 = z * z
    s_lo = jnp.sum(jnp.where(lo, z2, 0.0), axis=-1, keepdims=True)
    s_hi = jnp.sum(jnp.where(lo, 0.0, z2), axis=-1, keepdims=True)
    inv = lax.rsqrt(jnp.where(lo, s_lo, s_hi) * (1.0 / HEAD_DIM) + RMS_EPS)
    return z * inv * gain


def _ada_kernel(c_ref, w_ref, b_ref, o_ref):
    c = c_ref[...]
    s = c * _sigmoid(c)
    o_ref[0] = _dot(s.astype(BF16), w_ref[0]) + b_ref[0]


def ada_mod(c_all, w_ada, b_ada):
    m, d = c_all.shape
    nl, _, n = w_ada.shape
    tn = 1536
    return pl.pallas_call(
        _ada_kernel,
        grid=(nl, n // tn),
        in_specs=[pl.BlockSpec((m, d), lambda l, j: (0, 0)),
                  pl.BlockSpec((1, d, tn), lambda l, j: (l, 0, j)),
                  pl.BlockSpec((1, 1, tn), lambda l, j: (l, 0, j))],
        out_specs=pl.BlockSpec((1, m, tn), lambda l, j: (l, 0, j)),
        out_shape=jax.ShapeDtypeStruct((nl, m, n), F32),
        compiler_params=_cparams("arbitrary", "arbitrary"),
        name="ada_mod",
    )(c_all, w_ada, b_ada.reshape(nl, 1, n))


def _in_proj_kernel(x_ref, sc_ref, sh_ref, g_ref, w_ref, hg_ref, *out_refs, norm_blocks, outs):
    x = x_ref[0]
    ms = jnp.mean(x * x, axis=-1, keepdims=True)
    h = x * lax.rsqrt(ms + RMS_EPS) * g_ref[...]
    h = h * (1.0 + sc_ref[0]) + sh_ref[0]
    z = _dot(h.astype(BF16), w_ref[...])
    nblk = z.shape[1] // LANES
    blocks = []
    for j in range(nblk):
        zb = z[:, j * LANES:(j + 1) * LANES]
        if j in norm_blocks:
            zb = _head_norm(zb, hg_ref[:, j * LANES:(j + 1) * LANES])
        blocks.append(zb)
    for o_ref, (c0, width, _) in zip(out_refs, outs):
        for j in range(width // LANES):
            o_ref[0, :, j * LANES:(j + 1) * LANES] = blocks[c0 // LANES + j].astype(o_ref.dtype)


def in_proj(x, mod, sh_idx, sc_idx, g, w, head_gain, norm_blocks, outs, tm):
    bm, t, d = x.shape
    r = mod.shape[1]
    n = w.shape[1]
    tm = min(tm, t)
    if r == 1:
        mod_spec = lambda k: pl.BlockSpec((1, 1, d), lambda b, i: (b, 0, k))
    else:
        assert r == t and tm == t
        mod_spec = lambda k: pl.BlockSpec((1, tm, d), lambda b, i: (b, 0, k))
    kern = functools.partial(_in_proj_kernel, norm_blocks=tuple(norm_blocks), outs=tuple(outs))
    return pl.pallas_call(
        kern,
        grid=(bm, t // tm),
        in_specs=[pl.BlockSpec((1, tm, d), lambda b, i: (b, i, 0)),
                  mod_spec(sc_idx), mod_spec(sh_idx),
                  pl.BlockSpec((1, d), lambda b, i: (0, 0)),
                  pl.BlockSpec((d, n), lambda b, i: (0, 0)),
                  pl.BlockSpec((1, n), lambda b, i: (0, 0))],
        out_specs=[pl.BlockSpec((1, tm, wd), lambda b, i: (b, i, 0)) for (_, wd, _) in outs],
        out_shape=[jax.ShapeDtypeStruct((bm, t, wd), dt) for (_, wd, dt) in outs],
        compiler_params=_cparams("arbitrary", "arbitrary"),
        name="in_proj",
    )(x, mod, mod, g.reshape(1, d), w, head_gain)


def _out_ffn_kernel(ma_ref, mb_ref, x_ref, gm_ref, shf_ref, scf_ref, gf_ref, lnf_ref,
                    woa_ref, wob_ref, wg_ref, wu_ref, wd_ref, o_ref, *, hidden_chunk):
    mix = _dot(ma_ref[0], woa_ref[...]) + _dot(mb_ref[0], wob_ref[...])
    x1 = x_ref[0] + gm_ref[0] * mix
    ms = jnp.mean(x1 * x1, axis=-1, keepdims=True)
    hf = x1 * lax.rsqrt(ms + RMS_EPS) * lnf_ref[...]
    hf = (hf * (1.0 + scf_ref[0]) + shf_ref[0]).astype(BF16)
    hidden = wg_ref.shape[1]
    ffn = jnp.zeros(x1.shape, F32)
    for c0 in range(0, hidden, hidden_chunk):
        gt = _dot(hf, wg_ref[:, c0:c0 + hidden_chunk])
        up = _dot(hf, wu_ref[:, c0:c0 + hidden_chunk])
        act = (gt * _sigmoid(gt) * up).astype(BF16)
        ffn = ffn + _dot(act, wd_ref[c0:c0 + hidden_chunk, :])
    o_ref[0] = x1 + gf_ref[0] * ffn


def out_ffn(mix_a, mix_b, x, mod, ln_ffn, wo_a, wo_b, wg, wu, wd, tm):
    bm, t, d = x.shape
    r = mod.shape[1]
    tm = min(tm, t)
    ka, kb = mix_a.shape[2], mix_b.shape[2]
    hidden = wg.shape[1]
    if r == 1:
        mod_spec = lambda k: pl.BlockSpec((1, 1, d), lambda b, i: (b, 0, k))
    else:
        assert r == t and tm == t
        mod_spec = lambda k: pl.BlockSpec((1, tm, d), lambda b, i: (b, 0, k))
    const = lambda shape: pl.BlockSpec(shape, lambda b, i: (0,) * len(shape), pipeline_mode=pl.Buffered(1))
    kern = functools.partial(_out_ffn_kernel, hidden_chunk=hidden // 2)
    return pl.pallas_call(
        kern,
        grid=(bm, t // tm),
        in_specs=[pl.BlockSpec((1, tm, ka), lambda b, i: (b, i, 0)),
                  pl.BlockSpec((1, tm, kb), lambda b, i: (b, i, 0)),
                  pl.BlockSpec((1, tm, d), lambda b, i: (b, i, 0)),
                  mod_spec(2), mod_spec(3), mod_spec(4), mod_spec(5),
                  const((1, d)), const((ka, d)), const((kb, d)),
                  const((d, hidden)), const((d, hidden)), const((hidden, d))],
        out_specs=pl.BlockSpec((1, tm, d), lambda b, i: (b, i, 0)),
        out_shape=jax.ShapeDtypeStruct((bm, t, d), F32),
        compiler_params=_cparams("arbitrary", "arbitrary"),
        name="out_ffn",
    )(mix_a, mix_b, x, mod, mod, mod, mod, ln_ffn.reshape(1, d), wo_a, wo_b, wg, wu, wd)


def _compress_rows(load, nh, w1_ref, pos_ref, w2_ref):
    halves = []
    for parity in range(2):
        halves.append(jnp.concatenate(
            [load(parity * CMP_BLOCK + l, nh, 2 * CMP_BLOCK) for l in range(CMP_BLOCK)], axis=1))
    xs = (jnp.concatenate(halves, axis=0) + pos_ref[...]).astype(BF16)
    hid = _gelu(_dot(xs, w1_ref[...]))
    return _dot(hid.astype(BF16), w2_ref[...])


def _nsa_compress_kernel(rk_ref, rv_ref, w1k_ref, posk_ref, w2k_ref, w1v_ref, posv_ref, w2v_ref, kg_ref,
                         kc_ref, vc_ref):
    nh = kc_ref.shape[1] // 2
    ldk = lambda s, n, st: rk_ref[0, pl.ds(s, n, stride=st), :]
    ldv = lambda s, n, st: rv_ref[0, pl.ds(s, n, stride=st), :]
    kc = _compress_rows(ldk, nh, w1k_ref, posk_ref, w2k_ref)
    kc_ref[0] = _head_norm(kc, kg_ref[...]).astype(kc_ref.dtype)
    vc_ref[0] = _compress_rows(ldv, nh, w1v_ref, posv_ref, w2v_ref).astype(vc_ref.dtype)


def nsa_compress(rows, cw):
    b, t, _ = rows.shape
    nc = t // CMP_BLOCK
    const = lambda a: pl.BlockSpec(a.shape, lambda i: (0,) * a.ndim)
    ws = (cw["w1k"], cw["posk"], cw["w2k"], cw["w1v"], cw["posv"], cw["w2v"], cw["kg0"])
    return pl.pallas_call(
        _nsa_compress_kernel,
        grid=(b,),
        in_specs=[pl.BlockSpec((1, t, LANES), lambda i: (i, 0, 0)),
                  pl.BlockSpec((1, t, LANES), lambda i: (i, 0, 1))] + [const(a) for a in ws],
        out_specs=[pl.BlockSpec((1, nc, LANES), lambda i: (i, 0, 0))] * 2,
        out_shape=[jax.ShapeDtypeStruct((b, nc, LANES), BF16)] * 2,
        compiler_params=_cparams("arbitrary"),
        name="nsa_compress",
    )(rows, rows, *ws)


def _select_blocks(imp, blk_f, n_pick):
    sel = jnp.zeros(imp.shape, F32)
    work = imp
    for _ in range(n_pick):
        m = jnp.max(work, axis=-1, keepdims=True)
        idx = jnp.min(jnp.where(work == m, blk_f, 1e9), axis=-1, keepdims=True)
        pick = blk_f == idx
        sel = jnp.where(pick, 1.0, sel)
        work = jnp.where(pick, -2.0, work)
    return sel


def _nsa_prompt_kernel(q_ref, kc_ref, vc_ref, ks_ref, vs_ref, kw_ref, vw_ref, gl_ref, o_ref):
    qi = pl.program_id(1)
    t0 = qi * QBLK
    nc = kc_ref.shape[1]
    nb = nc // 2
    scale = HEAD_DIM ** -0.5
    rep = q_ref.shape[2] // LANES
    rows = 2 * rep * QBLK

    lane = lax.broadcasted_iota(jnp.int32, (QBLK, LANES), 1)
    trow = t0 + lax.broadcasted_iota(jnp.int32, (QBLK, LANES), 0)
    q = q_ref[0]
    qs = []
    for g in range(2):
        gm = (lane >= g * HEAD_DIM) & (lane < (g + 1) * HEAD_DIM)
        for r in range(rep):
            qs.append(jnp.where(gm, q[:, r * LANES:(r + 1) * LANES], jnp.zeros((), q.dtype)))
    qall = jnp.concatenate(qs, axis=0)

    s = _dot_nt(qall, kc_ref[0]) * scale
    cl = lax.broadcasted_iota(jnp.int32, (QBLK, nc), 1)
    cidx = jnp.where(cl < nb, 2 * cl, 2 * (cl - nb) + 1)
    tq_c = t0 + lax.broadcasted_iota(jnp.int32, (QBLK, nc), 0)
    cmask1 = ((cidx + 1) * CMP_BLOCK - 1) <= tq_c
    cmask = jnp.concatenate([cmask1] * (2 * rep), axis=0)
    s = jnp.where(cmask, s, NEG_INF)
    m = jnp.max(s, axis=-1, keepdims=True)
    e = jnp.where(cmask, jnp.exp(s - m), 0.0)
    den = jnp.maximum(jnp.sum(e, axis=-1, keepdims=True), 1e-30)
    p = e * (1.0 / den)
    o_cmp = _dot(p.astype(BF16), vc_ref[0])

    bl = lax.broadcasted_iota(jnp.int32, (QBLK, nb), 1)
    tq_b = t0 + lax.broadcasted_iota(jnp.int32, (QBLK, nb), 0)
    forced = (bl == 0) | (bl == tq_b // SEL_BLOCK)
    valid = bl * SEL_BLOCK <= tq_b
    blk_f = bl.astype(F32)
    sels = []
    for g in range(2):
        ps = p[(g * rep) * QBLK:(g * rep + 1) * QBLK]
        for r in range(1, rep):
            ps = ps + p[(g * rep + r) * QBLK:(g * rep + r + 1) * QBLK]
        imp = ps[:, :nb] + ps[:, nb:]
        imp = jnp.where(forced, FORCE_SCORE, imp)
        imp = jnp.where(valid, imp, -1.0)
        sels.append(_select_blocks(imp, blk_f, min(N_SEL, nb)).astype(BF16))

    erow = lax.broadcasted_iota(jnp.int32, (nb, QBLK), 0)
    ecol = lax.broadcasted_iota(jnp.int32, (nb, QBLK), 1) // SEL_BLOCK

    def flash(k_ref, v_ref, lo, hi, mask_fn):
        def body(kt, carry):
            m_i, l_i, acc = carry
            ks = k_ref[0, pl.ds(pl.multiple_of(kt * QBLK, QBLK), QBLK), :]
            vs = v_ref[0, pl.ds(pl.multiple_of(kt * QBLK, QBLK), QBLK), :]
            sc = _dot_nt(qall, ks) * scale
            mk = mask_fn(kt)
            sc = jnp.where(mk, sc, NEG_INF)
            m_new = jnp.maximum(m_i, jnp.max(sc, axis=-1, keepdims=True))
            alpha = jnp.exp(m_i - m_new)
            ee = jnp.where(mk, jnp.exp(sc - m_new), 0.0)
            l_new = alpha * l_i + jnp.sum(ee, axis=-1, keepdims=True)
            acc_new = alpha * acc + _dot(ee.astype(BF16), vs)
            return m_new, l_new, acc_new
        init = (jnp.full((rows, 1), NEG_INF, F32), jnp.zeros((rows, 1), F32), jnp.zeros((rows, LANES), F32))
        _, l_f, acc_f = lax.fori_loop(lo, hi, body, init)
        return acc_f * (1.0 / jnp.maximum(l_f, 1e-30))

    def sel_mask(kt):
        kpos = kt * QBLK + lane
        causal = kpos <= trow
        expand = (erow == 2 * kt + ecol).astype(BF16)
        parts = []
        for g in range(2):
            mg = (_dot(sels[g], expand) > 0.5) & causal
            parts += [mg] * rep
        return jnp.concatenate(parts, axis=0)

    def win_mask(kt):
        dist = trow - (kt * QBLK + lane)
        mw = (dist >= 0) & (dist <= NSA_WINDOW)
        return jnp.concatenate([mw] * (2 * rep), axis=0)

    o_sel = flash(ks_ref, vs_ref, 0, qi + 1, sel_mask)
    o_win = flash(kw_ref, vw_ref, jnp.maximum(qi - NSA_WINDOW // QBLK, 0), qi + 1, win_mask)

    gate = _sigmoid(gl_ref[0])
    outs = []
    for r in range(rep):
        og = []
        for g in range(2):
            c = (g * rep + r) * 3
            sl = slice((g * rep + r) * QBLK, (g * rep + r + 1) * QBLK)
            og.append(gate[:, c:c + 1] * o_cmp[sl] + gate[:, c + 1:c + 2] * o_sel[sl]
                      + gate[:, c + 2:c + 3] * o_win[sl])
        outs.append(jnp.where(lane < HEAD_DIM, og[0], og[1]))
    o_ref[0] = jnp.concatenate(outs, axis=1).astype(o_ref.dtype)


def nsa_prompt_attn(q_bf, kc, vc, rows_bf, win_bf, gl):
    b, t, qw = q_bf.shape
    nc = kc.shape[1]
    nq = t // QBLK
    full = lambda k: pl.BlockSpec((1, t, LANES), lambda i, j: (i, 0, k))
    return pl.pallas_call(
        _nsa_prompt_kernel,
        grid=(b, nq),
        in_specs=[pl.BlockSpec((1, QBLK, qw), lambda i, j: (i, j, 0)),
                  pl.BlockSpec((1, nc, LANES), lambda i, j: (i, 0, 0)),
                  pl.BlockSpec((1, nc, LANES), lambda i, j: (i, 0, 0)),
                  full(2), full(3), full(0), full(1),
                  pl.BlockSpec((1, QBLK, LANES), lambda i, j: (i, j, 0))],
        out_specs=pl.BlockSpec((1, QBLK, qw), lambda i, j: (i, j, 0)),
        out_shape=jax.ShapeDtypeStruct((b, t, qw), BF16),
        compiler_params=_cparams("arbitrary", "arbitrary"),
        name="nsa_prompt_attn",
    )(q_bf, kc, vc, rows_bf, rows_bf, win_bf, win_bf, gl)


def _rglru_gates(xc, wa_ref, ba_ref, wx_ref, bx_ref, lam_ref):
    xb = xc.astype(BF16)
    r = _sigmoid(_dot(xb, wa_ref[...]) + ba_ref[...])
    i = _sigmoid(_dot(xb, wx_ref[...]) + bx_ref[...])
    nl = -lam_ref[...]
    softplus = jnp.maximum(nl, 0.0) + jnp.log1p(jnp.exp(-jnp.abs(nl)))
    log_a = -RG_C * r * softplus
    a = jnp.exp(log_a)
    u = jnp.sqrt(-jnp.tanh(log_a) * (a * a + 1.0)) * (i * xc)
    return a, u


def _rglru_prompt_kernel(xr_ref, gr_ref, cw_ref, cb_ref, wa_ref, ba_ref, wx_ref, bx_ref, lam_ref,
                         o_ref, hl_ref, cl_ref, hcar, xcar):
    i = pl.program_id(1)
    tm, c = xr_ref.shape[1], xr_ref.shape[2]

    @pl.when(i == 0)
    def _():
        hcar[...] = jnp.zeros(hcar.shape, F32)
        xcar[...] = jnp.zeros(xcar.shape, F32)

    x = xr_ref[0]
    prev = xcar[...]
    row = lax.broadcasted_iota(jnp.int32, (tm, c), 0)
    xc = cb_ref[...] + cw_ref[3:4, :] * x
    for k in range(1, 4):
        cur = pltpu.roll(x, k, axis=0)
        old = jnp.tile(pltpu.roll(prev, k, axis=0), (tm // 8, 1))
        xc = xc + cw_ref[3 - k:4 - k, :] * jnp.where(row < k, old, cur)
    a, u = _rglru_gates(xc, wa_ref, ba_ref, wx_ref, bx_ref, lam_ref)

    s = 1
    while s < tm:
        a_sh = jnp.where(row >= s, pltpu.roll(a, s, axis=0), 1.0)
        u_sh = jnp.where(row >= s, pltpu.roll(u, s, axis=0), 0.0)
        u = a * u_sh + u
        a = a * a_sh
        s *= 2
    h = a * hcar[0:1, :] + u
    o_ref[0] = (h * _gelu(gr_ref[0])).astype(o_ref.dtype)
    hcar[...] = jnp.broadcast_to(h[tm - 1:tm, :], hcar.shape)
    xcar[...] = x[tm - 8:tm, :]
    hl_ref[0] = hcar[...]
    cl_ref[0] = xcar[...]


def rglru_prompt(xr, gr, rw, tm):
    b, t, c = xr.shape
    tm = min(tm, t)
    const = lambda a: pl.BlockSpec(a.shape, lambda i, j: (0,) * a.ndim)
    ws = (rw["conv_w"], rw["conv_b"], rw["wa"], rw["ba"], rw["wx"], rw["bx"], rw["lam"])
    tile = pl.BlockSpec((1, tm, c), lambda i, j: (i, j, 0))
    last = pl.BlockSpec((1, 8, c), lambda i, j: (i, 0, 0))
    return pl.pallas_call(
        _rglru_prompt_kernel,
        grid=(b, t // tm),
        in_specs=[tile, tile] + [const(a) for a in ws],
        out_specs=[tile, last, last],
        out_shape=[jax.ShapeDtypeStruct((b, t, c), BF16),
                   jax.ShapeDtypeStruct((b, 8, c), F32),
                   jax.ShapeDtypeStruct((b, 8, c), F32)],
        scratch_shapes=[pltpu.VMEM((8, c), F32), pltpu.VMEM((8, c), F32)],
        compiler_params=_cparams("arbitrary", "arbitrary"),
        name="rglru_prompt",
    )(xr, gr, *ws)


def _gmlp_v(v_raw, vg_ref):
    v = _gelu(v_raw)
    ms = jnp.mean(v * v, axis=-1, keepdims=True)
    return v * lax.rsqrt(ms + RMS_EPS) * vg_ref[...]


def _gmlp_prompt_kernel(u_ref, v_ref, vg_ref, ws_ref, bs_ref, o_ref):
    lc = u_ref.shape[1]
    v = _gmlp_v(v_ref[0], vg_ref).astype(BF16)
    tril = (lax.broadcasted_iota(jnp.int32, (lc, lc), 0) >= lax.broadcasted_iota(jnp.int32, (lc, lc), 1))
    lane = lax.broadcasted_iota(jnp.int32, (lc, LANES), 1)
    parts = []
    for j in range(v.shape[1] // LANES):
        vj = v[:, j * LANES:(j + 1) * LANES]
        w0 = jnp.where(tril, ws_ref[2 * j], 0.0).astype(BF16)
        w1 = jnp.where(tril, ws_ref[2 * j + 1], 0.0).astype(BF16)
        parts.append(jnp.where(lane < HEAD_DIM, _dot(w0, vj), _dot(w1, vj)))
    mixed = jnp.concatenate(parts, axis=1) + bs_ref[...]
    o_ref[0] = (_gelu(u_ref[0]) * mixed).astype(o_ref.dtype)


def gmlp_prompt(u, v, v_gain, ws, bs_exp):
    b, t, c = u.shape
    lc = ws.shape[1]
    tile = pl.BlockSpec((1, lc, c), lambda i, j: (i, j, 0))
    const = lambda a: pl.BlockSpec(a.shape, lambda i, j: (0,) * a.ndim)
    return pl.pallas_call(
        _gmlp_prompt_kernel,
        grid=(b, t // lc),
        in_specs=[tile, tile, const(v_gain), const(ws), const(bs_exp)],
        out_specs=tile,
        out_shape=jax.ShapeDtypeStruct((b, t, c), BF16),
        compiler_params=_cparams("arbitrary", "arbitrary"),
        name="gmlp_prompt",
    )(u, v, v_gain, ws, bs_exp)


DIL_SPAN = DIL_MAX
DIL_UNROLL = 4


def _dil_prompt_kernel(q_ref, k_ref, v_ref, o_ref, acc_ref, m_ref, l_ref):
    span = q_ref.shape[1]
    base = pl.program_id(2) * span
    scale = HEAD_DIM ** -0.5
    lane = lax.broadcasted_iota(jnp.int32, (QBLK, LANES), 1)
    row = lax.broadcasted_iota(jnp.int32, (QBLK, LANES), 0)
    lo = lane < HEAD_DIM
    band_prev = jnp.where(lane >= row, 0.0, NEG_INF)
    band_diag = jnp.where(lane <= row, 0.0, NEG_INF)
    band_prev = jnp.concatenate([band_prev, band_prev], axis=0)
    band_diag = jnp.concatenate([band_diag, band_diag], axis=0)

    for p, (window, dil) in enumerate(DIL_PATTERNS):
        assert window // dil == QBLK and span % (QBLK * dil) == 0

        def tile(idx, carry, p=p, dil=dil):
            start = (idx // dil) * (QBLK * dil) + idx % dil
            g0 = base + start
            rows = pl.ds(start, QBLK, stride=dil)
            q = (q_ref[0, rows, :] * scale).astype(BF16)
            zero = jnp.zeros((), BF16)
            q2 = jnp.concatenate([jnp.where(lo, q, zero), jnp.where(lo, zero, q)], axis=0)
            has_prev = g0 >= QBLK * dil
            d_rows = pl.ds(g0, QBLK, stride=dil)
            p_rows = pl.ds(jnp.where(has_prev, g0 - QBLK * dil, g0), QBLK, stride=dil)
            s_d = _dot_nt(q2, k_ref[0, d_rows, :].astype(BF16)) + band_diag
            s_p = _dot_nt(q2, k_ref[0, p_rows, :].astype(BF16)) + (band_prev + jnp.where(has_prev, 0.0, NEG_INF))
            m = jnp.max(jnp.maximum(s_d, s_p), axis=-1, keepdims=True)
            e_d = jnp.exp(s_d - m)
            e_p = jnp.exp(s_p - m)
            l = jnp.sum(e_d + e_p, axis=-1, keepdims=True)
            acc = (_dot(e_d.astype(BF16), v_ref[0, d_rows, :].astype(BF16))
                   + _dot(e_p.astype(BF16), v_ref[0, p_rows, :].astype(BF16)))
            acc_ref[p, rows, :] = jnp.where(lo, acc[:QBLK], acc[QBLK:])
            m_ref[p, rows, :] = jnp.where(lo, m[:QBLK], m[QBLK:])
            l_ref[p, rows, :] = jnp.where(lo, l[:QBLK], l[QBLK:])
            return carry

        lax.fori_loop(0, span // QBLK, tile, 0, unroll=DIL_UNROLL)

    def combine(c, carry):
        rows = pl.ds(pl.multiple_of(c * QBLK, QBLK), QBLK)
        ms = [m_ref[p, rows, :] for p in range(len(DIL_PATTERNS))]
        mx = functools.reduce(jnp.maximum, ms)
        num, den = 0.0, 0.0
        for p, mp in enumerate(ms):
            w = jnp.exp(mp - mx)
            num = num + w * acc_ref[p, rows, :]
            den = den + w * l_ref[p, rows, :]
        o_ref[0, rows, :] = (num * (1.0 / den)).astype(o_ref.dtype)
        return carry

    lax.fori_loop(0, span // QBLK, combine, 0)


def dil_prompt_attn(q, kv):
    b, t, w = q.shape
    npair = w // LANES
    span = min(DIL_SPAN, t)
    n_pat = len(DIL_PATTERNS)
    return pl.pallas_call(
        _dil_prompt_kernel,
        grid=(b, npair, t // span),
        in_specs=[pl.BlockSpec((1, span, LANES), lambda i, p, j: (i, j, p)),
                  pl.BlockSpec((1, t, LANES), lambda i, p, j: (i, 0, p)),
                  pl.BlockSpec((1, t, LANES), lambda i, p, j: (i, 0, npair + p))],
        out_specs=pl.BlockSpec((1, span, LANES), lambda i, p, j: (i, j, p)),
        out_shape=jax.ShapeDtypeStruct((b, t, w), BF16),
        scratch_shapes=[pltpu.VMEM((n_pat, span, LANES), F32)] * 3,
        compiler_params=_cparams("arbitrary", "arbitrary", "arbitrary"),
        name="dil_prompt_attn",
    )(q, kv, kv)


NSA_G, NSA_REP = 2, 4
NSA_QW = NSA_G * NSA_REP * HEAD_DIM
QPERM = np.arange(NSA_QW).reshape(NSA_G, NSA_REP, HEAD_DIM).transpose(1, 0, 2).reshape(-1)


def _block_diag(blocks):
    n, a, b = blocks.shape
    out = jnp.zeros((n, a, n, b), blocks.dtype)
    out = out.at[jnp.arange(n), :, jnp.arange(n), :].set(blocks)
    return out.reshape(n * a, n * b)


def prep_layer0(p):
    w_in = p["w_in"]
    d = w_in.shape[0]
    n_gate = NSA_G * NSA_REP * 3
    kv_w = 6 * NSA_G * HEAD_DIM
    c_gl = NSA_QW + kv_w
    c_xr = c_gl + n_gate
    d_rnn = (w_in.shape[1] - c_xr) // 2
    cols = np.concatenate([QPERM, np.arange(NSA_QW, c_gl), np.arange(c_xr, c_xr + 2 * d_rnn),
                           np.arange(c_gl, c_xr)])
    w = jnp.concatenate([w_in[:, cols], jnp.zeros((d, LANES - n_gate), w_in.dtype)], axis=1).astype(BF16)
    n = w.shape[1]
    hg = jnp.ones((n,), F32)
    hg = hg.at[0:NSA_QW].set(jnp.tile(p["q_gain"], NSA_QW // HEAD_DIM))
    hg = hg.at[NSA_QW + 2 * LANES:NSA_QW + 3 * LANES].set(jnp.tile(p["k_gain"][1], 2))
    hg = hg.at[NSA_QW + 4 * LANES:NSA_QW + 5 * LANES].set(jnp.tile(p["k_gain"][2], 2))
    qb = NSA_QW // LANES
    c_rows, c_win, c_x, c_g, c_l = NSA_QW, NSA_QW + 4 * LANES, NSA_QW + 6 * LANES, NSA_QW + 6 * LANES + d_rnn, \
        NSA_QW + 6 * LANES + 2 * d_rnn
    outs = ((0, NSA_QW, BF16), (c_rows, 4 * LANES, F32), (c_rows, 4 * LANES, BF16), (c_win, 2 * LANES, F32),
            (c_win, 2 * LANES, BF16), (c_x, d_rnn, F32), (c_g, d_rnn, F32), (c_l, LANES, F32))
    cw = {}
    for c, nm in enumerate("kv"):
        w1 = p["cmp_w1"][c]
        big = jnp.zeros((CMP_BLOCK, NSA_G, HEAD_DIM, NSA_G, HEAD_DIM), F32)
        for g in range(NSA_G):
            big = big.at[:, g, :, g, :].set(w1)
        cw["w1" + nm] = big.reshape(CMP_BLOCK * LANES, LANES).astype(BF16)
        cw["pos" + nm] = jnp.tile(p["cmp_pos"][c], (1, NSA_G)).reshape(1, CMP_BLOCK * LANES)
        cw["w2" + nm] = _block_diag(jnp.stack([p["cmp_w2"][c]] * NSA_G)).astype(BF16)
    cw["kg0"] = jnp.tile(p["k_gain"][0], 2).reshape(1, LANES)
    rw = dict(conv_w=p["conv_w"], conv_b=p["conv_b"].reshape(1, -1),
              wa=_block_diag(p["wa"]).astype(BF16), ba=p["ba"].reshape(1, -1),
              wx=_block_diag(p["wx"]).astype(BF16), bx=p["bx"].reshape(1, -1), lam=p["lam"].reshape(1, -1))
    w_out = p["w_out"]
    return dict(w=w, hg=hg.reshape(1, n), norm_blocks=tuple(range(qb)) + (qb + 2, qb + 4), outs=outs,
                cw=cw, rw=rw, wo_a=w_out[QPERM].astype(BF16), wo_b=w_out[NSA_QW:].astype(BF16))


def prep_layer1(p):
    w = p["w_in"].astype(BF16)
    n = w.shape[1]
    c_w = p["v_gain"].shape[0]
    dil_w = (n - 2 * c_w) // 3
    hg = jnp.ones((n,), F32)
    hg = hg.at[2 * c_w:2 * c_w + dil_w].set(jnp.tile(p["q_gain"], dil_w // HEAD_DIM))
    hg = hg.at[2 * c_w + dil_w:2 * c_w + 2 * dil_w].set(jnp.tile(p["k_gain"], dil_w // HEAD_DIM))
    b0 = 2 * c_w // LANES
    nbq = dil_w // LANES
    outs = ((0, c_w, F32), (c_w, c_w, F32), (2 * c_w, dil_w, F32), (2 * c_w + dil_w, 2 * dil_w, F32))
    w_out = p["w_out"]
    gw = c_w // p["ws"].shape[0]
    return dict(w=w, hg=hg.reshape(1, n), norm_blocks=tuple(range(b0, b0 + 2 * nbq)), outs=outs,
                v_gain=p["v_gain"].reshape(1, c_w), ws=p["ws"], bs_exp=jnp.repeat(p["bs"].T, gw, axis=1),
                ws_diag=jnp.repeat(p["ws"][:, 0, 0], gw).reshape(1, c_w),
                bs0=jnp.repeat(p["bs"][:, 0], gw).reshape(1, c_w),
                wo_a=w_out[:c_w].astype(BF16), wo_b=w_out[c_w:].astype(BF16))


def layer0_prompt(x, mod, ln_mix, ln_ffn, ffn_w, pp):
    b, t, _ = x.shape
    q_bf, rows, rows_bf, win, win_bf, xr, gr, gl = in_proj(
        x, mod, 0, 1, ln_mix, pp["w"], pp["hg"], pp["norm_blocks"], pp["outs"], tm=512)
    kc, vc = nsa_compress(rows, pp["cw"])
    o_nsa = nsa_prompt_attn(q_bf, kc, vc, rows_bf, win_bf, gl)
    o_rnn, h_last, conv_last = rglru_prompt(xr, gr, pp["rw"], tm=256)
    y = out_ffn(o_nsa, o_rnn, x, mod, ln_ffn, pp["wo_a"], pp["wo_b"], *ffn_w, tm=512)
    nwin = min(NSA_WINDOW, t)
    state = (rows.reshape(b, t, 4, NSA_G, HEAD_DIM), win[:, t - nwin:].reshape(b, nwin, 2, NSA_G, HEAD_DIM),
             h_last[:, 0], conv_last[:, 5:8])
    return y, state


def layer1_prompt(x, mod, ln_mix, ln_ffn, ffn_w, pp):
    b, t, _ = x.shape
    u, v, q, kv = in_proj(x, mod, 0, 1, ln_mix, pp["w"], pp["hg"], pp["norm_blocks"], pp["outs"], tm=512)
    o_c = gmlp_prompt(u, v, pp["v_gain"], pp["ws"], pp["bs_exp"])
    o_d = dil_prompt_attn(q, kv)
    y = out_ffn(o_c, o_d, x, mod, ln_ffn, pp["wo_a"], pp["wo_b"], *ffn_w, tm=512)
    nkv = min(DIL_MAX, t)
    heads = kv.shape[2] // (2 * HEAD_DIM)
    return y, kv[:, t - nkv:].reshape(b, nkv, 2, heads, HEAD_DIM)


def _row_softmax_parts(s_list, mask_list, s_new):
    m = s_new[0]
    for sn in s_new[1:]:
        m = jnp.maximum(m, sn)
    for s, mk in zip(s_list, mask_list):
        sm = s if mk is None else jnp.where(mk, s, NEG_INF)
        m = jnp.maximum(m, jnp.max(sm, axis=-1, keepdims=True))
    es, den = [], 0.0
    for s, mk in zip(s_list, mask_list):
        e = jnp.exp(s - m)
        if mk is not None:
            e = jnp.where(mk, e, 0.0)
        es.append(e)
        den = den + jnp.sum(e, axis=-1, keepdims=True)
    en = [jnp.exp(sn - m) for sn in s_new]
    for e in en:
        den = den + e
    return es, en, 1.0 / den


def _nsa_decode_kernel(pt_ref, cache_ref, q_ref, rn_ref, wn_ref, gl_ref, sw_ref,
                       w1k_ref, posk_ref, w2k_ref, w1v_ref, posv_ref, w2v_ref, kg_ref, ex_ref,
                       o_ref, wo_ref, bufk, bufv, bufs, sem):
    b = pl.program_id(0)
    nbatch = pl.num_programs(0)
    slot = b % 2
    n_pages = pt_ref.shape[1]
    page = cache_ref.shape[1]
    past = n_pages * page
    nc = past // CMP_BLOCK
    nb = past // SEL_BLOCK
    scale = HEAD_DIM ** -0.5
    rep = q_ref.shape[2] // LANES
    nh = 2 * rep

    def copies(bb, sl):
        out = []
        for j in range(n_pages):
            pg = pt_ref[bb, j]
            dst = pl.ds(j * page, page)
            out.append(pltpu.make_async_copy(cache_ref.at[pg, :, pl.ds(0, LANES)], bufk.at[sl, dst], sem.at[sl, 0, j]))
            out.append(pltpu.make_async_copy(cache_ref.at[pg, :, pl.ds(LANES, LANES)], bufv.at[sl, dst],
                                             sem.at[sl, 1, j]))
            out.append(pltpu.make_async_copy(cache_ref.at[pg, :, pl.ds(2 * LANES, 2 * LANES)], bufs.at[sl, dst],
                                             sem.at[sl, 2, j]))
        return out

    @pl.when(b == 0)
    def _():
        for c in copies(0, 0):
            c.start()

    @pl.when(b + 1 < nbatch)
    def _():
        for c in copies(b + 1, 1 - slot):
            c.start()

    for c in copies(b, slot):
        c.wait()

    lane = lax.broadcasted_iota(jnp.int32, (nh, LANES), 1)
    hrow = lax.broadcasted_iota(jnp.int32, (nh, LANES), 0)
    q = q_ref[0]
    qrows = []
    for g in range(2):
        for r in range(rep):
            qrows.append(q[:, r * LANES:(r + 1) * LANES])
    qm = jnp.concatenate(qrows, axis=0)
    qm = jnp.where((lane // HEAD_DIM) == (hrow // rep), qm, jnp.zeros((), qm.dtype))
    qf = qm.astype(F32)

    def padded(c2):
        z = jnp.zeros((LANES - nc // 2, LANES), F32)
        return jnp.concatenate([c2[:nc // 2], z, c2[nc // 2:], z], axis=0)
    ldk = lambda s, n, st: bufk[slot, pl.ds(s, n, stride=st), :]
    ldv = lambda s, n, st: bufv[slot, pl.ds(s, n, stride=st), :]
    kc = _head_norm(_compress_rows(ldk, nc // 2, w1k_ref, posk_ref, w2k_ref), kg_ref[...])
    vc = _compress_rows(ldv, nc // 2, w1v_ref, posv_ref, w2v_ref)
    kcp = padded(kc).astype(BF16)
    vcp = padded(vc).astype(BF16)
    s_c = _dot_nt(qm, kcp) * scale
    cl = lax.broadcasted_iota(jnp.int32, (nh, 2 * LANES), 1)
    cmask = (cl % LANES) < nc // 2
    s_c = jnp.where(cmask, s_c, NEG_INF)
    e_c = jnp.where(cmask, jnp.exp(s_c - jnp.max(s_c, axis=-1, keepdims=True)), 0.0)
    p_c = e_c * (1.0 / jnp.maximum(jnp.sum(e_c, axis=-1, keepdims=True), 1e-30))
    o_cmp = _dot(p_c.astype(BF16), vcp)

    pp = p_c[:, :LANES] + p_c[:, LANES:]
    imps = []
    for g in range(2):
        ig = pp[g * rep:g * rep + 1]
        for r in range(1, rep):
            ig = ig + pp[g * rep + r:g * rep + r + 1]
        imps += [ig] * rep
    imp = jnp.concatenate(imps, axis=0)
    imp = jnp.where((lane == 0) | (lane == nb), FORCE_SCORE, imp)
    imp = jnp.where(lane <= nb, imp, -3.0)
    sel = _select_blocks(imp, lane.astype(F32), min(N_SEL, nb + 1))
    selm = _dot(sel.astype(BF16), ex_ref[...]) > 0.5

    rn = rn_ref[0]
    s_sel = _dot_nt(qm, bufs[slot, :, pl.ds(0, LANES)].astype(BF16)) * scale
    s_sel_new = jnp.sum(qf * rn[:, 2 * LANES:3 * LANES], axis=-1, keepdims=True) * scale
    (e_s,), (e_sn,), inv_s = _row_softmax_parts([s_sel], [selm], [s_sel_new])
    o_sel = (_dot(e_s.astype(BF16), bufs[slot, :, pl.ds(LANES, LANES)].astype(BF16))
             + e_sn * rn[:, 3 * LANES:4 * LANES]) * inv_s

    sw = sw_ref[0]
    wn = wn_ref[0]
    s_w = _dot_nt(qm, sw[:, :LANES].astype(BF16)) * scale
    s_w_new = jnp.sum(qf * wn[:, :LANES], axis=-1, keepdims=True) * scale
    (e_w,), (e_wn,), inv_w = _row_softmax_parts([s_w], [None], [s_w_new])
    o_win = (_dot(e_w.astype(BF16), sw[:, LANES:].astype(BF16)) + e_wn * wn[:, LANES:]) * inv_w

    gate = _sigmoid(gl_ref[0])
    o = gate[:, 0:1] * o_cmp + gate[:, 1:2] * o_sel + gate[:, 2:3] * o_win
    l1 = lax.broadcasted_iota(jnp.int32, (1, LANES), 1)
    o_ref[0] = jnp.concatenate([jnp.where(l1 < HEAD_DIM, o[r:r + 1], o[rep + r:rep + r + 1])
                                for r in range(rep)], axis=1).astype(o_ref.dtype)

    wb = sw.shape[0]
    wrow = lax.broadcasted_iota(jnp.int32, sw.shape, 0)
    wo_ref[0] = jnp.where(wrow == wb - 1, wn, pltpu.roll(sw, wb - 1, axis=0))


def nsa_decode(page_table, cache, q_bf, rows_new, win_new, gl3, state_win, cw):
    b, n_pages = page_table.shape
    page = cache.shape[1]
    past = n_pages * page
    nb = past // SEL_BLOCK
    assert past % SEL_BLOCK == 0 and nb < LANES and past // CMP_BLOCK <= 2 * LANES
    wb = state_win.shape[1]
    assert wb <= NSA_WINDOW
    expand = (jnp.arange(LANES)[:, None] == (jnp.arange(past) // SEL_BLOCK)[None, :]).astype(BF16)
    ws = (cw["w1k"], cw["posk"], cw["w2k"], cw["w1v"], cw["posv"], cw["w2v"], cw["kg0"], expand)
    per_b = lambda a: pl.BlockSpec((1,) + a.shape[1:], lambda i, pt: (i, 0, 0))
    const = lambda a: pl.BlockSpec(a.shape, lambda i, pt: (0,) * a.ndim)
    grid_spec = pltpu.PrefetchScalarGridSpec(
        num_scalar_prefetch=1,
        grid=(b,),
        in_specs=[pl.BlockSpec(memory_space=pl.ANY), per_b(q_bf), per_b(rows_new), per_b(win_new), per_b(gl3),
                  per_b(state_win)] + [const(a) for a in ws],
        out_specs=[pl.BlockSpec((1, 1, q_bf.shape[2]), lambda i, pt: (i, 0, 0)),
                   pl.BlockSpec((1, wb, state_win.shape[2]), lambda i, pt: (i, 0, 0))],
        scratch_shapes=[pltpu.VMEM((2, past, LANES), F32), pltpu.VMEM((2, past, LANES), F32),
                        pltpu.VMEM((2, past, 2 * LANES), F32), pltpu.SemaphoreType.DMA((2, 3, n_pages))],
    )
    return pl.pallas_call(
        _nsa_decode_kernel,
        grid_spec=grid_spec,
        out_shape=[jax.ShapeDtypeStruct((b, 1, q_bf.shape[2]), BF16),
                   jax.ShapeDtypeStruct(state_win.shape, F32)],
        compiler_params=_cparams("arbitrary"),
        name="nsa_decode",
    )(page_table, cache, q_bf, rows_new, win_new, gl3, state_win, *ws)


def _rglru_decode_kernel(xr_ref, gr_ref, cs_ref, h0_ref, cw_ref, cb_ref, wa_ref, ba_ref, wx_ref, bx_ref, lam_ref,
                         o_ref, h_ref, cn_ref):
    x = xr_ref[...]
    xc = cb_ref[...] + cw_ref[3:4, :] * x
    for k in range(3):
        xc = xc + cw_ref[k:k + 1, :] * cs_ref[k]
    a, u = _rglru_gates(xc, wa_ref, ba_ref, wx_ref, bx_ref, lam_ref)
    h = a * h0_ref[...] + u
    h_ref[...] = h
    o_ref[...] = (h * _gelu(gr_ref[...])).astype(o_ref.dtype)
    cn_ref[0] = cs_ref[1]
    cn_ref[1] = cs_ref[2]
    cn_ref[2] = x


def rglru_decode(xr, gr, conv_t, h0, rw):
    ws = (rw["conv_w"], rw["conv_b"], rw["wa"], rw["ba"], rw["wx"], rw["bx"], rw["lam"])
    return pl.pallas_call(
        _rglru_decode_kernel,
        out_shape=[jax.ShapeDtypeStruct(xr.shape, BF16), jax.ShapeDtypeStruct(xr.shape, F32),
                   jax.ShapeDtypeStruct(conv_t.shape, F32)],
        compiler_params=pltpu.CompilerParams(vmem_limit_bytes=VMEM_LIMIT),
        name="rglru_decode",
    )(xr, gr, conv_t, h0, *ws)


def _gmlp_decode_kernel(u_ref, v_ref, vg_ref, wd_ref, b0_ref, o_ref, vn_ref):
    v = _gmlp_v(v_ref[...], vg_ref)
    vn_ref[...] = v
    o_ref[...] = (_gelu(u_ref[...]) * (wd_ref[...] * v + b0_ref[...])).astype(o_ref.dtype)


def gmlp_decode(u, v, v_gain, ws_diag, bs0):
    return pl.pallas_call(
        _gmlp_decode_kernel,
        out_shape=[jax.ShapeDtypeStruct(u.shape, BF16), jax.ShapeDtypeStruct(u.shape, F32)],
        compiler_params=pltpu.CompilerParams(vmem_limit_bytes=VMEM_LIMIT),
        name="gmlp_decode",
    )(u, v, v_gain, ws_diag, bs0)


DIL_COPY_CHUNKS = 8


def _dil_decode_kernel(q_ref, kvn_ref, *rest):
    n_pat = len(DIL_PATTERNS)
    pat_refs, (st_hbm, o_ref, so_hbm, sem, sem_new) = rest[:n_pat], rest[n_pat:]
    b = pl.program_id(0)
    nbatch = pl.num_programs(0)
    slot = b % 2
    wb = st_hbm.shape[1]
    scale = HEAD_DIM ** -0.5

    def shift_copies(bb, sl):
        rows = wb - 1
        chunk = -(-rows // DIL_COPY_CHUNKS)
        out = []
        for c in range(DIL_COPY_CHUNKS):
            r0 = c * chunk
            n = min(chunk, rows - r0)
            out.append(pltpu.make_async_copy(st_hbm.at[bb, pl.ds(r0 + 1, n)], so_hbm.at[bb, pl.ds(r0, n)],
                                             sem.at[sl, c]))
        return out

    for c in shift_copies(b, slot):
        c.start()
    new_row = pltpu.make_async_copy(kvn_ref.at[0], so_hbm.at[b, pl.ds(wb - 1, 1)], sem_new.at[0])
    new_row.start()

    qs = q_ref[...] * scale
    kn, vn = kvn_ref[0, 0, 0], kvn_ref[0, 0, 1]
    s_new = jnp.sum(qs * kn, axis=-1, keepdims=True)
    ss = [jnp.sum(p_ref[:, 0] * qs[None], axis=-1, keepdims=True) for p_ref in pat_refs]
    m = s_new
    for s in ss:
        m = jnp.maximum(m, jnp.max(s, axis=0))
    e_new = n_pat * jnp.exp(s_new - m)
    den = e_new
    acc = e_new * vn
    for s, p_ref in zip(ss, pat_refs):
        e = jnp.exp(s - m[None])
        den = den + jnp.sum(e, axis=0)
        acc = acc + jnp.sum(e * p_ref[:, 1], axis=0)
    o_ref[...] = acc * (1.0 / den)

    new_row.wait()

    @pl.when(b > 0)
    def _():
        for c in shift_copies(b - 1, 1 - slot):
            c.wait()

    @pl.when(b == nbatch - 1)
    def _():
        for c in shift_copies(b, slot):
            c.wait()


def dil_decode(q, kv_new, state):
    b, wb, _, nh, hd = state.shape
    assert wb == DIL_MAX
    views, view_specs = [], []
    for window, dil in DIL_PATTERNS:
        n = window // dil
        assert (wb // dil) % n == 0
        views.append(state.reshape(b, wb // dil, dil, 2, nh, hd))
        view_specs.append(pl.BlockSpec((None, n, None, 2, nh, hd),
                                       lambda i, blk=wb // dil // n - 1: (i, blk, 0, 0, 0, 0)))
    return pl.pallas_call(
        _dil_decode_kernel,
        grid=(b,),
        in_specs=[pl.BlockSpec((None, nh, hd), lambda i: (i, 0, 0)),
                  pl.BlockSpec((1, 1, 2, nh, hd), lambda i: (i, 0, 0, 0, 0))] + view_specs
                 + [pl.BlockSpec(memory_space=pl.ANY)],
        out_specs=[pl.BlockSpec((None, nh, hd), lambda i: (i, 0, 0)),
                   pl.BlockSpec(memory_space=pl.ANY)],
        out_shape=[jax.ShapeDtypeStruct((b, nh, hd), F32), jax.ShapeDtypeStruct(state.shape, F32)],
        scratch_shapes=[pltpu.SemaphoreType.DMA((2, DIL_COPY_CHUNKS)), pltpu.SemaphoreType.DMA((1,))],
        compiler_params=_cparams("arbitrary"),
        name="dil_decode",
    )(q, kv_new, *views, state)


def layer0_sample(x, mod, ln_mix, ln_ffn, ffn_w, pp, cache, page_table, state_win, state_h, state_conv):
    b, _, d = x.shape
    xs = x.reshape(1, b, d)
    q_bf, rows, _, win, _, xr, gr, gl = in_proj(
        xs, mod, 0, 1, ln_mix, pp["w"], pp["hg"], pp["norm_blocks"], pp["outs"], tm=b)
    n_gate = NSA_G * NSA_REP * 3
    gl3 = gl[0, :, :n_gate].reshape(b, NSA_G * NSA_REP, 3)
    wb = state_win.shape[1]
    o_nsa, win_out = nsa_decode(
        page_table, cache.reshape(cache.shape[0], cache.shape[1], -1), q_bf.reshape(b, 1, -1),
        rows.reshape(b, 1, -1), win.reshape(b, 1, -1), gl3, state_win.reshape(b, wb, -1), pp["cw"])
    o_rnn, h_new, conv_new = rglru_decode(xr[0], gr[0], state_conv.transpose(1, 0, 2), state_h, pp["rw"])
    y = out_ffn(o_nsa.reshape(1, b, -1), o_rnn[None], xs, mod, ln_ffn, pp["wo_a"], pp["wo_b"], *ffn_w, tm=b)
    state = (rows.reshape(b, 1, 4, NSA_G, HEAD_DIM), win_out.reshape(b, wb, 2, NSA_G, HEAD_DIM), h_new,
             conv_new.transpose(1, 0, 2))
    return y.reshape(b, 1, d), state


def layer1_sample(x, mod, ln_mix, ln_ffn, ffn_w, pp, state_dil):
    b, _, d = x.shape
    xs = x.reshape(1, b, d)
    u, v, q, kv = in_proj(xs, mod, 0, 1, ln_mix, pp["w"], pp["hg"], pp["norm_blocks"], pp["outs"], tm=b)
    o_c, v_n = gmlp_decode(u[0], v[0], pp["v_gain"], pp["ws_diag"], pp["bs0"])
    heads = state_dil.shape[3]
    o_d, dil_out = dil_decode(q.reshape(b, heads, HEAD_DIM), kv.reshape(b, 1, 2, heads, HEAD_DIM), state_dil)
    y = out_ffn(o_c[None], o_d.reshape(1, b, -1).astype(BF16), xs, mod, ln_ffn, pp["wo_a"], pp["wo_b"], *ffn_w,
                tm=b)
    return y.reshape(b, 1, d), (dil_out, v_n.reshape(b, 1, -1))


def kernel(x_prompt, x_sample, cache_nsa_kv, state_nsa_win, state_rglru_h, state_rglru_conv, state_dil_kv, page_table, c_prompt, c_sample, norm_mix_g, norm_ffn_g, w_ada, b_ada, w_ffn_gate, w_ffn_up, w_ffn_down, w_in_ab, w_out_ab, nsa_q_gain, nsa_k_gain, nsa_cmp_w1, nsa_cmp_w2, nsa_cmp_pos, rg_conv_w, rg_conv_b, rg_wa, rg_ba, rg_wx, rg_bx, rg_lambda, w_in_cd, w_out_cd, gmlp_v_gain, gmlp_ws, gmlp_bs, dil_q_gain, dil_k_gain):
    depth = norm_mix_g.shape[0]
    bp, bs = x_prompt.shape[0], x_sample.shape[0]
    pad = -(bp + bs) % 8
    c_all = jnp.concatenate([c_prompt, c_sample, jnp.zeros((pad, c_prompt.shape[1]), F32)], axis=0)
    mod_all = ada_mod(c_all, w_ada.astype(BF16), b_ada)
    yp, ys = x_prompt, x_sample
    kv_p, kv_s, win_p, win_s, h_p, h_s, conv_p, conv_s, dil_p, dil_s, gv_s = ([] for _ in range(11))
    for layer in range(depth):
        i = layer // 2
        mod_p = mod_all[layer, :bp, None, :]
        mod_s = mod_all[layer, None, bp:bp + bs, :]
        ffn_w = (w_ffn_gate[layer].astype(BF16), w_ffn_up[layer].astype(BF16), w_ffn_down[layer].astype(BF16))
        if layer % 2 == 0:
            pp = prep_layer0(dict(w_in=w_in_ab[i], w_out=w_out_ab[i], q_gain=nsa_q_gain[i], k_gain=nsa_k_gain[i],
                                  cmp_w1=nsa_cmp_w1[i], cmp_w2=nsa_cmp_w2[i], cmp_pos=nsa_cmp_pos[i],
                                  conv_w=rg_conv_w[i], conv_b=rg_conv_b[i], wa=rg_wa[i], ba=rg_ba[i],
                                  wx=rg_wx[i], bx=rg_bx[i], lam=rg_lambda[i]))
            yp, st = layer0_prompt(yp, mod_p, norm_mix_g[layer], norm_ffn_g[layer], ffn_w, pp)
            kv_p.append(st[0]); win_p.append(st[1]); h_p.append(st[2]); conv_p.append(st[3])
            ys, st = layer0_sample(ys, mod_s, norm_mix_g[layer], norm_ffn_g[layer], ffn_w, pp, cache_nsa_kv[i],
                                   page_table, state_nsa_win[i], state_rglru_h[i], state_rglru_conv[i])
            kv_s.append(st[0]); win_s.append(st[1]); h_s.append(st[2]); conv_s.append(st[3])
        else:
            pp = prep_layer1(dict(w_in=w_in_cd[i], w_out=w_out_cd[i], v_gain=gmlp_v_gain[i], ws=gmlp_ws[i],
                                  bs=gmlp_bs[i], q_gain=dil_q_gain[i], k_gain=dil_k_gain[i]))
            yp, st = layer1_prompt(yp, mod_p, norm_mix_g[layer], norm_ffn_g[layer], ffn_w, pp)
            dil_p.append(st)
            ys, st = layer1_sample(ys, mod_s, norm_mix_g[layer], norm_ffn_g[layer], ffn_w, pp, state_dil_kv[i])
            dil_s.append(st[0]); gv_s.append(st[1])
    return (yp, ys, jnp.stack(kv_p), jnp.stack(kv_s), jnp.stack(win_p), jnp.stack(win_s),
            jnp.stack(h_p), jnp.stack(h_s), jnp.stack(conv_p), jnp.stack(conv_s),
            jnp.stack(dil_p), jnp.stack(dil_s), jnp.stack(gv_s))
```

```python
import functools

import numpy as np
import jax
import jax.numpy as jnp
from jax import lax
from jax.experimental import pallas as pl
from jax.experimental.pallas import tpu as pltpu

F32 = jnp.float32
BF16 = jnp.bfloat16

LANES = 128
HEAD_DIM = 64
QBLK = 128
CMP_BLOCK = 32
SEL_BLOCK = 64
N_SEL = 16
NSA_WINDOW = 512
FORCE_SCORE = 1.0e4
DIL_PATTERNS = ((128, 1), (512, 4), (2048, 16))
DIL_MAX = 2048
RG_C = 8.0
RMS_EPS = 1e-6
NEG_INF = -1e30
VMEM_LIMIT = 56 * 1024 * 1024


def _cparams(*sem):
    return pltpu.CompilerParams(dimension_semantics=sem, vmem_limit_bytes=VMEM_LIMIT)


def _dot(a, b):
    return jnp.dot(a, b, preferred_element_type=F32)


def _dot_nt(a, b):
    return lax.dot_general(a, b, (((1,), (1,)), ((), ())), preferred_element_type=F32)


def _gelu(x):
    return 0.5 * x * (1.0 + jnp.tanh(np.sqrt(2.0 / np.pi) * (x + 0.044715 * (x * x * x))))


def _sigmoid(x):
    return 1.0 / (1.0 + jnp.exp(-x))


def _head_norm(z, gain):
    lo = lax.broadcasted_iota(jnp.int32, z.shape, 1) < HEAD_DIM
    z2 = z * z
    s_lo = jnp.sum(jnp.where(lo, z2, 0.0), axis=-1, keepdims=True)
    s_hi = jnp.sum(jnp.where(lo, 0.0, z2), axis=-1, keepdims=True)
    inv = lax.rsqrt(jnp.where(lo, s_lo, s_hi) * (1.0 / HEAD_DIM) + RMS_EPS)
    return z * inv * gain


def _ada_kernel(c_ref, w_ref, b_ref, o_ref):
    c = c_ref[...]
    s = c * _sigmoid(c)
    o_ref[0] = _dot(s.astype(BF16), w_ref[0]) + b_ref[0]


def ada_mod(c_all, w_ada, b_ada):
    m, d = c_all.shape
    nl, _, n = w_ada.shape
    tn = 1536
    return pl.pallas_call(
        _ada_kernel,
        grid=(nl, n // tn),
        in_specs=[pl.BlockSpec((m, d), lambda l, j: (0, 0)),
                  pl.BlockSpec((1, d, tn), lambda l, j: (l, 0, j)),
                  pl.BlockSpec((1, 1, tn), lambda l, j: (l, 0, j))],
        out_specs=pl.BlockSpec((1, m, tn), lambda l, j: (l, 0, j)),
        out_shape=jax.ShapeDtypeStruct((nl, m, n), F32),
        compiler_params=_cparams("arbitrary", "arbitrary"),
        name="ada_mod",
    )(c_all, w_ada, b_ada.reshape(nl, 1, n))


def _in_proj_kernel(x_ref, sc_ref, sh_ref, g_ref, w_ref, hg_ref, *out_refs, norm_blocks, outs):
    x = x_ref[0]
    ms = jnp.mean(x * x, axis=-1, keepdims=True)
    h = x * lax.rsqrt(ms + RMS_EPS) * g_ref[...]
    h = h * (1.0 + sc_ref[0]) + sh_ref[0]
    z = _dot(h.astype(BF16), w_ref[...])
    nblk = z.shape[1] // LANES
    blocks = []
    for j in range(nblk):
        zb = z[:, j * LANES:(j + 1) * LANES]
        if j in norm_blocks:
            zb = _head_norm(zb, hg_ref[:, j * LANES:(j + 1) * LANES])
        blocks.append(zb)
    for o_ref, (c0, width, _) in zip(out_refs, outs):
        for j in range(width // LANES):
            o_ref[0, :, j * LANES:(j + 1) * LANES] = blocks[c0 // LANES + j].astype(o_ref.dtype)


def in_proj(x, mod, sh_idx, sc_idx, g, w, head_gain, norm_blocks, outs, tm):
    bm, t, d = x.shape
    r = mod.shape[1]
    n = w.shape[1]
    tm = min(tm, t)
    if r == 1:
        mod_spec = lambda k: pl.BlockSpec((1, 1, d), lambda b, i: (b, 0, k))
    else:
        assert r == t and tm == t
        mod_spec = lambda k: pl.BlockSpec((1, tm, d), lambda b, i: (b, 0, k))
    kern = functools.partial(_in_proj_kernel, norm_blocks=tuple(norm_blocks), outs=tuple(outs))
    return pl.pallas_call(
        kern,
        grid=(bm, t // tm),
        in_specs=[pl.BlockSpec((1, tm, d), lambda b, i: (b, i, 0)),
                  mod_spec(sc_idx), mod_spec(sh_idx),
                  pl.BlockSpec((1, d), lambda b, i: (0, 0)),
                  pl.BlockSpec((d, n), lambda b, i: (0, 0)),
                  pl.BlockSpec((1, n), lambda b, i: (0, 0))],
        out_specs=[pl.BlockSpec((1, tm, wd), lambda b, i: (b, i, 0)) for (_, wd, _) in outs],
        out_shape=[jax.ShapeDtypeStruct((bm, t, wd), dt) for (_, wd, dt) in outs],
        compiler_params=_cparams("arbitrary", "arbitrary"),
        name="in_proj",
    )(x, mod, mod, g.reshape(1, d), w, head_gain)


def _out_ffn_kernel(ma_ref, mb_ref, x_ref, gm_ref, shf_ref, scf_ref, gf_ref, lnf_ref,
                    woa_ref, wob_ref, wg_ref, wu_ref, wd_ref, o_ref, *, hidden_chunk):
    mix = _dot(ma_ref[0], woa_ref[...]) + _dot(mb_ref[0], wob_ref[...])
    x1 = x_ref[0] + gm_ref[0] * mix
    ms = jnp.mean(x1 * x1, axis=-1, keepdims=True)
    hf = x1 * lax.rsqrt(ms + RMS_EPS) * lnf_ref[...]
    hf = (hf * (1.0 + scf_ref[0]) + shf_ref[0]).astype(BF16)
    hidden = wg_ref.shape[1]
    ffn = jnp.zeros(x1.shape, F32)
    for c0 in range(0, hidden, hidden_chunk):
        gt = _dot(hf, wg_ref[:, c0:c0 + hidden_chunk])
        up = _dot(hf, wu_ref[:, c0:c0 + hidden_chunk])
        act = (gt * _sigmoid(gt) * up).astype(BF16)
        ffn = ffn + _dot(act, wd_ref[c0:c0 + hidden_chunk, :])
    o_ref[0] = x1 + gf_ref[0] * ffn


def out_ffn(mix_a, mix_b, x, mod, ln_ffn, wo_a, wo_b, wg, wu, wd, tm):
    bm, t, d = x.shape
    r = mod.shape[1]
    tm = min(tm, t)
    ka, kb = mix_a.shape[2], mix_b.shape[2]
    hidden = wg.shape[1]
    if r == 1:
        mod_spec = lambda k: pl.BlockSpec((1, 1, d), lambda b, i: (b, 0, k))
    else:
        assert r == t and tm == t
        mod_spec = lambda k: pl.BlockSpec((1, tm, d), lambda b, i: (b, 0, k))
    const = lambda shape: pl.BlockSpec(shape, lambda b, i: (0,) * len(shape), pipeline_mode=pl.Buffered(1))
    kern = functools.partial(_out_ffn_kernel, hidden_chunk=hidden // 2)
    return pl.pallas_call(
        kern,
        grid=(bm, t // tm),
        in_specs=[pl.BlockSpec((1, tm, ka), lambda b, i: (b, i, 0)),
                  pl.BlockSpec((1, tm, kb), lambda b, i: (b, i, 0)),
                  pl.BlockSpec((1, tm, d), lambda b, i: (b, i, 0)),
                  mod_spec(2), mod_spec(3), mod_spec(4), mod_spec(5),
                  const((1, d)), const((ka, d)), const((kb, d)),
                  const((d, hidden)), const((d, hidden)), const((hidden, d))],
        out_specs=pl.BlockSpec((1, tm, d), lambda b, i: (b, i, 0)),
        out_shape=jax.ShapeDtypeStruct((bm, t, d), F32),
        compiler_params=_cparams("arbitrary", "arbitrary"),
        name="out_ffn",
    )(mix_a, mix_b, x, mod, mod, mod, mod, ln_ffn.reshape(1, d), wo_a, wo_b, wg, wu, wd)


def _compress_rows(load, nh, w1_ref, pos_ref, w2_ref):
    halves = []
    for parity in range(2):
        halves.append(jnp.concatenate(
            [load(parity * CMP_BLOCK + l, nh, 2 * CMP_BLOCK) for l in range(CMP_BLOCK)], axis=1))
    xs = (jnp.concatenate(halves, axis=0) + pos_ref[...]).astype(BF16)
    hid = _gelu(_dot(xs, w1_ref[...]))
    return _dot(hid.astype(BF16), w2_ref[...])


def _nsa_compress_kernel(rk_ref, rv_ref, w1k_ref, posk_ref, w2k_ref, w1v_ref, posv_ref, w2v_ref, kg_ref,
                         kc_ref, vc_ref):
    nh = kc_ref.shape[1] // 2
    ldk = lambda s, n, st: rk_ref[0, pl.ds(s, n, stride=st), :]
    ldv = lambda s, n, st: rv_ref[0, pl.ds(s, n, stride=st), :]
    kc = _compress_rows(ldk, nh, w1k_ref, posk_ref, w2k_ref)
    kc_ref[0] = _head_norm(kc, kg_ref[...]).astype(kc_ref.dtype)
    vc_ref[0] = _compress_rows(ldv, nh, w1v_ref, posv_ref, w2v_ref).astype(vc_ref.dtype)


def nsa_compress(rows, cw):
    b, t, _ = rows.shape
    nc = t // CMP_BLOCK
    const = lambda a: pl.BlockSpec(a.shape, lambda i: (0,) * a.ndim)
    ws = (cw["w1k"], cw["posk"], cw["w2k"], cw["w1v"], cw["posv"], cw["w2v"], cw["kg0"])
    return pl.pallas_call(
        _nsa_compress_kernel,
        grid=(b,),
        in_specs=[pl.BlockSpec((1, t, LANES), lambda i: (i, 0, 0)),
                  pl.BlockSpec((1, t, LANES), lambda i: (i, 0, 1))] + [const(a) for a in ws],
        out_specs=[pl.BlockSpec((1, nc, LANES), lambda i: (i, 0, 0))] * 2,
        out_shape=[jax.ShapeDtypeStruct((b, nc, LANES), BF16)] * 2,
        compiler_params=_cparams("arbitrary"),
        name="nsa_compress",
    )(rows, rows, *ws)


def _select_blocks(imp, blk_f, n_pick):
    sel = jnp.zeros(imp.shape, F32)
    work = imp
    for _ in range(n_pick):
        m = jnp.max(work, axis=-1, keepdims=True)
        idx = jnp.min(jnp.where(work == m, blk_f, 1e9), axis=-1, keepdims=True)
        pick = blk_f == idx
        sel = jnp.where(pick, 1.0, sel)
        work = jnp.where(pick, -2.0, work)
    return sel


def _select_blocks_t(imp, blk_f, n_pick):
    sel = jnp.zeros(imp.shape, F32)
    work = imp
    for _ in range(n_pick):
        m = jnp.max(work, axis=0, keepdims=True)
        idx = jnp.min(jnp.where(work == m, blk_f, 1e9), axis=0, keepdims=True)
        pick = blk_f == idx
        sel = jnp.where(pick, 1.0, sel)
        work = jnp.where(pick, -2.0, work)
    return sel


NSA_SEL_CHUNK = 4


def _dot_tn(a, b):
    return lax.dot_general(a, b, (((0,), (0,)), ((), ())), preferred_element_type=F32)


def _nsa_prompt_kernel(q_ref, kc_ref, vc_ref, ks_ref, vs_ref, kw_ref, vw_ref, gl_ref, o_ref, sb_ref):
    qi = pl.program_id(1)
    t0 = qi * QBLK
    nc = kc_ref.shape[1]
    nb = nc // 2
    scale = HEAD_DIM ** -0.5
    rep = q_ref.shape[2] // LANES
    nhead = 2 * rep
    ncol = nhead * QBLK

    lane = lax.broadcasted_iota(jnp.int32, (QBLK, LANES), 1)
    sub = lax.broadcasted_iota(jnp.int32, (QBLK, LANES), 0)
    tile8 = lambda x: jnp.concatenate([x] * nhead, axis=1)
    q = q_ref[0]
    qs = []
    for g in range(2):
        gm = (lane >= g * HEAD_DIM) & (lane < (g + 1) * HEAD_DIM)
        for r in range(rep):
            qs.append(jnp.where(gm, q[:, r * LANES:(r + 1) * LANES], jnp.zeros((), q.dtype)))
    qall = jnp.concatenate(qs, axis=0) * scale

    crow = lax.broadcasted_iota(jnp.int32, (nc, QBLK), 0)
    cidx = jnp.where(crow < nb, 2 * crow, 2 * (crow - nb) + 1)
    cvalid = ((cidx + 1) * CMP_BLOCK - 1) <= t0 + lax.broadcasted_iota(jnp.int32, (nc, QBLK), 1)
    cbias = tile8(jnp.where(cvalid, 0.0, NEG_INF))
    cone = tile8(jnp.where(cvalid, 1.0, 0.0))
    sc = _dot_nt(kc_ref[0], qall) + cbias
    ec = jnp.exp(sc - jnp.max(sc, axis=0, keepdims=True)) * cone
    p = ec * (1.0 / jnp.maximum(jnp.sum(ec, axis=0, keepdims=True), 1e-30))
    o_cmp = _dot_tn(vc_ref[0], p.astype(BF16))

    brow = lax.broadcasted_iota(jnp.int32, (nb, QBLK), 0)
    tb = t0 + lax.broadcasted_iota(jnp.int32, (nb, QBLK), 1)
    forced = (brow == 0) | (brow == tb // SEL_BLOCK)
    valid = brow * SEL_BLOCK <= tb
    blk_f = brow.astype(F32)
    for g in range(2):
        ps = p[:, (g * rep) * QBLK:(g * rep + 1) * QBLK]
        for r in range(1, rep):
            ps = ps + p[:, (g * rep + r) * QBLK:(g * rep + r + 1) * QBLK]
        imp = ps[:nb] + ps[nb:]
        imp = jnp.where(forced, FORCE_SCORE, imp)
        imp = jnp.where(valid, imp, -1.0)
        sel = _select_blocks_t(imp, blk_f, min(N_SEL, nb))
        sb_ref[:, g * QBLK:(g + 1) * QBLK] = jnp.where(sel > 0.5, 0.0, NEG_INF)

    def flash_step(k_ref, v_ref, start, nkeys, bias, carry):
        m_i, l_i, acc = carry
        rows = pl.ds(pl.multiple_of(start, QBLK), nkeys)
        s = _dot_nt(k_ref[0, rows, :], qall) + bias
        m_new = jnp.maximum(m_i, jnp.max(s, axis=0, keepdims=True))
        alpha = jnp.exp(m_i - m_new)
        e = jnp.exp(s - m_new)
        l_new = alpha * l_i + jnp.sum(e, axis=0, keepdims=True)
        acc_new = alpha * acc + _dot_tn(v_ref[0, rows, :], e.astype(BF16))
        return m_new, l_new, acc_new

    init = (jnp.full((1, ncol), NEG_INF, F32), jnp.zeros((1, ncol), F32), jnp.zeros((LANES, ncol), F32))

    def sel_bias(kc, causal):
        tiles = []
        for j in range(NSA_SEL_CHUNK):
            kt = kc * NSA_SEL_CHUNK + j
            r0 = sb_ref[pl.ds(2 * kt, 1), :]
            r1 = sb_ref[pl.ds(2 * kt + 1, 1), :]
            parts = []
            for g in range(2):
                bg = jnp.where(sub < SEL_BLOCK, r0[:, g * QBLK:(g + 1) * QBLK], r1[:, g * QBLK:(g + 1) * QBLK])
                if causal:
                    bg = bg + jnp.where(kt * QBLK + sub <= t0 + lane, 0.0, NEG_INF)
                parts += [bg] * rep
            tiles.append(jnp.concatenate(parts, axis=1))
        return jnp.concatenate(tiles, axis=0)

    chunk = NSA_SEL_CHUNK * QBLK
    last = qi // NSA_SEL_CHUNK
    carry = lax.fori_loop(0, last, lambda kc, c: flash_step(ks_ref, vs_ref, kc * chunk, chunk,
                                                            sel_bias(kc, False), c), init)
    _, l_s, acc_s = flash_step(ks_ref, vs_ref, last * chunk, chunk, sel_bias(last, True), carry)
    o_sel = acc_s * (1.0 / l_s)

    wkeys = NSA_WINDOW + QBLK
    wstart = jnp.maximum(qi - NSA_WINDOW // QBLK, 0) * QBLK
    dist = (t0 + lax.broadcasted_iota(jnp.int32, (wkeys, QBLK), 1)) \
        - (wstart + lax.broadcasted_iota(jnp.int32, (wkeys, QBLK), 0))
    wbias = tile8(jnp.where((dist >= 0) & (dist <= NSA_WINDOW), 0.0, NEG_INF))
    _, l_w, acc_w = flash_step(kw_ref, vw_ref, wstart, wkeys, wbias, init)
    o_win = acc_w * (1.0 / l_w)

    gate = _sigmoid(gl_ref[0]).T
    for r in range(rep):
        og = []
        for g in range(2):
            c = (g * rep + r) * 3
            cols = slice((g * rep + r) * QBLK, (g * rep + r + 1) * QBLK)
            og.append(gate[c:c + 1] * o_cmp[:, cols] + gate[c + 1:c + 2] * o_sel[:, cols]
                      + gate[c + 2:c + 3] * o_win[:, cols])
        o_ref[0, :, r * LANES:(r + 1) * LANES] = jnp.where(sub < HEAD_DIM, og[0], og[1]).T.astype(o_ref.dtype)


def nsa_prompt_attn(q_bf, kc, vc, rows_bf, win_bf, gl):
    b, t, qw = q_bf.shape
    nc = kc.shape[1]
    nq = t // QBLK
    assert t % (NSA_SEL_CHUNK * QBLK) == 0 and t >= NSA_WINDOW + QBLK
    full = lambda k: pl.BlockSpec((1, t, LANES), lambda i, j: (i, 0, k))
    return pl.pallas_call(
        _nsa_prompt_kernel,
        grid=(b, nq),
        in_specs=[pl.BlockSpec((1, QBLK, qw), lambda i, j: (i, j, 0)),
                  pl.BlockSpec((1, nc, LANES), lambda i, j: (i, 0, 0)),
                  pl.BlockSpec((1, nc, LANES), lambda i, j: (i, 0, 0)),
                  full(2), full(3), full(0), full(1),
                  pl.BlockSpec((1, QBLK, LANES), lambda i, j: (i, j, 0))],
        out_specs=pl.BlockSpec((1, QBLK, qw), lambda i, j: (i, j, 0)),
        out_shape=jax.ShapeDtypeStruct((b, t, qw), BF16),
        scratch_shapes=[pltpu.VMEM((nc // 2, 2 * QBLK), F32)],
        compiler_params=_cparams("arbitrary", "arbitrary"),
        name="nsa_prompt_attn",
    )(q_bf, kc, vc, rows_bf, rows_bf, win_bf, win_bf, gl)


def _rglru_gates(xc, wa_ref, ba_ref, wx_ref, bx_ref, lam_ref):
    xb = xc.astype(BF16)
    r = _sigmoid(_dot(xb, wa_ref[...]) + ba_ref[...])
    i = _sigmoid(_dot(xb, wx_ref[...]) + bx_ref[...])
    nl = -lam_ref[...]
    softplus = jnp.maximum(nl, 0.0) + jnp.log1p(jnp.exp(-jnp.abs(nl)))
    log_a = -RG_C * r * softplus
    a = jnp.exp(log_a)
    u = jnp.sqrt(-jnp.tanh(log_a) * (a * a + 1.0)) * (i * xc)
    return a, u


def _rglru_prompt_kernel(xr_ref, gr_ref, cw_ref, cb_ref, wa_ref, ba_ref, wx_ref, bx_ref, lam_ref,
                         o_ref, hl_ref, cl_ref, hcar, xcar):
    i = pl.program_id(1)
    tm, c = xr_ref.shape[1], xr_ref.shape[2]

    @pl.when(i == 0)
    def _():
        hcar[...] = jnp.zeros(hcar.shape, F32)
        xcar[...] = jnp.zeros(xcar.shape, F32)

    x = xr_ref[0]
    prev = xcar[...]
    row = lax.broadcasted_iota(jnp.int32, (tm, c), 0)
    xc = cb_ref[...] + cw_ref[3:4, :] * x
    for k in range(1, 4):
        cur = pltpu.roll(x, k, axis=0)
        old = jnp.tile(pltpu.roll(prev, k, axis=0), (tm // 8, 1))
        xc = xc + cw_ref[3 - k:4 - k, :] * jnp.where(row < k, old, cur)
    a, u = _rglru_gates(xc, wa_ref, ba_ref, wx_ref, bx_ref, lam_ref)

    s = 1
    while s < tm:
        a_sh = jnp.where(row >= s, pltpu.roll(a, s, axis=0), 1.0)
        u_sh = jnp.where(row >= s, pltpu.roll(u, s, axis=0), 0.0)
        u = a * u_sh + u
        a = a * a_sh
        s *= 2
    h = a * hcar[0:1, :] + u
    o_ref[0] = (h * _gelu(gr_ref[0])).astype(o_ref.dtype)
    hcar[...] = jnp.broadcast_to(h[tm - 1:tm, :], hcar.shape)
    xcar[...] = x[tm - 8:tm, :]
    hl_ref[0] = hcar[...]
    cl_ref[0] = xcar[...]


def rglru_prompt(xr, gr, rw, tm):
    b, t, c = xr.shape
    tm = min(tm, t)
    const = lambda a: pl.BlockSpec(a.shape, lambda i, j: (0,) * a.ndim)
    ws = (rw["conv_w"], rw["conv_b"], rw["wa"], rw["ba"], rw["wx"], rw["bx"], rw["lam"])
    tile = pl.BlockSpec((1, tm, c), lambda i, j: (i, j, 0))
    last = pl.BlockSpec((1, 8, c), lambda i, j: (i, 0, 0))
    return pl.pallas_call(
        _rglru_prompt_kernel,
        grid=(b, t // tm),
        in_specs=[tile, tile] + [const(a) for a in ws],
        out_specs=[tile, last, last],
        out_shape=[jax.ShapeDtypeStruct((b, t, c), BF16),
                   jax.ShapeDtypeStruct((b, 8, c), F32),
                   jax.ShapeDtypeStruct((b, 8, c), F32)],
        scratch_shapes=[pltpu.VMEM((8, c), F32), pltpu.VMEM((8, c), F32)],
        compiler_params=_cparams("arbitrary", "arbitrary"),
        name="rglru_prompt",
    )(xr, gr, *ws)


def _gmlp_v(v_raw, vg_ref):
    v = _gelu(v_raw)
    ms = jnp.mean(v * v, axis=-1, keepdims=True)
    return v * lax.rsqrt(ms + RMS_EPS) * vg_ref[...]


def _gmlp_prompt_kernel(u_ref, v_ref, vg_ref, ws_ref, bs_ref, o_ref):
    lc = u_ref.shape[1]
    v = _gmlp_v(v_ref[0], vg_ref).astype(BF16)
    tril = (lax.broadcasted_iota(jnp.int32, (lc, lc), 0) >= lax.broadcasted_iota(jnp.int32, (lc, lc), 1))
    lane = lax.broadcasted_iota(jnp.int32, (lc, LANES), 1)
    parts = []
    for j in range(v.shape[1] // LANES):
        vj = v[:, j * LANES:(j + 1) * LANES]
        w0 = jnp.where(tril, ws_ref[2 * j], 0.0).astype(BF16)
        w1 = jnp.where(tril, ws_ref[2 * j + 1], 0.0).astype(BF16)
        parts.append(jnp.where(lane < HEAD_DIM, _dot(w0, vj), _dot(w1, vj)))
    mixed = jnp.concatenate(parts, axis=1) + bs_ref[...]
    o_ref[0] = (_gelu(u_ref[0]) * mixed).astype(o_ref.dtype)


def gmlp_prompt(u, v, v_gain, ws, bs_exp):
    b, t, c = u.shape
    lc = ws.shape[1]
    tile = pl.BlockSpec((1, lc, c), lambda i, j: (i, j, 0))
    const = lambda a: pl.BlockSpec(a.shape, lambda i, j: (0,) * a.ndim)
    return pl.pallas_call(
        _gmlp_prompt_kernel,
        grid=(b, t // lc),
        in_specs=[tile, tile, const(v_gain), const(ws), const(bs_exp)],
        out_specs=tile,
        out_shape=jax.ShapeDtypeStruct((b, t, c), BF16),
        compiler_params=_cparams("arbitrary", "arbitrary"),
        name="gmlp_prompt",
    )(u, v, v_gain, ws, bs_exp)


DIL_SPAN = DIL_MAX
DIL_UNROLL = 4


def _dil_prompt_kernel(q_ref, k_ref, v_ref, o_ref, acc_ref, m_ref, l_ref):
    span = q_ref.shape[1]
    base = pl.program_id(2) * span
    scale = HEAD_DIM ** -0.5
    lane = lax.broadcasted_iota(jnp.int32, (QBLK, LANES), 1)
    row = lax.broadcasted_iota(jnp.int32, (QBLK, LANES), 0)
    lo = lane < HEAD_DIM
    band_prev = jnp.where(lane >= row, 0.0, NEG_INF)
    band_diag = jnp.where(lane <= row, 0.0, NEG_INF)
    band_prev = jnp.concatenate([band_prev, band_prev], axis=0)
    band_diag = jnp.concatenate([band_diag, band_diag], axis=0)

    for p, (window, dil) in enumerate(DIL_PATTERNS):
        assert window // dil == QBLK and span % (QBLK * dil) == 0

        def tile(idx, carry, p=p, dil=dil):
            start = (idx // dil) * (QBLK * dil) + idx % dil
            g0 = base + start
            rows = pl.ds(start, QBLK, stride=dil)
            q = (q_ref[0, rows, :] * scale).astype(BF16)
            zero = jnp.zeros((), BF16)
            q2 = jnp.concatenate([jnp.where(lo, q, zero), jnp.where(lo, zero, q)], axis=0)
            has_prev = g0 >= QBLK * dil
            d_rows = pl.ds(g0, QBLK, stride=dil)
            p_rows = pl.ds(jnp.where(has_prev, g0 - QBLK * dil, g0), QBLK, stride=dil)
            s_d = _dot_nt(q2, k_ref[0, d_rows, :].astype(BF16)) + band_diag
            s_p = _dot_nt(q2, k_ref[0, p_rows, :].astype(BF16)) + (band_prev + jnp.where(has_prev, 0.0, NEG_INF))
            m = jnp.max(jnp.maximum(s_d, s_p), axis=-1, keepdims=True)
            e_d = jnp.exp(s_d - m)
            e_p = jnp.exp(s_p - m)
            l = jnp.sum(e_d + e_p, axis=-1, keepdims=True)
            acc = (_dot(e_d.astype(BF16), v_ref[0, d_rows, :].astype(BF16))
                   + _dot(e_p.astype(BF16), v_ref[0, p_rows, :].astype(BF16)))
            acc_ref[p, rows, :] = jnp.where(lo, acc[:QBLK], acc[QBLK:])
            m_ref[p, rows, :] = jnp.where(lo, m[:QBLK], m[QBLK:])
            l_ref[p, rows, :] = jnp.where(lo, l[:QBLK], l[QBLK:])
            return carry

        lax.fori_loop(0, span // QBLK, tile, 0, unroll=DIL_UNROLL)

    def combine(c, carry):
        rows = pl.ds(pl.multiple_of(c * QBLK, QBLK), QBLK)
        ms = [m_ref[p, rows, :] for p in range(len(DIL_PATTERNS))]
        mx = functools.reduce(jnp.maximum, ms)
        num, den = 0.0, 0.0
        for p, mp in enumerate(ms):
            w = jnp.exp(mp - mx)
            num = num + w * acc_ref[p, rows, :]
            den = den + w * l_ref[p, rows, :]
        o_ref[0, rows, :] = (num * (1.0 / den)).astype(o_ref.dtype)
        return carry

    lax.fori_loop(0, span // QBLK, combine, 0)


def dil_prompt_attn(q, kv):
    b, t, w = q.shape
    npair = w // LANES
    span = min(DIL_SPAN, t)
    n_pat = len(DIL_PATTERNS)
    return pl.pallas_call(
        _dil_prompt_kernel,
        grid=(b, npair, t // span),
        in_specs=[pl.BlockSpec((1, span, LANES), lambda i, p, j: (i, j, p)),
                  pl.BlockSpec((1, t, LANES), lambda i, p, j: (i, 0, p)),
                  pl.BlockSpec((1, t, LANES), lambda i, p, j: (i, 0, npair + p))],
        out_specs=pl.BlockSpec((1, span, LANES), lambda i, p, j: (i, j, p)),
        out_shape=jax.ShapeDtypeStruct((b, t, w), BF16),
        scratch_shapes=[pltpu.VMEM((n_pat, span, LANES), F32)] * 3,
        compiler_params=_cparams("arbitrary", "arbitrary", "arbitrary"),
        name="dil_prompt_attn",
    )(q, kv, kv)


NSA_G, NSA_REP = 2, 4
NSA_QW = NSA_G * NSA_REP * HEAD_DIM
QPERM = np.arange(NSA_QW).reshape(NSA_G, NSA_REP, HEAD_DIM).transpose(1, 0, 2).reshape(-1)


def _block_diag(blocks):
    n, a, b = blocks.shape
    out = jnp.zeros((n, a, n, b), blocks.dtype)
    out = out.at[jnp.arange(n), :, jnp.arange(n), :].set(blocks)
    return out.reshape(n * a, n * b)


def prep_layer0(p):
    w_in = p["w_in"]
    d = w_in.shape[0]
    n_gate = NSA_G * NSA_REP * 3
    kv_w = 6 * NSA_G * HEAD_DIM
    c_gl = NSA_QW + kv_w
    c_xr = c_gl + n_gate
    d_rnn = (w_in.shape[1] - c_xr) // 2
    cols = np.concatenate([QPERM, np.arange(NSA_QW, c_gl), np.arange(c_xr, c_xr + 2 * d_rnn),
                           np.arange(c_gl, c_xr)])
    w = jnp.concatenate([w_in[:, cols], jnp.zeros((d, LANES - n_gate), w_in.dtype)], axis=1).astype(BF16)
    n = w.shape[1]
    hg = jnp.ones((n,), F32)
    hg = hg.at[0:NSA_QW].set(jnp.tile(p["q_gain"], NSA_QW // HEAD_DIM))
    hg = hg.at[NSA_QW + 2 * LANES:NSA_QW + 3 * LANES].set(jnp.tile(p["k_gain"][1], 2))
    hg = hg.at[NSA_QW + 4 * LANES:NSA_QW + 5 * LANES].set(jnp.tile(p["k_gain"][2], 2))
    qb = NSA_QW // LANES
    c_rows, c_win, c_x, c_g, c_l = NSA_QW, NSA_QW + 4 * LANES, NSA_QW + 6 * LANES, NSA_QW + 6 * LANES + d_rnn, \
        NSA_QW + 6 * LANES + 2 * d_rnn
    outs = ((0, NSA_QW, BF16), (c_rows, 4 * LANES, F32), (c_rows, 4 * LANES, BF16), (c_win, 2 * LANES, F32),
            (c_win, 2 * LANES, BF16), (c_x, d_rnn, F32), (c_g, d_rnn, F32), (c_l, LANES, F32))
    cw = {}
    for c, nm in enumerate("kv"):
        w1 = p["cmp_w1"][c]
        big = jnp.zeros((CMP_BLOCK, NSA_G, HEAD_DIM, NSA_G, HEAD_DIM), F32)
        for g in range(NSA_G):
            big = big.at[:, g, :, g, :].set(w1)
        cw["w1" + nm] = big.reshape(CMP_BLOCK * LANES, LANES).astype(BF16)
        cw["pos" + nm] = jnp.tile(p["cmp_pos"][c], (1, NSA_G)).reshape(1, CMP_BLOCK * LANES)
        cw["w2" + nm] = _block_diag(jnp.stack([p["cmp_w2"][c]] * NSA_G)).astype(BF16)
    cw["kg0"] = jnp.tile(p["k_gain"][0], 2).reshape(1, LANES)
    rw = dict(conv_w=p["conv_w"], conv_b=p["conv_b"].reshape(1, -1),
              wa=_block_diag(p["wa"]).astype(BF16), ba=p["ba"].reshape(1, -1),
              wx=_block_diag(p["wx"]).astype(BF16), bx=p["bx"].reshape(1, -1), lam=p["lam"].reshape(1, -1))
    w_out = p["w_out"]
    return dict(w=w, hg=hg.reshape(1, n), norm_blocks=tuple(range(qb)) + (qb + 2, qb + 4), outs=outs,
                cw=cw, rw=rw, wo_a=w_out[QPERM].astype(BF16), wo_b=w_out[NSA_QW:].astype(BF16))


def prep_layer1(p):
    w = p["w_in"].astype(BF16)
    n = w.shape[1]
    c_w = p["v_gain"].shape[0]
    dil_w = (n - 2 * c_w) // 3
    hg = jnp.ones((n,), F32)
    hg = hg.at[2 * c_w:2 * c_w + dil_w].set(jnp.tile(p["q_gain"], dil_w // HEAD_DIM))
    hg = hg.at[2 * c_w + dil_w:2 * c_w + 2 * dil_w].set(jnp.tile(p["k_gain"], dil_w // HEAD_DIM))
    b0 = 2 * c_w // LANES
    nbq = dil_w // LANES
    outs = ((0, c_w, F32), (c_w, c_w, F32), (2 * c_w, dil_w, F32), (2 * c_w + dil_w, 2 * dil_w, F32))
    w_out = p["w_out"]
    gw = c_w // p["ws"].shape[0]
    return dict(w=w, hg=hg.reshape(1, n), norm_blocks=tuple(range(b0, b0 + 2 * nbq)), outs=outs,
                v_gain=p["v_gain"].reshape(1, c_w), ws=p["ws"], bs_exp=jnp.repeat(p["bs"].T, gw, axis=1),
                ws_diag=jnp.repeat(p["ws"][:, 0, 0], gw).reshape(1, c_w),
                bs0=jnp.repeat(p["bs"][:, 0], gw).reshape(1, c_w),
                wo_a=w_out[:c_w].astype(BF16), wo_b=w_out[c_w:].astype(BF16))


def layer0_prompt(x, mod, ln_mix, ln_ffn, ffn_w, pp):
    b, t, _ = x.shape
    q_bf, rows, rows_bf, win, win_bf, xr, gr, gl = in_proj(
        x, mod, 0, 1, ln_mix, pp["w"], pp["hg"], pp["norm_blocks"], pp["outs"], tm=512)
    kc, vc = nsa_compress(rows, pp["cw"])
    o_nsa = nsa_prompt_attn(q_bf, kc, vc, rows_bf, win_bf, gl)
    o_rnn, h_last, conv_last = rglru_prompt(xr, gr, pp["rw"], tm=256)
    y = out_ffn(o_nsa, o_rnn, x, mod, ln_ffn, pp["wo_a"], pp["wo_b"], *ffn_w, tm=512)
    nwin = min(NSA_WINDOW, t)
    state = (rows.reshape(b, t, 4, NSA_G, HEAD_DIM), win[:, t - nwin:].reshape(b, nwin, 2, NSA_G, HEAD_DIM),
             h_last[:, 0], conv_last[:, 5:8])
    return y, state


def layer1_prompt(x, mod, ln_mix, ln_ffn, ffn_w, pp):
    b, t, _ = x.shape
    u, v, q, kv = in_proj(x, mod, 0, 1, ln_mix, pp["w"], pp["hg"], pp["norm_blocks"], pp["outs"], tm=512)
    o_c = gmlp_prompt(u, v, pp["v_gain"], pp["ws"], pp["bs_exp"])
    o_d = dil_prompt_attn(q, kv)
    y = out_ffn(o_c, o_d, x, mod, ln_ffn, pp["wo_a"], pp["wo_b"], *ffn_w, tm=512)
    nkv = min(DIL_MAX, t)
    heads = kv.shape[2] // (2 * HEAD_DIM)
    return y, kv[:, t - nkv:].reshape(b, nkv, 2, heads, HEAD_DIM)


def _row_softmax_parts(s_list, mask_list, s_new):
    m = s_new[0]
    for sn in s_new[1:]:
        m = jnp.maximum(m, sn)
    for s, mk in zip(s_list, mask_list):
        sm = s if mk is None else jnp.where(mk, s, NEG_INF)
        m = jnp.maximum(m, jnp.max(sm, axis=-1, keepdims=True))
    es, den = [], 0.0
    for s, mk in zip(s_list, mask_list):
        e = jnp.exp(s - m)
        if mk is not None:
            e = jnp.where(mk, e, 0.0)
        es.append(e)
        den = den + jnp.sum(e, axis=-1, keepdims=True)
    en = [jnp.exp(sn - m) for sn in s_new]
    for e in en:
        den = den + e
    return es, en, 1.0 / den


def _nsa_decode_kernel(pt_ref, cache_ref, q_ref, rn_ref, wn_ref, gl_ref, sw_ref,
                       w1k_ref, posk_ref, w2k_ref, w1v_ref, posv_ref, w2v_ref, kg_ref, ex_ref,
                       o_ref, wo_ref, bufk, bufv, bufs, sem):
    b = pl.program_id(0)
    nbatch = pl.num_programs(0)
    slot = b % 2
    n_pages = pt_ref.shape[1]
    page = cache_ref.shape[1]
    past = n_pages * page
    nc = past // CMP_BLOCK
    nb = past // SEL_BLOCK
    scale = HEAD_DIM ** -0.5
    rep = q_ref.shape[2] // LANES
    nh = 2 * rep

    def copies(bb, sl):
        out = []
        for j in range(n_pages):
            pg = pt_ref[bb, j]
            dst = pl.ds(j * page, page)
            out.append(pltpu.make_async_copy(cache_ref.at[pg, :, pl.ds(0, LANES)], bufk.at[sl, dst], sem.at[sl, 0, j]))
            out.append(pltpu.make_async_copy(cache_ref.at[pg, :, pl.ds(LANES, LANES)], bufv.at[sl, dst],
                                             sem.at[sl, 1, j]))
            out.append(pltpu.make_async_copy(cache_ref.at[pg, :, pl.ds(2 * LANES, 2 * LANES)], bufs.at[sl, dst],
                                             sem.at[sl, 2, j]))
        return out

    @pl.when(b == 0)
    def _():
        for c in copies(0, 0):
            c.start()

    @pl.when(b + 1 < nbatch)
    def _():
        for c in copies(b + 1, 1 - slot):
            c.start()

    for c in copies(b, slot):
        c.wait()

    lane = lax.broadcasted_iota(jnp.int32, (nh, LANES), 1)
    hrow = lax.broadcasted_iota(jnp.int32, (nh, LANES), 0)
    q = q_ref[0]
    qrows = []
    for g in range(2):
        for r in range(rep):
            qrows.append(q[:, r * LANES:(r + 1) * LANES])
    qm = jnp.concatenate(qrows, axis=0)
    qm = jnp.where((lane // HEAD_DIM) == (hrow // rep), qm, jnp.zeros((), qm.dtype))
    qf = qm.astype(F32)

    def padded(c2):
        z = jnp.zeros((LANES - nc // 2, LANES), F32)
        return jnp.concatenate([c2[:nc // 2], z, c2[nc // 2:], z], axis=0)
    ldk = lambda s, n, st: bufk[slot, pl.ds(s, n, stride=st), :]
    ldv = lambda s, n, st: bufv[slot, pl.ds(s, n, stride=st), :]
    kc = _head_norm(_compress_rows(ldk, nc // 2, w1k_ref, posk_ref, w2k_ref), kg_ref[...])
    vc = _compress_rows(ldv, nc // 2, w1v_ref, posv_ref, w2v_ref)
    kcp = padded(kc).astype(BF16)
    vcp = padded(vc).astype(BF16)
    s_c = _dot_nt(qm, kcp) * scale
    cl = lax.broadcasted_iota(jnp.int32, (nh, 2 * LANES), 1)
    cmask = (cl % LANES) < nc // 2
    s_c = jnp.where(cmask, s_c, NEG_INF)
    e_c = jnp.where(cmask, jnp.exp(s_c - jnp.max(s_c, axis=-1, keepdims=True)), 0.0)
    p_c = e_c * (1.0 / jnp.maximum(jnp.sum(e_c, axis=-1, keepdims=True), 1e-30))
    o_cmp = _dot(p_c.astype(BF16), vcp)

    pp = p_c[:, :LANES] + p_c[:, LANES:]
    imps = []
    for g in range(2):
        ig = pp[g * rep:g * rep + 1]
        for r in range(1, rep):
            ig = ig + pp[g * rep + r:g * rep + r + 1]
        imps += [ig] * rep
    imp = jnp.concatenate(imps, axis=0)
    imp = jnp.where((lane == 0) | (lane == nb), FORCE_SCORE, imp)
    imp = jnp.where(lane <= nb, imp, -3.0)
    sel = _select_blocks(imp, lane.astype(F32), min(N_SEL, nb + 1))
    selm = _dot(sel.astype(BF16), ex_ref[...]) > 0.5

    rn = rn_ref[0]
    s_sel = _dot_nt(qm, bufs[slot, :, pl.ds(0, LANES)].astype(BF16)) * scale
    s_sel_new = jnp.sum(qf * rn[:, 2 * LANES:3 * LANES], axis=-1, keepdims=True) * scale
    (e_s,), (e_sn,), inv_s = _row_softmax_parts([s_sel], [selm], [s_sel_new])
    o_sel = (_dot(e_s.astype(BF16), bufs[slot, :, pl.ds(LANES, LANES)].astype(BF16))
             + e_sn * rn[:, 3 * LANES:4 * LANES]) * inv_s

    sw = sw_ref[0]
    wn = wn_ref[0]
    s_w = _dot_nt(qm, sw[:, :LANES].astype(BF16)) * scale
    s_w_new = jnp.sum(qf * wn[:, :LANES], axis=-1, keepdims=True) * scale
    (e_w,), (e_wn,), inv_w = _row_softmax_parts([s_w], [None], [s_w_new])
    o_win = (_dot(e_w.astype(BF16), sw[:, LANES:].astype(BF16)) + e_wn * wn[:, LANES:]) * inv_w

    gate = _sigmoid(gl_ref[0])
    o = gate[:, 0:1] * o_cmp + gate[:, 1:2] * o_sel + gate[:, 2:3] * o_win
    l1 = lax.broadcasted_iota(jnp.int32, (1, LANES), 1)
    o_ref[0] = jnp.concatenate([jnp.where(l1 < HEAD_DIM, o[r:r + 1], o[rep + r:rep + r + 1])
                                for r in range(rep)], axis=1).astype(o_ref.dtype)

    wb = sw.shape[0]
    wrow = lax.broadcasted_iota(jnp.int32, sw.shape, 0)
    wo_ref[0] = jnp.where(wrow == wb - 1, wn, pltpu.roll(sw, wb - 1, axis=0))


def nsa_decode(page_table, cache, q_bf, rows_new, win_new, gl3, state_win, cw):
    b, n_pages = page_table.shape
    page = cache.shape[1]
    past = n_pages * page
    nb = past // SEL_BLOCK
    assert past % SEL_BLOCK == 0 and nb < LANES and past // CMP_BLOCK <= 2 * LANES
    wb = state_win.shape[1]
    assert wb <= NSA_WINDOW
    expand = (jnp.arange(LANES)[:, None] == (jnp.arange(past) // SEL_BLOCK)[None, :]).astype(BF16)
    ws = (cw["w1k"], cw["posk"], cw["w2k"], cw["w1v"], cw["posv"], cw["w2v"], cw["kg0"], expand)
    per_b = lambda a: pl.BlockSpec((1,) + a.shape[1:], lambda i, pt: (i, 0, 0))
    const = lambda a: pl.BlockSpec(a.shape, lambda i, pt: (0,) * a.ndim)
    grid_spec = pltpu.PrefetchScalarGridSpec(
        num_scalar_prefetch=1,
        grid=(b,),
        in_specs=[pl.BlockSpec(memory_space=pl.ANY), per_b(q_bf), per_b(rows_new), per_b(win_new), per_b(gl3),
                  per_b(state_win)] + [const(a) for a in ws],
        out_specs=[pl.BlockSpec((1, 1, q_bf.shape[2]), lambda i, pt: (i, 0, 0)),
                   pl.BlockSpec((1, wb, state_win.shape[2]), lambda i, pt: (i, 0, 0))],
        scratch_shapes=[pltpu.VMEM((2, past, LANES), F32), pltpu.VMEM((2, past, LANES), F32),
                        pltpu.VMEM((2, past, 2 * LANES), F32), pltpu.SemaphoreType.DMA((2, 3, n_pages))],
    )
    return pl.pallas_call(
        _nsa_decode_kernel,
        grid_spec=grid_spec,
        out_shape=[jax.ShapeDtypeStruct((b, 1, q_bf.shape[2]), BF16),
                   jax.ShapeDtypeStruct(state_win.shape, F32)],
        compiler_params=_cparams("arbitrary"),
        name="nsa_decode",
    )(page_table, cache, q_bf, rows_new, win_new, gl3, state_win, *ws)


def _rglru_decode_kernel(xr_ref, gr_ref, cs_ref, h0_ref, cw_ref, cb_ref, wa_ref, ba_ref, wx_ref, bx_ref, lam_ref,
                         o_ref, h_ref, cn_ref):
    x = xr_ref[...]
    xc = cb_ref[...] + cw_ref[3:4, :] * x
    for k in range(3):
        xc = xc + cw_ref[k:k + 1, :] * cs_ref[k]
    a, u = _rglru_gates(xc, wa_ref, ba_ref, wx_ref, bx_ref, lam_ref)
    h = a * h0_ref[...] + u
    h_ref[...] = h
    o_ref[...] = (h * _gelu(gr_ref[...])).astype(o_ref.dtype)
    cn_ref[0] = cs_ref[1]
    cn_ref[1] = cs_ref[2]
    cn_ref[2] = x


def rglru_decode(xr, gr, conv_t, h0, rw):
    ws = (rw["conv_w"], rw["conv_b"], rw["wa"], rw["ba"], rw["wx"], rw["bx"], rw["lam"])
    return pl.pallas_call(
        _rglru_decode_kernel,
        out_shape=[jax.ShapeDtypeStruct(xr.shape, BF16), jax.ShapeDtypeStruct(xr.shape, F32),
                   jax.ShapeDtypeStruct(conv_t.shape, F32)],
        compiler_params=pltpu.CompilerParams(vmem_limit_bytes=VMEM_LIMIT),
        name="rglru_decode",
    )(xr, gr, conv_t, h0, *ws)


def _gmlp_decode_kernel(u_ref, v_ref, vg_ref, wd_ref, b0_ref, o_ref, vn_ref):
    v = _gmlp_v(v_ref[...], vg_ref)
    vn_ref[...] = v
    o_ref[...] = (_gelu(u_ref[...]) * (wd_ref[...] * v + b0_ref[...])).astype(o_ref.dtype)


def gmlp_decode(u, v, v_gain, ws_diag, bs0):
    return pl.pallas_call(
        _gmlp_decode_kernel,
        out_shape=[jax.ShapeDtypeStruct(u.shape, BF16), jax.ShapeDtypeStruct(u.shape, F32)],
        compiler_params=pltpu.CompilerParams(vmem_limit_bytes=VMEM_LIMIT),
        name="gmlp_decode",
    )(u, v, v_gain, ws_diag, bs0)


def _dil_decode_kernel(q_ref, kvn_ref, st_ref, o_ref, so_ref):
    nh, hd, wb = st_ref.shape[2], st_ref.shape[3], st_ref.shape[4]
    scale = HEAD_DIM ** -0.5
    n_pat = len(DIL_PATTERNS)
    lane1 = lax.broadcasted_iota(jnp.int32, (1, wb), 1)
    dist = wb - lane1
    cnt = jnp.zeros((1, wb), F32)
    for window, dil in DIL_PATTERNS:
        cnt = cnt + jnp.where((dist <= window) & ((dist & (dil - 1)) == 0), 1.0, 0.0)
    last = lax.broadcasted_iota(jnp.int32, (hd, wb), 1) == wb - 1

    def shifted(x, new_col):
        return jnp.where(last, new_col, pltpu.roll(x, wb - 1, axis=1))

    s_rows, s_new_rows = [], []
    for h in range(nh):
        kh = st_ref[0, 0, h]
        qh = q_ref[0, h] * scale
        kn = kvn_ref[0, 0, h]
        s_rows.append(jnp.sum(kh * qh, axis=0, keepdims=True))
        s_new_rows.append(jnp.sum(kn * qh, axis=0, keepdims=True))
        so_ref[0, 0, h] = shifted(kh, kn)
    s = jnp.concatenate(s_rows, axis=0)
    s_new = jnp.concatenate(s_new_rows, axis=0)
    s = jnp.where(cnt > 0.0, s, NEG_INF)
    m = jnp.maximum(jnp.max(s, axis=-1, keepdims=True), s_new)
    e = cnt * jnp.exp(s - m)
    e_new = n_pat * jnp.exp(s_new - m)
    inv = 1.0 / (jnp.sum(e, axis=-1, keepdims=True) + e_new)
    for h in range(nh):
        vh = st_ref[0, 1, h]
        vn = kvn_ref[0, 1, h]
        acc = jnp.sum(vh * e[h:h + 1, :], axis=1, keepdims=True) + e_new[h:h + 1, :] * vn
        o_ref[0, h] = acc * inv[h:h + 1, :]
        so_ref[0, 1, h] = shifted(vh, vn)


def dil_decode(q, kv_new, state):
    b, wb, _, nh, hd = state.shape
    assert wb == DIL_MAX
    st_t = jnp.transpose(state, (0, 2, 3, 4, 1))
    o, so_t = pl.pallas_call(
        _dil_decode_kernel,
        grid=(b,),
        in_specs=[pl.BlockSpec((1, nh, hd, 1), lambda i: (i, 0, 0, 0)),
                  pl.BlockSpec((1, 2, nh, hd, 1), lambda i: (i, 0, 0, 0, 0)),
                  pl.BlockSpec((1, 2, nh, hd, wb), lambda i: (i, 0, 0, 0, 0))],
        out_specs=[pl.BlockSpec((1, nh, hd, 1), lambda i: (i, 0, 0, 0)),
                   pl.BlockSpec((1, 2, nh, hd, wb), lambda i: (i, 0, 0, 0, 0))],
        out_shape=[jax.ShapeDtypeStruct((b, nh, hd, 1), F32), jax.ShapeDtypeStruct(st_t.shape, F32)],
        compiler_params=_cparams("arbitrary"),
        name="dil_decode",
    )(q[..., None], kv_new[..., None], st_t)
    return o[..., 0], jnp.transpose(so_t, (0, 4, 1, 2, 3))


def layer0_sample(x, mod, ln_mix, ln_ffn, ffn_w, pp, cache, page_table, state_win, state_h, state_conv):
    b, _, d = x.shape
    xs = x.reshape(1, b, d)
    q_bf, rows, _, win, _, xr, gr, gl = in_proj(
        xs, mod, 0, 1, ln_mix, pp["w"], pp["hg"], pp["norm_blocks"], pp["outs"], tm=b)
    n_gate = NSA_G * NSA_REP * 3
    gl3 = gl[0, :, :n_gate].reshape(b, NSA_G * NSA_REP, 3)
    wb = state_win.shape[1]
    o_nsa, win_out = nsa_decode(
        page_table, cache.reshape(cache.shape[0], cache.shape[1], -1), q_bf.reshape(b, 1, -1),
        rows.reshape(b, 1, -1), win.reshape(b, 1, -1), gl3, state_win.reshape(b, wb, -1), pp["cw"])
    o_rnn, h_new, conv_new = rglru_decode(xr[0], gr[0], state_conv.transpose(1, 0, 2), state_h, pp["rw"])
    y = out_ffn(o_nsa.reshape(1, b, -1), o_rnn[None], xs, mod, ln_ffn, pp["wo_a"], pp["wo_b"], *ffn_w, tm=b)
    state = (rows.reshape(b, 1, 4, NSA_G, HEAD_DIM), win_out.reshape(b, wb, 2, NSA_G, HEAD_DIM), h_new,
             conv_new.transpose(1, 0, 2))
    return y.reshape(b, 1, d), state


def layer1_sample(x, mod, ln_mix, ln_ffn, ffn_w, pp, state_dil):
    b, _, d = x.shape
    xs = x.reshape(1, b, d)
    u, v, q, kv = in_proj(xs, mod, 0, 1, ln_mix, pp["w"], pp["hg"], pp["norm_blocks"], pp["outs"], tm=b)
    o_c, v_n = gmlp_decode(u[0], v[0], pp["v_gain"], pp["ws_diag"], pp["bs0"])
    heads = state_dil.shape[3]
    o_d, dil_out = dil_decode(q.reshape(b, heads, HEAD_DIM), kv.reshape(b, 2, heads, HEAD_DIM), state_dil)
    y = out_ffn(o_c[None], o_d.reshape(1, b, -1).astype(BF16), xs, mod, ln_ffn, pp["wo_a"], pp["wo_b"], *ffn_w,
                tm=b)
    return y.reshape(b, 1, d), (dil_out, v_n.reshape(b, 1, -1))


def kernel(x_prompt, x_sample, cache_nsa_kv, state_nsa_win, state_rglru_h, state_rglru_conv, state_dil_kv, page_table, c_prompt, c_sample, norm_mix_g, norm_ffn_g, w_ada, b_ada, w_ffn_gate, w_ffn_up, w_ffn_down, w_in_ab, w_out_ab, nsa_q_gain, nsa_k_gain, nsa_cmp_w1, nsa_cmp_w2, nsa_cmp_pos, rg_conv_w, rg_conv_b, rg_wa, rg_ba, rg_wx, rg_bx, rg_lambda, w_in_cd, w_out_cd, gmlp_v_gain, gmlp_ws, gmlp_bs, dil_q_gain, dil_k_gain):
    depth = norm_mix_g.shape[0]
    bp, bs = x_prompt.shape[0], x_sample.shape[0]
    pad = -(bp + bs) % 8
    c_all = jnp.concatenate([c_prompt, c_sample, jnp.zeros((pad, c_prompt.shape[1]), F32)], axis=0)
    mod_all = ada_mod(c_all, w_ada.astype(BF16), b_ada)
    yp, ys = x_prompt, x_sample
    kv_p, kv_s, win_p, win_s, h_p, h_s, conv_p, conv_s, dil_p, dil_s, gv_s = ([] for _ in range(11))
    for layer in range(depth):
        i = layer // 2
        mod_p = mod_all[layer, :bp, None, :]
        mod_s = mod_all[layer, None, bp:bp + bs, :]
        ffn_w = (w_ffn_gate[layer].astype(BF16), w_ffn_up[layer].astype(BF16), w_ffn_down[layer].astype(BF16))
        if layer % 2 == 0:
            pp = prep_layer0(dict(w_in=w_in_ab[i], w_out=w_out_ab[i], q_gain=nsa_q_gain[i], k_gain=nsa_k_gain[i],
                                  cmp_w1=nsa_cmp_w1[i], cmp_w2=nsa_cmp_w2[i], cmp_pos=nsa_cmp_pos[i],
                                  conv_w=rg_conv_w[i], conv_b=rg_conv_b[i], wa=rg_wa[i], ba=rg_ba[i],
                                  wx=rg_wx[i], bx=rg_bx[i], lam=rg_lambda[i]))
            yp, st = layer0_prompt(yp, mod_p, norm_mix_g[layer], norm_ffn_g[layer], ffn_w, pp)
            kv_p.append(st[0]); win_p.append(st[1]); h_p.append(st[2]); conv_p.append(st[3])
            ys, st = layer0_sample(ys, mod_s, norm_mix_g[layer], norm_ffn_g[layer], ffn_w, pp, cache_nsa_kv[i],
                                   page_table, state_nsa_win[i], state_rglru_h[i], state_rglru_conv[i])
            kv_s.append(st[0]); win_s.append(st[1]); h_s.append(st[2]); conv_s.append(st[3])
        else:
            pp = prep_layer1(dict(w_in=w_in_cd[i], w_out=w_out_cd[i], v_gain=gmlp_v_gain[i], ws=gmlp_ws[i],
                                  bs=gmlp_bs[i], q_gain=dil_q_gain[i], k_gain=dil_k_gain[i]))
            yp, st = layer1_prompt(yp, mod_p, norm_mix_g[layer], norm_ffn_g[layer], ffn_w, pp)
            dil_p.append(st)
            ys, st = layer1_sample(ys, mod_s, norm_mix_g[layer], norm_ffn_g[layer], ffn_w, pp, state_dil_kv[i])
            dil_s.append(st[0]); gv_s.append(st[1])
    return (yp, ys, jnp.stack(kv_p), jnp.stack(kv_s), jnp.stack(win_p), jnp.stack(win_s),
            jnp.stack(h_p), jnp.stack(h_s), jnp.stack(conv_p), jnp.stack(conv_s),
            jnp.stack(dil_p), jnp.stack(dil_s), jnp.stack(gv_s))
```

```python
import functools

import numpy as np
import jax
import jax.numpy as jnp
from jax import lax
from jax.experimental import pallas as pl
from jax.experimental.pallas import tpu as pltpu

F32 = jnp.float32
BF16 = jnp.bfloat16

LANES = 128
HEAD_DIM = 64
QBLK = 128
CMP_BLOCK = 32
SEL_BLOCK = 64
N_SEL = 16
NSA_WINDOW = 512
FORCE_SCORE = 1.0e4
DIL_PATTERNS = ((128, 1), (512, 4), (2048, 16))
DIL_MAX = 2048
RG_C = 8.0
RMS_EPS = 1e-6
NEG_INF = -1e30
LOG2_E = 1.4426950408889634
VMEM_LIMIT = 56 * 1024 * 1024


def _cparams(*sem):
    return pltpu.CompilerParams(dimension_semantics=sem, vmem_limit_bytes=VMEM_LIMIT)


def _dot(a, b):
    return jnp.dot(a, b, preferred_element_type=F32)


def _dot_nt(a, b):
    return lax.dot_general(a, b, (((1,), (1,)), ((), ())), preferred_element_type=F32)


def _gelu(x):
    return 0.5 * x * (1.0 + jnp.tanh(np.sqrt(2.0 / np.pi) * (x + 0.044715 * (x * x * x))))


def _sigmoid(x):
    return 1.0 / (1.0 + jnp.exp(-x))


def _head_norm(z, gain):
    lo = lax.broadcasted_iota(jnp.int32, z.shape, 1) < HEAD_DIM
    z2 = z * z
    s_lo = jnp.sum(jnp.where(lo, z2, 0.0), axis=-1, keepdims=True)
    s_hi = jnp.sum(jnp.where(lo, 0.0, z2), axis=-1, keepdims=True)
    inv = lax.rsqrt(jnp.where(lo, s_lo, s_hi) * (1.0 / HEAD_DIM) + RMS_EPS)
    return z * inv * gain


def _ada_kernel(c_ref, w_ref, b_ref, o_ref):
    c = c_ref[...]
    s = c * _sigmoid(c)
    o_ref[0] = _dot(s.astype(BF16), w_ref[0]) + b_ref[0]


def ada_mod(c_all, w_ada, b_ada):
    m, d = c_all.shape
    nl, _, n = w_ada.shape
    tn = 1536
    return pl.pallas_call(
        _ada_kernel,
        grid=(nl, n // tn),
        in_specs=[pl.BlockSpec((m, d), lambda l, j: (0, 0)),
                  pl.BlockSpec((1, d, tn), lambda l, j: (l, 0, j)),
                  pl.BlockSpec((1, 1, tn), lambda l, j: (l, 0, j))],
        out_specs=pl.BlockSpec((1, m, tn), lambda l, j: (l, 0, j)),
        out_shape=jax.ShapeDtypeStruct((nl, m, n), F32),
        compiler_params=_cparams("arbitrary", "arbitrary"),
        name="ada_mod",
    )(c_all, w_ada, b_ada.reshape(nl, 1, n))


def _in_proj_kernel(x_ref, sc_ref, sh_ref, g_ref, w_ref, hg_ref, *out_refs, norm_blocks, outs):
    x = x_ref[0]
    ms = jnp.mean(x * x, axis=-1, keepdims=True)
    h = x * lax.rsqrt(ms + RMS_EPS) * g_ref[...]
    h = h * (1.0 + sc_ref[0]) + sh_ref[0]
    z = _dot(h.astype(BF16), w_ref[...])
    nblk = z.shape[1] // LANES
    blocks = []
    for j in range(nblk):
        zb = z[:, j * LANES:(j + 1) * LANES]
        if j in norm_blocks:
            zb = _head_norm(zb, hg_ref[:, j * LANES:(j + 1) * LANES])
        blocks.append(zb)
    for o_ref, (c0, width, _) in zip(out_refs, outs):
        for j in range(width // LANES):
            o_ref[0, :, j * LANES:(j + 1) * LANES] = blocks[c0 // LANES + j].astype(o_ref.dtype)


def in_proj(x, mod, sh_idx, sc_idx, g, w, head_gain, norm_blocks, outs, tm):
    bm, t, d = x.shape
    r = mod.shape[1]
    n = w.shape[1]
    tm = min(tm, t)
    if r == 1:
        mod_spec = lambda k: pl.BlockSpec((1, 1, d), lambda b, i: (b, 0, k))
    else:
        assert r == t and tm == t
        mod_spec = lambda k: pl.BlockSpec((1, tm, d), lambda b, i: (b, 0, k))
    kern = functools.partial(_in_proj_kernel, norm_blocks=tuple(norm_blocks), outs=tuple(outs))
    return pl.pallas_call(
        kern,
        grid=(bm, t // tm),
        in_specs=[pl.BlockSpec((1, tm, d), lambda b, i: (b, i, 0)),
                  mod_spec(sc_idx), mod_spec(sh_idx),
                  pl.BlockSpec((1, d), lambda b, i: (0, 0)),
                  pl.BlockSpec((d, n), lambda b, i: (0, 0)),
                  pl.BlockSpec((1, n), lambda b, i: (0, 0))],
        out_specs=[pl.BlockSpec((1, tm, wd), lambda b, i: (b, i, 0)) for (_, wd, _) in outs],
        out_shape=[jax.ShapeDtypeStruct((bm, t, wd), dt) for (_, wd, dt) in outs],
        compiler_params=_cparams("arbitrary", "arbitrary"),
        name="in_proj",
    )(x, mod, mod, g.reshape(1, d), w, head_gain)


def _out_ffn_kernel(ma_ref, mb_ref, x_ref, gm_ref, shf_ref, scf_ref, gf_ref, lnf_ref,
                    woa_ref, wob_ref, wg_ref, wu_ref, wd_ref, o_ref, *, hidden_chunk):
    mix = _dot(ma_ref[0], woa_ref[...]) + _dot(mb_ref[0], wob_ref[...])
    x1 = x_ref[0] + gm_ref[0] * mix
    ms = jnp.mean(x1 * x1, axis=-1, keepdims=True)
    hf = x1 * lax.rsqrt(ms + RMS_EPS) * lnf_ref[...]
    hf = (hf * (1.0 + scf_ref[0]) + shf_ref[0]).astype(BF16)
    hidden = wg_ref.shape[1]
    ffn = jnp.zeros(x1.shape, F32)
    for c0 in range(0, hidden, hidden_chunk):
        gt = _dot(hf, wg_ref[:, c0:c0 + hidden_chunk])
        up = _dot(hf, wu_ref[:, c0:c0 + hidden_chunk])
        act = (gt * _sigmoid(gt) * up).astype(BF16)
        ffn = ffn + _dot(act, wd_ref[c0:c0 + hidden_chunk, :])
    o_ref[0] = x1 + gf_ref[0] * ffn


def out_ffn(mix_a, mix_b, x, mod, ln_ffn, wo_a, wo_b, wg, wu, wd, tm):
    bm, t, d = x.shape
    r = mod.shape[1]
    tm = min(tm, t)
    ka, kb = mix_a.shape[2], mix_b.shape[2]
    hidden = wg.shape[1]
    if r == 1:
        mod_spec = lambda k: pl.BlockSpec((1, 1, d), lambda b, i: (b, 0, k))
    else:
        assert r == t and tm == t
        mod_spec = lambda k: pl.BlockSpec((1, tm, d), lambda b, i: (b, 0, k))
    const = lambda shape: pl.BlockSpec(shape, lambda b, i: (0,) * len(shape), pipeline_mode=pl.Buffered(1))
    kern = functools.partial(_out_ffn_kernel, hidden_chunk=hidden // 2)
    return pl.pallas_call(
        kern,
        grid=(bm, t // tm),
        in_specs=[pl.BlockSpec((1, tm, ka), lambda b, i: (b, i, 0)),
                  pl.BlockSpec((1, tm, kb), lambda b, i: (b, i, 0)),
                  pl.BlockSpec((1, tm, d), lambda b, i: (b, i, 0)),
                  mod_spec(2), mod_spec(3), mod_spec(4), mod_spec(5),
                  const((1, d)), const((ka, d)), const((kb, d)),
                  const((d, hidden)), const((d, hidden)), const((hidden, d))],
        out_specs=pl.BlockSpec((1, tm, d), lambda b, i: (b, i, 0)),
        out_shape=jax.ShapeDtypeStruct((bm, t, d), F32),
        compiler_params=_cparams("arbitrary", "arbitrary"),
        name="out_ffn",
    )(mix_a, mix_b, x, mod, mod, mod, mod, ln_ffn.reshape(1, d), wo_a, wo_b, wg, wu, wd)


def _compress_rows(load, nh, w1_ref, pos_ref, w2_ref):
    halves = []
    for parity in range(2):
        halves.append(jnp.concatenate(
            [load(parity * CMP_BLOCK + l, nh, 2 * CMP_BLOCK) for l in range(CMP_BLOCK)], axis=1))
    xs = (jnp.concatenate(halves, axis=0) + pos_ref[...]).astype(BF16)
    hid = _gelu(_dot(xs, w1_ref[...]))
    return _dot(hid.astype(BF16), w2_ref[...])


def _nsa_compress_kernel(rk_ref, rv_ref, w1k_ref, posk_ref, w2k_ref, w1v_ref, posv_ref, w2v_ref, kg_ref,
                         kc_ref, vc_ref):
    nh = kc_ref.shape[1] // 2
    ldk = lambda s, n, st: rk_ref[0, pl.ds(s, n, stride=st), :]
    ldv = lambda s, n, st: rv_ref[0, pl.ds(s, n, stride=st), :]
    kc = _compress_rows(ldk, nh, w1k_ref, posk_ref, w2k_ref)
    kc_ref[0] = _head_norm(kc, kg_ref[...]).astype(kc_ref.dtype)
    vc_ref[0] = _compress_rows(ldv, nh, w1v_ref, posv_ref, w2v_ref).astype(vc_ref.dtype)


def nsa_compress(rows, cw):
    b, t, _ = rows.shape
    nc = t // CMP_BLOCK
    const = lambda a: pl.BlockSpec(a.shape, lambda i: (0,) * a.ndim)
    ws = (cw["w1k"], cw["posk"], cw["w2k"], cw["w1v"], cw["posv"], cw["w2v"], cw["kg0"])
    return pl.pallas_call(
        _nsa_compress_kernel,
        grid=(b,),
        in_specs=[pl.BlockSpec((1, t, LANES), lambda i: (i, 0, 0)),
                  pl.BlockSpec((1, t, LANES), lambda i: (i, 0, 1))] + [const(a) for a in ws],
        out_specs=[pl.BlockSpec((1, nc, LANES), lambda i: (i, 0, 0))] * 2,
        out_shape=[jax.ShapeDtypeStruct((b, nc, LANES), BF16)] * 2,
        compiler_params=_cparams("arbitrary"),
        name="nsa_compress",
    )(rows, rows, *ws)


def _select_blocks(imp, blk_f, n_pick):
    sel = jnp.zeros(imp.shape, F32)
    work = imp
    for _ in range(n_pick):
        m = jnp.max(work, axis=-1, keepdims=True)
        idx = jnp.min(jnp.where(work == m, blk_f, 1e9), axis=-1, keepdims=True)
        pick = blk_f == idx
        sel = jnp.where(pick, 1.0, sel)
        work = jnp.where(pick, -2.0, work)
    return sel


def _select_blocks_t(imp, blk_f, n_pick):
    sel = jnp.zeros(imp.shape, F32)
    work = imp
    for _ in range(n_pick):
        m = jnp.max(work, axis=0, keepdims=True)
        idx = jnp.min(jnp.where(work == m, blk_f, 1e9), axis=0, keepdims=True)
        pick = blk_f == idx
        sel = jnp.where(pick, 1.0, sel)
        work = jnp.where(pick, -2.0, work)
    return sel


NSA_SEL_CHUNK = 8
NSA_DECODE_SEQS = 2


def _dot_tn(a, b):
    return lax.dot_general(a, b, (((0,), (0,)), ((), ())), preferred_element_type=F32)


def _nsa_prompt_kernel(q_ref, kc_ref, vc_ref, ks_ref, vs_ref, kw_ref, vw_ref, gl_ref, o_ref, sb_ref):
    qi = pl.program_id(1)
    t0 = qi * QBLK
    nc = kc_ref.shape[1]
    nb = nc // 2
    scale = HEAD_DIM ** -0.5
    rep = q_ref.shape[2] // LANES
    nhead = 2 * rep
    ncol = nhead * QBLK

    lane = lax.broadcasted_iota(jnp.int32, (QBLK, LANES), 1)
    sub = lax.broadcasted_iota(jnp.int32, (QBLK, LANES), 0)
    tile8 = lambda x: jnp.concatenate([x] * nhead, axis=1)
    q = q_ref[0]
    qs = []
    for g in range(2):
        gm = (lane >= g * HEAD_DIM) & (lane < (g + 1) * HEAD_DIM)
        for r in range(rep):
            qs.append(jnp.where(gm, q[:, r * LANES:(r + 1) * LANES], jnp.zeros((), q.dtype)))
    qall = (jnp.concatenate(qs, axis=0).astype(F32) * (scale * LOG2_E)).astype(BF16)

    crow = lax.broadcasted_iota(jnp.int32, (nc, QBLK), 0)
    cidx = jnp.where(crow < nb, 2 * crow, 2 * (crow - nb) + 1)
    cvalid = ((cidx + 1) * CMP_BLOCK - 1) <= t0 + lax.broadcasted_iota(jnp.int32, (nc, QBLK), 1)
    cbias = tile8(jnp.where(cvalid, 0.0, NEG_INF))
    cone = tile8(jnp.where(cvalid, 1.0, 0.0))
    sc = _dot_nt(kc_ref[0], qall) + cbias
    ec = jnp.exp2(sc - jnp.max(sc, axis=0, keepdims=True)) * cone
    p = ec * (1.0 / jnp.maximum(jnp.sum(ec, axis=0, keepdims=True), 1e-30))
    o_cmp = _dot_tn(vc_ref[0], p.astype(BF16))

    brow = lax.broadcasted_iota(jnp.int32, (nb, QBLK), 0)
    tb = t0 + lax.broadcasted_iota(jnp.int32, (nb, QBLK), 1)
    forced = (brow == 0) | (brow == tb // SEL_BLOCK)
    valid = brow * SEL_BLOCK <= tb
    blk_f = brow.astype(F32)
    for g in range(2):
        ps = p[:, (g * rep) * QBLK:(g * rep + 1) * QBLK]
        for r in range(1, rep):
            ps = ps + p[:, (g * rep + r) * QBLK:(g * rep + r + 1) * QBLK]
        imp = ps[:nb] + ps[nb:]
        imp = jnp.where(forced, FORCE_SCORE, imp)
        imp = jnp.where(valid, imp, -1.0)
        sel = _select_blocks_t(imp, blk_f, min(N_SEL, nb))
        sb_ref[:, g * QBLK:(g + 1) * QBLK] = jnp.where(sel > 0.5, 0.0, NEG_INF)

    def flash_step(k_ref, v_ref, start, nkeys, bias, carry):
        m_i, l_i, acc = carry
        rows = pl.ds(pl.multiple_of(start, QBLK), nkeys)
        s = _dot_nt(k_ref[0, rows, :], qall) + bias
        m_new = jnp.maximum(m_i, jnp.max(s, axis=0, keepdims=True))
        alpha = jnp.exp2(m_i - m_new)
        e = jnp.exp2(s - m_new)
        l_new = alpha * l_i + jnp.sum(e, axis=0, keepdims=True)
        acc_new = alpha * acc + _dot_tn(v_ref[0, rows, :], e.astype(BF16))
        return m_new, l_new, acc_new

    init = (jnp.full((1, ncol), NEG_INF, F32), jnp.zeros((1, ncol), F32), jnp.zeros((LANES, ncol), F32))

    def sel_bias(kc, causal):
        tiles = []
        for j in range(NSA_SEL_CHUNK):
            kt = kc * NSA_SEL_CHUNK + j
            r0 = sb_ref[pl.ds(2 * kt, 1), :]
            r1 = sb_ref[pl.ds(2 * kt + 1, 1), :]
            parts = []
            for g in range(2):
                bg = jnp.where(sub < SEL_BLOCK, r0[:, g * QBLK:(g + 1) * QBLK], r1[:, g * QBLK:(g + 1) * QBLK])
                if causal:
                    bg = bg + jnp.where(kt * QBLK + sub <= t0 + lane, 0.0, NEG_INF)
                parts += [bg] * rep
            tiles.append(jnp.concatenate(parts, axis=1))
        return jnp.concatenate(tiles, axis=0)

    chunk = NSA_SEL_CHUNK * QBLK
    last = qi // NSA_SEL_CHUNK
    carry = lax.fori_loop(0, last, lambda kc, c: flash_step(ks_ref, vs_ref, kc * chunk, chunk,
                                                            sel_bias(kc, False), c), init)
    _, l_s, acc_s = flash_step(ks_ref, vs_ref, last * chunk, chunk, sel_bias(last, True), carry)
    o_sel = acc_s * (1.0 / l_s)

    wkeys = NSA_WINDOW + QBLK
    wstart = jnp.maximum(qi - NSA_WINDOW // QBLK, 0) * QBLK
    dist = (t0 + lax.broadcasted_iota(jnp.int32, (wkeys, QBLK), 1)) \
        - (wstart + lax.broadcasted_iota(jnp.int32, (wkeys, QBLK), 0))
    wbias = tile8(jnp.where((dist >= 0) & (dist <= NSA_WINDOW), 0.0, NEG_INF))
    _, l_w, acc_w = flash_step(kw_ref, vw_ref, wstart, wkeys, wbias, init)
    o_win = acc_w * (1.0 / l_w)

    gate = _sigmoid(gl_ref[0]).T
    for r in range(rep):
        og = []
        for g in range(2):
            c = (g * rep + r) * 3
            cols = slice((g * rep + r) * QBLK, (g * rep + r + 1) * QBLK)
            og.append(gate[c:c + 1] * o_cmp[:, cols] + gate[c + 1:c + 2] * o_sel[:, cols]
                      + gate[c + 2:c + 3] * o_win[:, cols])
        o_ref[0, :, r * LANES:(r + 1) * LANES] = jnp.where(sub < HEAD_DIM, og[0], og[1]).T.astype(o_ref.dtype)


def nsa_prompt_attn(q_bf, kc, vc, rows_bf, win_bf, gl):
    b, t, qw = q_bf.shape
    nc = kc.shape[1]
    nq = t // QBLK
    assert t % (NSA_SEL_CHUNK * QBLK) == 0 and t >= NSA_WINDOW + QBLK
    full = lambda k: pl.BlockSpec((1, t, LANES), lambda i, j: (i, 0, k))
    return pl.pallas_call(
        _nsa_prompt_kernel,
        grid=(b, nq),
        in_specs=[pl.BlockSpec((1, QBLK, qw), lambda i, j: (i, j, 0)),
                  pl.BlockSpec((1, nc, LANES), lambda i, j: (i, 0, 0)),
                  pl.BlockSpec((1, nc, LANES), lambda i, j: (i, 0, 0)),
                  full(2), full(3), full(0), full(1),
                  pl.BlockSpec((1, QBLK, LANES), lambda i, j: (i, j, 0))],
        out_specs=pl.BlockSpec((1, QBLK, qw), lambda i, j: (i, j, 0)),
        out_shape=jax.ShapeDtypeStruct((b, t, qw), BF16),
        scratch_shapes=[pltpu.VMEM((nc // 2, 2 * QBLK), F32)],
        compiler_params=_cparams("arbitrary", "arbitrary"),
        name="nsa_prompt_attn",
    )(q_bf, kc, vc, rows_bf, rows_bf, win_bf, win_bf, gl)


def _rglru_gates(xc, wa_ref, ba_ref, wx_ref, bx_ref, lam_ref):
    xb = xc.astype(BF16)
    r = _sigmoid(_dot(xb, wa_ref[...]) + ba_ref[...])
    i = _sigmoid(_dot(xb, wx_ref[...]) + bx_ref[...])
    nl = -lam_ref[...]
    softplus = jnp.maximum(nl, 0.0) + jnp.log1p(jnp.exp(-jnp.abs(nl)))
    log_a = -RG_C * r * softplus
    a = jnp.exp(log_a)
    u = jnp.sqrt(-jnp.tanh(log_a) * (a * a + 1.0)) * (i * xc)
    return a, u


def _rglru_prompt_kernel(xr_ref, gr_ref, cw_ref, cb_ref, wa_ref, ba_ref, wx_ref, bx_ref, lam_ref,
                         o_ref, hl_ref, cl_ref, hcar, xcar):
    i = pl.program_id(1)
    tm, c = xr_ref.shape[1], xr_ref.shape[2]

    @pl.when(i == 0)
    def _():
        hcar[...] = jnp.zeros(hcar.shape, F32)
        xcar[...] = jnp.zeros(xcar.shape, F32)

    x = xr_ref[0]
    prev = xcar[...]
    row = lax.broadcasted_iota(jnp.int32, (tm, c), 0)
    xc = cb_ref[...] + cw_ref[3:4, :] * x
    for k in range(1, 4):
        cur = pltpu.roll(x, k, axis=0)
        old = jnp.tile(pltpu.roll(prev, k, axis=0), (tm // 8, 1))
        xc = xc + cw_ref[3 - k:4 - k, :] * jnp.where(row < k, old, cur)
    a, u = _rglru_gates(xc, wa_ref, ba_ref, wx_ref, bx_ref, lam_ref)

    s = 1
    while s < tm:
        a_sh = jnp.where(row >= s, pltpu.roll(a, s, axis=0), 1.0)
        u_sh = jnp.where(row >= s, pltpu.roll(u, s, axis=0), 0.0)
        u = a * u_sh + u
        a = a * a_sh
        s *= 2
    h = a * hcar[0:1, :] + u
    o_ref[0] = (h * _gelu(gr_ref[0])).astype(o_ref.dtype)
    hcar[...] = jnp.broadcast_to(h[tm - 1:tm, :], hcar.shape)
    xcar[...] = x[tm - 8:tm, :]
    hl_ref[0] = hcar[...]
    cl_ref[0] = xcar[...]


def rglru_prompt(xr, gr, rw, tm):
    b, t, c = xr.shape
    tm = min(tm, t)
    const = lambda a: pl.BlockSpec(a.shape, lambda i, j: (0,) * a.ndim)
    ws = (rw["conv_w"], rw["conv_b"], rw["wa"], rw["ba"], rw["wx"], rw["bx"], rw["lam"])
    tile = pl.BlockSpec((1, tm, c), lambda i, j: (i, j, 0))
    last = pl.BlockSpec((1, 8, c), lambda i, j: (i, 0, 0))
    return pl.pallas_call(
        _rglru_prompt_kernel,
        grid=(b, t // tm),
        in_specs=[tile, tile] + [const(a) for a in ws],
        out_specs=[tile, last, last],
        out_shape=[jax.ShapeDtypeStruct((b, t, c), BF16),
                   jax.ShapeDtypeStruct((b, 8, c), F32),
                   jax.ShapeDtypeStruct((b, 8, c), F32)],
        scratch_shapes=[pltpu.VMEM((8, c), F32), pltpu.VMEM((8, c), F32)],
        compiler_params=_cparams("arbitrary", "arbitrary"),
        name="rglru_prompt",
    )(xr, gr, *ws)


def _gmlp_v(v_raw, vg_ref):
    v = _gelu(v_raw)
    ms = jnp.mean(v * v, axis=-1, keepdims=True)
    return v * lax.rsqrt(ms + RMS_EPS) * vg_ref[...]


def _gmlp_prompt_kernel(u_ref, v_ref, vg_ref, ws_ref, bs_ref, o_ref):
    lc = u_ref.shape[1]
    v = _gmlp_v(v_ref[0], vg_ref).astype(BF16)
    tril = (lax.broadcasted_iota(jnp.int32, (lc, lc), 0) >= lax.broadcasted_iota(jnp.int32, (lc, lc), 1))
    lane = lax.broadcasted_iota(jnp.int32, (lc, LANES), 1)
    parts = []
    for j in range(v.shape[1] // LANES):
        vj = v[:, j * LANES:(j + 1) * LANES]
        w0 = jnp.where(tril, ws_ref[2 * j], 0.0).astype(BF16)
        w1 = jnp.where(tril, ws_ref[2 * j + 1], 0.0).astype(BF16)
        parts.append(jnp.where(lane < HEAD_DIM, _dot(w0, vj), _dot(w1, vj)))
    mixed = jnp.concatenate(parts, axis=1) + bs_ref[...]
    o_ref[0] = (_gelu(u_ref[0]) * mixed).astype(o_ref.dtype)


def gmlp_prompt(u, v, v_gain, ws, bs_exp):
    b, t, c = u.shape
    lc = ws.shape[1]
    tile = pl.BlockSpec((1, lc, c), lambda i, j: (i, j, 0))
    const = lambda a: pl.BlockSpec(a.shape, lambda i, j: (0,) * a.ndim)
    return pl.pallas_call(
        _gmlp_prompt_kernel,
        grid=(b, t // lc),
        in_specs=[tile, tile, const(v_gain), const(ws), const(bs_exp)],
        out_specs=tile,
        out_shape=jax.ShapeDtypeStruct((b, t, c), BF16),
        compiler_params=_cparams("arbitrary", "arbitrary"),
        name="gmlp_prompt",
    )(u, v, v_gain, ws, bs_exp)


DIL_SPAN = DIL_MAX
DIL_UNROLL = 4


def _dil_prompt_kernel(q_ref, k_ref, v_ref, o_ref, acc_ref, m_ref, l_ref):
    span = q_ref.shape[1]
    base = pl.program_id(2) * span
    scale = HEAD_DIM ** -0.5
    lane = lax.broadcasted_iota(jnp.int32, (QBLK, LANES), 1)
    row = lax.broadcasted_iota(jnp.int32, (QBLK, LANES), 0)
    lo = lane < HEAD_DIM
    band_prev = jnp.where(lane >= row, 0.0, NEG_INF)
    band_diag = jnp.where(lane <= row, 0.0, NEG_INF)
    band_prev = jnp.concatenate([band_prev, band_prev], axis=0)
    band_diag = jnp.concatenate([band_diag, band_diag], axis=0)

    for p, (window, dil) in enumerate(DIL_PATTERNS):
        assert window // dil == QBLK and span % (QBLK * dil) == 0

        def tile(idx, carry, p=p, dil=dil):
            start = (idx // dil) * (QBLK * dil) + idx % dil
            g0 = base + start
            rows = pl.ds(start, QBLK, stride=dil)
            q = (q_ref[0, rows, :] * scale).astype(BF16)
            zero = jnp.zeros((), BF16)
            q2 = jnp.concatenate([jnp.where(lo, q, zero), jnp.where(lo, zero, q)], axis=0)
            has_prev = g0 >= QBLK * dil
            d_rows = pl.ds(g0, QBLK, stride=dil)
            p_rows = pl.ds(jnp.where(has_prev, g0 - QBLK * dil, g0), QBLK, stride=dil)
            s_d = _dot_nt(q2, k_ref[0, d_rows, :].astype(BF16)) + band_diag
            s_p = _dot_nt(q2, k_ref[0, p_rows, :].astype(BF16)) + (band_prev + jnp.where(has_prev, 0.0, NEG_INF))
            m = jnp.max(jnp.maximum(s_d, s_p), axis=-1, keepdims=True)
            e_d = jnp.exp(s_d - m)
            e_p = jnp.exp(s_p - m)
            l = jnp.sum(e_d + e_p, axis=-1, keepdims=True)
            acc = (_dot(e_d.astype(BF16), v_ref[0, d_rows, :].astype(BF16))
                   + _dot(e_p.astype(BF16), v_ref[0, p_rows, :].astype(BF16)))
            acc_ref[p, rows, :] = jnp.where(lo, acc[:QBLK], acc[QBLK:])
            m_ref[p, rows, :] = jnp.where(lo, m[:QBLK], m[QBLK:])
            l_ref[p, rows, :] = jnp.where(lo, l[:QBLK], l[QBLK:])
            return carry

        lax.fori_loop(0, span // QBLK, tile, 0, unroll=DIL_UNROLL)

    def combine(c, carry):
        rows = pl.ds(pl.multiple_of(c * QBLK, QBLK), QBLK)
        ms = [m_ref[p, rows, :] for p in range(len(DIL_PATTERNS))]
        mx = functools.reduce(jnp.maximum, ms)
        num, den = 0.0, 0.0
        for p, mp in enumerate(ms):
            w = jnp.exp(mp - mx)
            num = num + w * acc_ref[p, rows, :]
            den = den + w * l_ref[p, rows, :]
        o_ref[0, rows, :] = (num * (1.0 / den)).astype(o_ref.dtype)
        return carry

    lax.fori_loop(0, span // QBLK, combine, 0)


def dil_prompt_attn(q, kv):
    b, t, w = q.shape
    npair = w // LANES
    span = min(DIL_SPAN, t)
    n_pat = len(DIL_PATTERNS)
    return pl.pallas_call(
        _dil_prompt_kernel,
        grid=(b, npair, t // span),
        in_specs=[pl.BlockSpec((1, span, LANES), lambda i, p, j: (i, j, p)),
                  pl.BlockSpec((1, t, LANES), lambda i, p, j: (i, 0, p)),
                  pl.BlockSpec((1, t, LANES), lambda i, p, j: (i, 0, npair + p))],
        out_specs=pl.BlockSpec((1, span, LANES), lambda i, p, j: (i, j, p)),
        out_shape=jax.ShapeDtypeStruct((b, t, w), BF16),
        scratch_shapes=[pltpu.VMEM((n_pat, span, LANES), F32)] * 3,
        compiler_params=_cparams("arbitrary", "arbitrary", "arbitrary"),
        name="dil_prompt_attn",
    )(q, kv, kv)


NSA_G, NSA_REP = 2, 4
NSA_QW = NSA_G * NSA_REP * HEAD_DIM
QPERM = np.arange(NSA_QW).reshape(NSA_G, NSA_REP, HEAD_DIM).transpose(1, 0, 2).reshape(-1)


def _block_diag(blocks):
    n, a, b = blocks.shape
    out = jnp.zeros((n, a, n, b), blocks.dtype)
    out = out.at[jnp.arange(n), :, jnp.arange(n), :].set(blocks)
    return out.reshape(n * a, n * b)


def prep_layer0(p):
    w_in = p["w_in"]
    d = w_in.shape[0]
    n_gate = NSA_G * NSA_REP * 3
    kv_w = 6 * NSA_G * HEAD_DIM
    c_gl = NSA_QW + kv_w
    c_xr = c_gl + n_gate
    d_rnn = (w_in.shape[1] - c_xr) // 2
    cols = np.concatenate([QPERM, np.arange(NSA_QW, c_gl), np.arange(c_xr, c_xr + 2 * d_rnn),
                           np.arange(c_gl, c_xr)])
    w = jnp.concatenate([w_in[:, cols], jnp.zeros((d, LANES - n_gate), w_in.dtype)], axis=1).astype(BF16)
    n = w.shape[1]
    hg = jnp.ones((n,), F32)
    hg = hg.at[0:NSA_QW].set(jnp.tile(p["q_gain"], NSA_QW // HEAD_DIM))
    hg = hg.at[NSA_QW + 2 * LANES:NSA_QW + 3 * LANES].set(jnp.tile(p["k_gain"][1], 2))
    hg = hg.at[NSA_QW + 4 * LANES:NSA_QW + 5 * LANES].set(jnp.tile(p["k_gain"][2], 2))
    qb = NSA_QW // LANES
    c_rows, c_win, c_x, c_g, c_l = NSA_QW, NSA_QW + 4 * LANES, NSA_QW + 6 * LANES, NSA_QW + 6 * LANES + d_rnn, \
        NSA_QW + 6 * LANES + 2 * d_rnn
    outs = ((0, NSA_QW, BF16), (c_rows, 4 * LANES, F32), (c_rows, 4 * LANES, BF16), (c_win, 2 * LANES, F32),
            (c_win, 2 * LANES, BF16), (c_x, d_rnn, F32), (c_g, d_rnn, F32), (c_l, LANES, F32))
    cw = {}
    for c, nm in enumerate("kv"):
        w1 = p["cmp_w1"][c]
        big = jnp.zeros((CMP_BLOCK, NSA_G, HEAD_DIM, NSA_G, HEAD_DIM), F32)
        for g in range(NSA_G):
            big = big.at[:, g, :, g, :].set(w1)
        cw["w1" + nm] = big.reshape(CMP_BLOCK * LANES, LANES).astype(BF16)
        cw["pos" + nm] = jnp.tile(p["cmp_pos"][c], (1, NSA_G)).reshape(1, CMP_BLOCK * LANES)
        cw["w2" + nm] = _block_diag(jnp.stack([p["cmp_w2"][c]] * NSA_G)).astype(BF16)
    cw["kg0"] = jnp.tile(p["k_gain"][0], 2).reshape(1, LANES)
    rw = dict(conv_w=p["conv_w"], conv_b=p["conv_b"].reshape(1, -1),
              wa=_block_diag(p["wa"]).astype(BF16), ba=p["ba"].reshape(1, -1),
              wx=_block_diag(p["wx"]).astype(BF16), bx=p["bx"].reshape(1, -1), lam=p["lam"].reshape(1, -1))
    w_out = p["w_out"]
    return dict(w=w, hg=hg.reshape(1, n), norm_blocks=tuple(range(qb)) + (qb + 2, qb + 4), outs=outs,
                cw=cw, rw=rw, wo_a=w_out[QPERM].astype(BF16), wo_b=w_out[NSA_QW:].astype(BF16))


def prep_layer1(p):
    w = p["w_in"].astype(BF16)
    n = w.shape[1]
    c_w = p["v_gain"].shape[0]
    dil_w = (n - 2 * c_w) // 3
    hg = jnp.ones((n,), F32)
    hg = hg.at[2 * c_w:2 * c_w + dil_w].set(jnp.tile(p["q_gain"], dil_w // HEAD_DIM))
    hg = hg.at[2 * c_w + dil_w:2 * c_w + 2 * dil_w].set(jnp.tile(p["k_gain"], dil_w // HEAD_DIM))
    b0 = 2 * c_w // LANES
    nbq = dil_w // LANES
    outs = ((0, c_w, F32), (c_w, c_w, F32), (2 * c_w, dil_w, F32), (2 * c_w + dil_w, 2 * dil_w, F32))
    w_out = p["w_out"]
    gw = c_w // p["ws"].shape[0]
    return dict(w=w, hg=hg.reshape(1, n), norm_blocks=tuple(range(b0, b0 + 2 * nbq)), outs=outs,
                v_gain=p["v_gain"].reshape(1, c_w), ws=p["ws"], bs_exp=jnp.repeat(p["bs"].T, gw, axis=1),
                ws_diag=jnp.repeat(p["ws"][:, 0, 0], gw).reshape(1, c_w),
                bs0=jnp.repeat(p["bs"][:, 0], gw).reshape(1, c_w),
                wo_a=w_out[:c_w].astype(BF16), wo_b=w_out[c_w:].astype(BF16))


def layer0_prompt(x, mod, ln_mix, ln_ffn, ffn_w, pp):
    b, t, _ = x.shape
    q_bf, rows, rows_bf, win, win_bf, xr, gr, gl = in_proj(
        x, mod, 0, 1, ln_mix, pp["w"], pp["hg"], pp["norm_blocks"], pp["outs"], tm=512)
    kc, vc = nsa_compress(rows, pp["cw"])
    o_nsa = nsa_prompt_attn(q_bf, kc, vc, rows_bf, win_bf, gl)
    o_rnn, h_last, conv_last = rglru_prompt(xr, gr, pp["rw"], tm=256)
    y = out_ffn(o_nsa, o_rnn, x, mod, ln_ffn, pp["wo_a"], pp["wo_b"], *ffn_w, tm=512)
    nwin = min(NSA_WINDOW, t)
    state = (rows.reshape(b, t, 4, NSA_G, HEAD_DIM), win[:, t - nwin:].reshape(b, nwin, 2, NSA_G, HEAD_DIM),
             h_last[:, 0], conv_last[:, 5:8])
    return y, state


def layer1_prompt(x, mod, ln_mix, ln_ffn, ffn_w, pp):
    b, t, _ = x.shape
    u, v, q, kv = in_proj(x, mod, 0, 1, ln_mix, pp["w"], pp["hg"], pp["norm_blocks"], pp["outs"], tm=512)
    o_c = gmlp_prompt(u, v, pp["v_gain"], pp["ws"], pp["bs_exp"])
    o_d = dil_prompt_attn(q, kv)
    y = out_ffn(o_c, o_d, x, mod, ln_ffn, pp["wo_a"], pp["wo_b"], *ffn_w, tm=512)
    nkv = min(DIL_MAX, t)
    heads = kv.shape[2] // (2 * HEAD_DIM)
    return y, kv[:, t - nkv:].reshape(b, nkv, 2, heads, HEAD_DIM)


def _row_softmax_parts(s_list, mask_list, s_new):
    m = s_new[0]
    for sn in s_new[1:]:
        m = jnp.maximum(m, sn)
    for s, mk in zip(s_list, mask_list):
        sm = s if mk is None else jnp.where(mk, s, NEG_INF)
        m = jnp.maximum(m, jnp.max(sm, axis=-1, keepdims=True))
    es, den = [], 0.0
    for s, mk in zip(s_list, mask_list):
        e = jnp.exp(s - m)
        if mk is not None:
            e = jnp.where(mk, e, 0.0)
        es.append(e)
        den = den + jnp.sum(e, axis=-1, keepdims=True)
    en = [jnp.exp(sn - m) for sn in s_new]
    for e in en:
        den = den + e
    return es, en, 1.0 / den


def _nsa_decode_kernel(pt_ref, cache_ref, q_ref, rn_ref, wn_ref, wnc_ref, gl_ref, sw_ref,
                       w1k_ref, posk_ref, w2k_ref, w1v_ref, posv_ref, w2v_ref, kg_ref, ex_ref,
                       o_ref, wo_ref, buf, xk, xv, sem):
    b = pl.program_id(0)
    nbatch = pl.num_programs(0)
    slot = b % 2
    n_pages = pt_ref.shape[1]
    page = cache_ref.shape[3]
    past = n_pages * page
    nc = past // CMP_BLOCK
    nb = past // SEL_BLOCK
    scale = HEAD_DIM ** -0.5
    rep = q_ref.shape[2] // LANES
    nh = 2 * rep

    nseq = q_ref.shape[0]

    def copies(step, sl):
        return [pltpu.make_async_copy(cache_ref.at[pt_ref[step * nseq + u, j]], buf.at[sl, u, j], sem.at[sl, u, j])
                for u in range(nseq) for j in range(n_pages)]

    @pl.when(b == 0)
    def _():
        for c in copies(0, 0):
            c.start()

    @pl.when(b + 1 < nbatch)
    def _():
        for c in copies(b + 1, 1 - slot):
            c.start()

    for c in copies(b, slot):
        c.wait()

    lane = lax.broadcasted_iota(jnp.int32, (nh, LANES), 1)
    hrow = lax.broadcasted_iota(jnp.int32, (nh, LANES), 0)
    blk_t = lax.broadcasted_iota(jnp.int32, (LANES, LANES), 0)
    wb = sw_ref.shape[3]
    last = lax.broadcasted_iota(jnp.int32, (LANES, wb), 1) == wb - 1

    def padded(c2):
        z = jnp.zeros((LANES - nc // 2, LANES), F32)
        return jnp.concatenate([c2[:nc // 2], z, c2[nc // 2:], z], axis=0)

    def one_sequence(u):
        q = q_ref[u]
        qrows = []
        for g in range(2):
            for r in range(rep):
                qrows.append(q[:, r * LANES:(r + 1) * LANES])
        qm = jnp.concatenate(qrows, axis=0)
        qm = jnp.where((lane // HEAD_DIM) == (hrow // rep), qm, jnp.zeros((), qm.dtype))
        qf = qm.astype(F32)

        for j in range(n_pages):
            xk[u, pl.ds(j * page, page), :] = buf[slot, u, j, 0].T
            xv[u, pl.ds(j * page, page), :] = buf[slot, u, j, 1].T
        ldk = lambda s, n, st: xk[u, pl.ds(s, n, stride=st), :]
        ldv = lambda s, n, st: xv[u, pl.ds(s, n, stride=st), :]
        kc = _head_norm(_compress_rows(ldk, nc // 2, w1k_ref, posk_ref, w2k_ref), kg_ref[...])
        vc = _compress_rows(ldv, nc // 2, w1v_ref, posv_ref, w2v_ref)
        kcp = padded(kc).astype(BF16)
        vcp = padded(vc).astype(BF16)
        s_c = _dot_nt(qm, kcp) * scale
        cl = lax.broadcasted_iota(jnp.int32, (nh, 2 * LANES), 1)
        cmask = (cl % LANES) < nc // 2
        s_c = jnp.where(cmask, s_c, NEG_INF)
        e_c = jnp.where(cmask, jnp.exp(s_c - jnp.max(s_c, axis=-1, keepdims=True)), 0.0)
        p_c = e_c * (1.0 / jnp.maximum(jnp.sum(e_c, axis=-1, keepdims=True), 1e-30))
        o_cmp = _dot(p_c.astype(BF16), vcp)

        pp = p_c[:, :LANES] + p_c[:, LANES:]
        imps = []
        for g in range(2):
            ig = pp[g * rep:g * rep + 1]
            for r in range(1, rep):
                ig = ig + pp[g * rep + r:g * rep + r + 1]
            imps += [ig] * rep
        imp = jnp.concatenate(imps + [jnp.zeros((LANES - nh, LANES), F32)], axis=0)
        imp_t = imp.T
        imp_t = jnp.where((blk_t == 0) | (blk_t == nb), FORCE_SCORE, imp_t)
        imp_t = jnp.where(blk_t <= nb, imp_t, -3.0)
        sel = _select_blocks_t(imp_t, blk_t.astype(F32), min(N_SEL, nb + 1)).T[:nh]
        selm = _dot(sel.astype(BF16), ex_ref[...]) > 0.5

        rn = rn_ref[u]
        kst = jnp.concatenate([buf[slot, u, j, 2] for j in range(n_pages)], axis=1).astype(BF16)
        vst = jnp.concatenate([buf[slot, u, j, 3] for j in range(n_pages)], axis=1).astype(BF16)
        s_sel = _dot(qm, kst) * scale
        s_sel_new = jnp.sum(qf * rn[:, 2 * LANES:3 * LANES], axis=-1, keepdims=True) * scale
        (e_s,), (e_sn,), inv_s = _row_softmax_parts([s_sel], [selm], [s_sel_new])
        o_sel = (_dot_nt(e_s.astype(BF16), vst) + e_sn * rn[:, 3 * LANES:4 * LANES]) * inv_s

        kwt, vwt = sw_ref[u, 0], sw_ref[u, 1]
        wn = wn_ref[u]
        s_w = _dot(qm, kwt.astype(BF16)) * scale
        s_w_new = jnp.sum(qf * wn[:, :LANES], axis=-1, keepdims=True) * scale
        (e_w,), (e_wn,), inv_w = _row_softmax_parts([s_w], [None], [s_w_new])
        o_win = (_dot_nt(e_w.astype(BF16), vwt.astype(BF16)) + e_wn * wn[:, LANES:]) * inv_w

        gate = _sigmoid(gl_ref[u])
        o = gate[:, 0:1] * o_cmp + gate[:, 1:2] * o_sel + gate[:, 2:3] * o_win
        l1 = lax.broadcasted_iota(jnp.int32, (1, LANES), 1)
        o_ref[u] = jnp.concatenate([jnp.where(l1 < HEAD_DIM, o[r:r + 1], o[rep + r:rep + r + 1])
                                    for r in range(rep)], axis=1).astype(o_ref.dtype)
        wo_ref[u, 0] = jnp.where(last, wnc_ref[u, 0], pltpu.roll(kwt, wb - 1, axis=1))
        wo_ref[u, 1] = jnp.where(last, wnc_ref[u, 1], pltpu.roll(vwt, wb - 1, axis=1))

    for u in range(nseq):
        one_sequence(u)


def nsa_decode(page_table, cache, q_bf, rows_new, win_new, gl3, state_win, cw):
    b, n_pages = page_table.shape
    n_pool, page, ncomp, ng, hd = cache.shape
    past = n_pages * page
    nb = past // SEL_BLOCK
    assert past % SEL_BLOCK == 0 and nb < LANES and past // CMP_BLOCK <= 2 * LANES and ng * hd == LANES
    wb = state_win.shape[1]
    assert wb <= NSA_WINDOW
    cache_t = jnp.transpose(cache, (0, 2, 3, 4, 1)).reshape(n_pool, ncomp, LANES, page)
    win_t = jnp.transpose(state_win, (0, 2, 3, 4, 1)).reshape(b, 2, LANES, wb)
    win_col = win_new.reshape(b, 2, LANES, 1)
    expand = (jnp.arange(LANES)[:, None] == (jnp.arange(past) // SEL_BLOCK)[None, :]).astype(BF16)
    ws = (cw["w1k"], cw["posk"], cw["w2k"], cw["w1v"], cw["posv"], cw["w2v"], cw["kg0"], expand)
    ns = NSA_DECODE_SEQS
    assert b % ns == 0
    per_b = lambda a: pl.BlockSpec((ns,) + a.shape[1:], lambda i, pt: (i,) + (0,) * (a.ndim - 1))
    const = lambda a: pl.BlockSpec(a.shape, lambda i, pt: (0,) * a.ndim)
    grid_spec = pltpu.PrefetchScalarGridSpec(
        num_scalar_prefetch=1,
        grid=(b // ns,),
        in_specs=[pl.BlockSpec(memory_space=pl.ANY), per_b(q_bf), per_b(rows_new), per_b(win_new), per_b(win_col),
                  per_b(gl3), per_b(win_t)] + [const(a) for a in ws],
        out_specs=[pl.BlockSpec((ns, 1, q_bf.shape[2]), lambda i, pt: (i, 0, 0)),
                   pl.BlockSpec((ns, 2, LANES, wb), lambda i, pt: (i, 0, 0, 0))],
        scratch_shapes=[pltpu.VMEM((2, ns, n_pages, ncomp, LANES, page), F32), pltpu.VMEM((ns, past, LANES), F32),
                        pltpu.VMEM((ns, past, LANES), F32), pltpu.SemaphoreType.DMA((2, ns, n_pages))],
    )
    o, win_out_t = pl.pallas_call(
        _nsa_decode_kernel,
        grid_spec=grid_spec,
        out_shape=[jax.ShapeDtypeStruct((b, 1, q_bf.shape[2]), BF16),
                   jax.ShapeDtypeStruct(win_t.shape, F32)],
        compiler_params=_cparams("arbitrary"),
        name="nsa_decode",
    )(page_table, cache_t, q_bf, rows_new, win_new, win_col, gl3, win_t, *ws)
    return o, jnp.transpose(win_out_t.reshape(b, 2, ng, hd, wb), (0, 4, 1, 2, 3))


def _rglru_decode_kernel(xr_ref, gr_ref, cs_ref, h0_ref, cw_ref, cb_ref, wa_ref, ba_ref, wx_ref, bx_ref, lam_ref,
                         o_ref, h_ref, cn_ref):
    x = xr_ref[...]
    xc = cb_ref[...] + cw_ref[3:4, :] * x
    for k in range(3):
        xc = xc + cw_ref[k:k + 1, :] * cs_ref[k]
    a, u = _rglru_gates(xc, wa_ref, ba_ref, wx_ref, bx_ref, lam_ref)
    h = a * h0_ref[...] + u
    h_ref[...] = h
    o_ref[...] = (h * _gelu(gr_ref[...])).astype(o_ref.dtype)
    cn_ref[0] = cs_ref[1]
    cn_ref[1] = cs_ref[2]
    cn_ref[2] = x


def rglru_decode(xr, gr, conv_t, h0, rw):
    ws = (rw["conv_w"], rw["conv_b"], rw["wa"], rw["ba"], rw["wx"], rw["bx"], rw["lam"])
    return pl.pallas_call(
        _rglru_decode_kernel,
        out_shape=[jax.ShapeDtypeStruct(xr.shape, BF16), jax.ShapeDtypeStruct(xr.shape, F32),
                   jax.ShapeDtypeStruct(conv_t.shape, F32)],
        compiler_params=pltpu.CompilerParams(vmem_limit_bytes=VMEM_LIMIT),
        name="rglru_decode",
    )(xr, gr, conv_t, h0, *ws)


def _gmlp_decode_kernel(u_ref, v_ref, vg_ref, wd_ref, b0_ref, o_ref, vn_ref):
    v = _gmlp_v(v_ref[...], vg_ref)
    vn_ref[...] = v
    o_ref[...] = (_gelu(u_ref[...]) * (wd_ref[...] * v + b0_ref[...])).astype(o_ref.dtype)


def gmlp_decode(u, v, v_gain, ws_diag, bs0):
    return pl.pallas_call(
        _gmlp_decode_kernel,
        out_shape=[jax.ShapeDtypeStruct(u.shape, BF16), jax.ShapeDtypeStruct(u.shape, F32)],
        compiler_params=pltpu.CompilerParams(vmem_limit_bytes=VMEM_LIMIT),
        name="gmlp_decode",
    )(u, v, v_gain, ws_diag, bs0)


def _dil_decode_kernel(q_ref, kvn_ref, st_ref, o_ref, so_ref):
    nh, hd, wb = st_ref.shape[2], st_ref.shape[3], st_ref.shape[4]
    scale = HEAD_DIM ** -0.5
    n_pat = len(DIL_PATTERNS)
    lane1 = lax.broadcasted_iota(jnp.int32, (1, wb), 1)
    dist = wb - lane1
    cnt = jnp.zeros((1, wb), F32)
    for window, dil in DIL_PATTERNS:
        cnt = cnt + jnp.where((dist <= window) & ((dist & (dil - 1)) == 0), 1.0, 0.0)
    last = lax.broadcasted_iota(jnp.int32, (hd, wb), 1) == wb - 1

    def shifted(x, new_col):
        return jnp.where(last, new_col, pltpu.roll(x, wb - 1, axis=1))

    s_rows, s_new_rows = [], []
    for h in range(nh):
        kh = st_ref[0, 0, h]
        qh = q_ref[0, h] * scale
        kn = kvn_ref[0, 0, h]
        s_rows.append(jnp.sum(kh * qh, axis=0, keepdims=True))
        s_new_rows.append(jnp.sum(kn * qh, axis=0, keepdims=True))
        so_ref[0, 0, h] = shifted(kh, kn)
    s = jnp.concatenate(s_rows, axis=0)
    s_new = jnp.concatenate(s_new_rows, axis=0)
    s = jnp.where(cnt > 0.0, s, NEG_INF)
    m = jnp.maximum(jnp.max(s, axis=-1, keepdims=True), s_new)
    e = cnt * jnp.exp(s - m)
    e_new = n_pat * jnp.exp(s_new - m)
    inv = 1.0 / (jnp.sum(e, axis=-1, keepdims=True) + e_new)
    for h in range(nh):
        vh = st_ref[0, 1, h]
        vn = kvn_ref[0, 1, h]
        acc = jnp.sum(vh * e[h:h + 1, :], axis=1, keepdims=True) + e_new[h:h + 1, :] * vn
        o_ref[0, h] = acc * inv[h:h + 1, :]
        so_ref[0, 1, h] = shifted(vh, vn)


def dil_decode(q, kv_new, state):
    b, wb, _, nh, hd = state.shape
    assert wb == DIL_MAX
    st_t = jnp.transpose(state, (0, 2, 3, 4, 1))
    o, so_t = pl.pallas_call(
        _dil_decode_kernel,
        grid=(b,),
        in_specs=[pl.BlockSpec((1, nh, hd, 1), lambda i: (i, 0, 0, 0)),
                  pl.BlockSpec((1, 2, nh, hd, 1), lambda i: (i, 0, 0, 0, 0)),
                  pl.BlockSpec((1, 2, nh, hd, wb), lambda i: (i, 0, 0, 0, 0))],
        out_specs=[pl.BlockSpec((1, nh, hd, 1), lambda i: (i, 0, 0, 0)),
                   pl.BlockSpec((1, 2, nh, hd, wb), lambda i: (i, 0, 0, 0, 0))],
        out_shape=[jax.ShapeDtypeStruct((b, nh, hd, 1), F32), jax.ShapeDtypeStruct(st_t.shape, F32)],
        compiler_params=_cparams("arbitrary"),
        name="dil_decode",
    )(q[..., None], kv_new[..., None], st_t)
    return o[..., 0], jnp.transpose(so_t, (0, 4, 1, 2, 3))


def layer0_sample(x, mod, ln_mix, ln_ffn, ffn_w, pp, cache, page_table, state_win, state_h, state_conv):
    b, _, d = x.shape
    xs = x.reshape(1, b, d)
    q_bf, rows, _, win, _, xr, gr, gl = in_proj(
        xs, mod, 0, 1, ln_mix, pp["w"], pp["hg"], pp["norm_blocks"], pp["outs"], tm=b)
    n_gate = NSA_G * NSA_REP * 3
    gl3 = gl[0, :, :n_gate].reshape(b, NSA_G * NSA_REP, 3)
    o_nsa, win_out = nsa_decode(page_table, cache, q_bf.reshape(b, 1, -1), rows.reshape(b, 1, -1),
                                win.reshape(b, 1, -1), gl3, state_win, pp["cw"])
    o_rnn, h_new, conv_new = rglru_decode(xr[0], gr[0], state_conv.transpose(1, 0, 2), state_h, pp["rw"])
    y = out_ffn(o_nsa.reshape(1, b, -1), o_rnn[None], xs, mod, ln_ffn, pp["wo_a"], pp["wo_b"], *ffn_w, tm=b)
    state = (rows.reshape(b, 1, 4, NSA_G, HEAD_DIM), win_out, h_new, conv_new.transpose(1, 0, 2))
    return y.reshape(b, 1, d), state


def layer1_sample(x, mod, ln_mix, ln_ffn, ffn_w, pp, state_dil):
    b, _, d = x.shape
    xs = x.reshape(1, b, d)
    u, v, q, kv = in_proj(xs, mod, 0, 1, ln_mix, pp["w"], pp["hg"], pp["norm_blocks"], pp["outs"], tm=b)
    o_c, v_n = gmlp_decode(u[0], v[0], pp["v_gain"], pp["ws_diag"], pp["bs0"])
    heads = state_dil.shape[3]
    o_d, dil_out = dil_decode(q.reshape(b, heads, HEAD_DIM), kv.reshape(b, 2, heads, HEAD_DIM), state_dil)
    y = out_ffn(o_c[None], o_d.reshape(1, b, -1).astype(BF16), xs, mod, ln_ffn, pp["wo_a"], pp["wo_b"], *ffn_w,
                tm=b)
    return y.reshape(b, 1, d), (dil_out, v_n.reshape(b, 1, -1))


def kernel(x_prompt, x_sample, cache_nsa_kv, state_nsa_win, state_rglru_h, state_rglru_conv, state_dil_kv, page_table, c_prompt, c_sample, norm_mix_g, norm_ffn_g, w_ada, b_ada, w_ffn_gate, w_ffn_up, w_ffn_down, w_in_ab, w_out_ab, nsa_q_gain, nsa_k_gain, nsa_cmp_w1, nsa_cmp_w2, nsa_cmp_pos, rg_conv_w, rg_conv_b, rg_wa, rg_ba, rg_wx, rg_bx, rg_lambda, w_in_cd, w_out_cd, gmlp_v_gain, gmlp_ws, gmlp_bs, dil_q_gain, dil_k_gain):
    depth = norm_mix_g.shape[0]
    bp, bs = x_prompt.shape[0], x_sample.shape[0]
    pad = -(bp + bs) % 8
    c_all = jnp.concatenate([c_prompt, c_sample, jnp.zeros((pad, c_prompt.shape[1]), F32)], axis=0)
    mod_all = ada_mod(c_all, w_ada.astype(BF16), b_ada)
    yp, ys = x_prompt, x_sample
    kv_p, kv_s, win_p, win_s, h_p, h_s, conv_p, conv_s, dil_p, dil_s, gv_s = ([] for _ in range(11))
    for layer in range(depth):
        i = layer // 2
        mod_p = mod_all[layer, :bp, None, :]
        mod_s = mod_all[layer, None, bp:bp + bs, :]
        ffn_w = (w_ffn_gate[layer].astype(BF16), w_ffn_up[layer].astype(BF16), w_ffn_down[layer].astype(BF16))
        if layer % 2 == 0:
            pp = prep_layer0(dict(w_in=w_in_ab[i], w_out=w_out_ab[i], q_gain=nsa_q_gain[i], k_gain=nsa_k_gain[i],
                                  cmp_w1=nsa_cmp_w1[i], cmp_w2=nsa_cmp_w2[i], cmp_pos=nsa_cmp_pos[i],
                                  conv_w=rg_conv_w[i], conv_b=rg_conv_b[i], wa=rg_wa[i], ba=rg_ba[i],
                                  wx=rg_wx[i], bx=rg_bx[i], lam=rg_lambda[i]))
            yp, st = layer0_prompt(yp, mod_p, norm_mix_g[layer], norm_ffn_g[layer], ffn_w, pp)
            kv_p.append(st[0]); win_p.append(st[1]); h_p.append(st[2]); conv_p.append(st[3])
            ys, st = layer0_sample(ys, mod_s, norm_mix_g[layer], norm_ffn_g[layer], ffn_w, pp, cache_nsa_kv[i],
                                   page_table, state_nsa_win[i], state_rglru_h[i], state_rglru_conv[i])
            kv_s.append(st[0]); win_s.append(st[1]); h_s.append(st[2]); conv_s.append(st[3])
        else:
            pp = prep_layer1(dict(w_in=w_in_cd[i], w_out=w_out_cd[i], v_gain=gmlp_v_gain[i], ws=gmlp_ws[i],
                                  bs=gmlp_bs[i], q_gain=dil_q_gain[i], k_gain=dil_k_gain[i]))
            yp, st = layer1_prompt(yp, mod_p, norm_mix_g[layer], norm_ffn_g[layer], ffn_w, pp)
            dil_p.append(st)
            ys, st = layer1_sample(ys, mod_s, norm_mix_g[layer], norm_ffn_g[layer], ffn_w, pp, state_dil_kv[i])
            dil_s.append(st[0]); gv_s.append(st[1])
    return (yp, ys, jnp.stack(kv_p), jnp.stack(kv_s), jnp.stack(win_p), jnp.stack(win_s),
            jnp.stack(h_p), jnp.stack(h_s), jnp.stack(conv_p), jnp.stack(conv_s),
            jnp.stack(dil_p), jnp.stack(dil_s), jnp.stack(gv_s))
```

```python
import functools

import numpy as np
import jax
import jax.numpy as jnp
from jax import lax
from jax.experimental import pallas as pl
from jax.experimental.pallas import tpu as pltpu

F32 = jnp.float32
BF16 = jnp.bfloat16

LANES = 128
HEAD_DIM = 64
QBLK = 128
CMP_BLOCK = 32
SEL_BLOCK = 64
N_SEL = 16
NSA_WINDOW = 512
FORCE_SCORE = 1.0e4
DIL_PATTERNS = ((128, 1), (512, 4), (2048, 16))
DIL_MAX = 2048
RG_C = 8.0
RMS_EPS = 1e-6
NEG_INF = -1e30
LOG2_E = 1.4426950408889634
VMEM_LIMIT = 56 * 1024 * 1024


def _cparams(*sem):
    return pltpu.CompilerParams(dimension_semantics=sem, vmem_limit_bytes=VMEM_LIMIT)


def _dot(a, b):
    return jnp.dot(a, b, preferred_element_type=F32)


def _dot_nt(a, b):
    return lax.dot_general(a, b, (((1,), (1,)), ((), ())), preferred_element_type=F32)


def _gelu(x):
    return 0.5 * x * (1.0 + jnp.tanh(np.sqrt(2.0 / np.pi) * (x + 0.044715 * (x * x * x))))


def _sigmoid(x):
    return 1.0 / (1.0 + jnp.exp(-x))


def _head_norm(z, gain):
    lo = lax.broadcasted_iota(jnp.int32, z.shape, 1) < HEAD_DIM
    z2 = z * z
    s_lo = jnp.sum(jnp.where(lo, z2, 0.0), axis=-1, keepdims=True)
    s_hi = jnp.sum(jnp.where(lo, 0.0, z2), axis=-1, keepdims=True)
    inv = lax.rsqrt(jnp.where(lo, s_lo, s_hi) * (1.0 / HEAD_DIM) + RMS_EPS)
    return z * inv * gain


def _ada_kernel(c_ref, w_ref, b_ref, o_ref):
    c = c_ref[...]
    s = c * _sigmoid(c)
    o_ref[0] = _dot(s.astype(BF16), w_ref[0]) + b_ref[0]


def ada_mod(c_all, w_ada, b_ada):
    m, d = c_all.shape
    nl, _, n = w_ada.shape
    tn = 1536
    return pl.pallas_call(
        _ada_kernel,
        grid=(nl, n // tn),
        in_specs=[pl.BlockSpec((m, d), lambda l, j: (0, 0)),
                  pl.BlockSpec((1, d, tn), lambda l, j: (l, 0, j)),
                  pl.BlockSpec((1, 1, tn), lambda l, j: (l, 0, j))],
        out_specs=pl.BlockSpec((1, m, tn), lambda l, j: (l, 0, j)),
        out_shape=jax.ShapeDtypeStruct((nl, m, n), F32),
        compiler_params=_cparams("arbitrary", "arbitrary"),
        name="ada_mod",
    )(c_all, w_ada, b_ada.reshape(nl, 1, n))


def _in_proj_kernel(x_ref, sc_ref, sh_ref, g_ref, w_ref, hg_ref, *out_refs, norm_blocks, outs):
    x = x_ref[0]
    ms = jnp.mean(x * x, axis=-1, keepdims=True)
    h = x * lax.rsqrt(ms + RMS_EPS) * g_ref[...]
    h = h * (1.0 + sc_ref[0]) + sh_ref[0]
    z = _dot(h.astype(BF16), w_ref[...])
    nblk = z.shape[1] // LANES
    blocks = []
    for j in range(nblk):
        zb = z[:, j * LANES:(j + 1) * LANES]
        if j in norm_blocks:
            zb = _head_norm(zb, hg_ref[:, j * LANES:(j + 1) * LANES])
        blocks.append(zb)
    for o_ref, (c0, width, _) in zip(out_refs, outs):
        for j in range(width // LANES):
            o_ref[0, :, j * LANES:(j + 1) * LANES] = blocks[c0 // LANES + j].astype(o_ref.dtype)


def in_proj(x, mod, sh_idx, sc_idx, g, w, head_gain, norm_blocks, outs, tm):
    bm, t, d = x.shape
    r = mod.shape[1]
    n = w.shape[1]
    tm = min(tm, t)
    if r == 1:
        mod_spec = lambda k: pl.BlockSpec((1, 1, d), lambda b, i: (b, 0, k))
    else:
        assert r == t and tm == t
        mod_spec = lambda k: pl.BlockSpec((1, tm, d), lambda b, i: (b, 0, k))
    kern = functools.partial(_in_proj_kernel, norm_blocks=tuple(norm_blocks), outs=tuple(outs))
    return pl.pallas_call(
        kern,
        grid=(bm, t // tm),
        in_specs=[pl.BlockSpec((1, tm, d), lambda b, i: (b, i, 0)),
                  mod_spec(sc_idx), mod_spec(sh_idx),
                  pl.BlockSpec((1, d), lambda b, i: (0, 0)),
                  pl.BlockSpec((d, n), lambda b, i: (0, 0)),
                  pl.BlockSpec((1, n), lambda b, i: (0, 0))],
        out_specs=[pl.BlockSpec((1, tm, wd), lambda b, i: (b, i, 0)) for (_, wd, _) in outs],
        out_shape=[jax.ShapeDtypeStruct((bm, t, wd), dt) for (_, wd, dt) in outs],
        compiler_params=_cparams("arbitrary", "arbitrary"),
        name="in_proj",
    )(x, mod, mod, g.reshape(1, d), w, head_gain)


def _out_ffn_kernel(ma_ref, mb_ref, x_ref, gm_ref, shf_ref, scf_ref, gf_ref, lnf_ref,
                    woa_ref, wob_ref, wg_ref, wu_ref, wd_ref, o_ref, *, hidden_chunk):
    mix = _dot(ma_ref[0], woa_ref[...]) + _dot(mb_ref[0], wob_ref[...])
    x1 = x_ref[0] + gm_ref[0] * mix
    ms = jnp.mean(x1 * x1, axis=-1, keepdims=True)
    hf = x1 * lax.rsqrt(ms + RMS_EPS) * lnf_ref[...]
    hf = (hf * (1.0 + scf_ref[0]) + shf_ref[0]).astype(BF16)
    hidden = wg_ref.shape[1]
    ffn = jnp.zeros(x1.shape, F32)
    for c0 in range(0, hidden, hidden_chunk):
        gt = _dot(hf, wg_ref[:, c0:c0 + hidden_chunk])
        up = _dot(hf, wu_ref[:, c0:c0 + hidden_chunk])
        act = (gt * _sigmoid(gt) * up).astype(BF16)
        ffn = ffn + _dot(act, wd_ref[c0:c0 + hidden_chunk, :])
    o_ref[0] = x1 + gf_ref[0] * ffn


def out_ffn(mix_a, mix_b, x, mod, ln_ffn, wo_a, wo_b, wg, wu, wd, tm):
    bm, t, d = x.shape
    r = mod.shape[1]
    tm = min(tm, t)
    ka, kb = mix_a.shape[2], mix_b.shape[2]
    hidden = wg.shape[1]
    if r == 1:
        mod_spec = lambda k: pl.BlockSpec((1, 1, d), lambda b, i: (b, 0, k))
    else:
        assert r == t and tm == t
        mod_spec = lambda k: pl.BlockSpec((1, tm, d), lambda b, i: (b, 0, k))
    const = lambda shape: pl.BlockSpec(shape, lambda b, i: (0,) * len(shape), pipeline_mode=pl.Buffered(1))
    kern = functools.partial(_out_ffn_kernel, hidden_chunk=hidden // 2)
    return pl.pallas_call(
        kern,
        grid=(bm, t // tm),
        in_specs=[pl.BlockSpec((1, tm, ka), lambda b, i: (b, i, 0)),
                  pl.BlockSpec((1, tm, kb), lambda b, i: (b, i, 0)),
                  pl.BlockSpec((1, tm, d), lambda b, i: (b, i, 0)),
                  mod_spec(2), mod_spec(3), mod_spec(4), mod_spec(5),
                  const((1, d)), const((ka, d)), const((kb, d)),
                  const((d, hidden)), const((d, hidden)), const((hidden, d))],
        out_specs=pl.BlockSpec((1, tm, d), lambda b, i: (b, i, 0)),
        out_shape=jax.ShapeDtypeStruct((bm, t, d), F32),
        compiler_params=_cparams("arbitrary", "arbitrary"),
        name="out_ffn",
    )(mix_a, mix_b, x, mod, mod, mod, mod, ln_ffn.reshape(1, d), wo_a, wo_b, wg, wu, wd)


def _compress_rows(load, nh, w1_ref, pos_ref, w2_ref):
    halves = []
    for parity in range(2):
        halves.append(jnp.concatenate(
            [load(parity * CMP_BLOCK + l, nh, 2 * CMP_BLOCK) for l in range(CMP_BLOCK)], axis=1))
    xs = (jnp.concatenate(halves, axis=0) + pos_ref[...]).astype(BF16)
    hid = _gelu(_dot(xs, w1_ref[...]))
    return _dot(hid.astype(BF16), w2_ref[...])


def _nsa_compress_kernel(rk_ref, rv_ref, w1k_ref, posk_ref, w2k_ref, w1v_ref, posv_ref, w2v_ref, kg_ref,
                         kc_ref, vc_ref):
    nh = kc_ref.shape[1] // 2
    ldk = lambda s, n, st: rk_ref[0, pl.ds(s, n, stride=st), :]
    ldv = lambda s, n, st: rv_ref[0, pl.ds(s, n, stride=st), :]
    kc = _compress_rows(ldk, nh, w1k_ref, posk_ref, w2k_ref)
    kc_ref[0] = _head_norm(kc, kg_ref[...]).astype(kc_ref.dtype)
    vc_ref[0] = _compress_rows(ldv, nh, w1v_ref, posv_ref, w2v_ref).astype(vc_ref.dtype)


def nsa_compress(rows, cw):
    b, t, _ = rows.shape
    nc = t // CMP_BLOCK
    const = lambda a: pl.BlockSpec(a.shape, lambda i: (0,) * a.ndim)
    ws = (cw["w1k"], cw["posk"], cw["w2k"], cw["w1v"], cw["posv"], cw["w2v"], cw["kg0"])
    return pl.pallas_call(
        _nsa_compress_kernel,
        grid=(b,),
        in_specs=[pl.BlockSpec((1, t, LANES), lambda i: (i, 0, 0)),
                  pl.BlockSpec((1, t, LANES), lambda i: (i, 0, 1))] + [const(a) for a in ws],
        out_specs=[pl.BlockSpec((1, nc, LANES), lambda i: (i, 0, 0))] * 2,
        out_shape=[jax.ShapeDtypeStruct((b, nc, LANES), BF16)] * 2,
        compiler_params=_cparams("arbitrary"),
        name="nsa_compress",
    )(rows, rows, *ws)


def _select_blocks(imp, blk_f, n_pick):
    sel = jnp.zeros(imp.shape, F32)
    work = imp
    for _ in range(n_pick):
        m = jnp.max(work, axis=-1, keepdims=True)
        idx = jnp.min(jnp.where(work == m, blk_f, 1e9), axis=-1, keepdims=True)
        pick = blk_f == idx
        sel = jnp.where(pick, 1.0, sel)
        work = jnp.where(pick, -2.0, work)
    return sel


def _select_blocks_t(imp, blk_f, n_pick):
    sel = jnp.zeros(imp.shape, F32)
    work = imp
    for _ in range(n_pick):
        m = jnp.max(work, axis=0, keepdims=True)
        idx = jnp.min(jnp.where(work == m, blk_f, 1e9), axis=0, keepdims=True)
        pick = blk_f == idx
        sel = jnp.where(pick, 1.0, sel)
        work = jnp.where(pick, -2.0, work)
    return sel


NSA_SEL_CHUNK = 8
NSA_SUB = 256
NSA_DECODE_SEQS = 2

def _dot_tn(a, b):
    return lax.dot_general(a, b, (((0,), (0,)), ((), ())), preferred_element_type=F32)


def _nsa_prompt_kernel(q_ref, kc_ref, vc_ref, ks_ref, vs_ref, kw_ref, vw_ref, gl_ref, e_ref, o_ref,
                       s_ref, mx_ref, m_ref, l_ref, acc_ref):
    qi = pl.program_id(1)
    t0 = qi * QBLK
    nc = kc_ref.shape[1]
    nb = nc // 2
    scale = HEAD_DIM ** -0.5
    rep = q_ref.shape[2] // LANES
    nhead = 2 * rep
    ncol = nhead * QBLK

    lane = lax.broadcasted_iota(jnp.int32, (QBLK, LANES), 1)
    sub = lax.broadcasted_iota(jnp.int32, (QBLK, LANES), 0)
    tile8 = lambda x: jnp.concatenate([x] * nhead, axis=1)
    q = q_ref[0]
    qs = []
    for g in range(2):
        gm = (lane >= g * HEAD_DIM) & (lane < (g + 1) * HEAD_DIM)
        for r in range(rep):
            qs.append(jnp.where(gm, q[:, r * LANES:(r + 1) * LANES], jnp.zeros((), q.dtype)))
    qall = (jnp.concatenate(qs, axis=0).astype(F32) * (scale * LOG2_E)).astype(BF16)

    crow = lax.broadcasted_iota(jnp.int32, (nc, QBLK), 0)
    cidx = jnp.where(crow < nb, 2 * crow, 2 * (crow - nb) + 1)
    cvalid = ((cidx + 1) * CMP_BLOCK - 1) <= t0 + lax.broadcasted_iota(jnp.int32, (nc, QBLK), 1)
    cbias = tile8(jnp.where(cvalid, 0.0, NEG_INF))
    cone = tile8(jnp.where(cvalid, 1.0, 0.0))
    sc = _dot_nt(kc_ref[0], qall) + cbias
    ec = jnp.exp2(sc - jnp.max(sc, axis=0, keepdims=True)) * cone
    p = ec * (1.0 / jnp.maximum(jnp.sum(ec, axis=0, keepdims=True), 1e-30))
    o_cmp = _dot_tn(vc_ref[0], p.astype(BF16))

    brow = lax.broadcasted_iota(jnp.int32, (nb, 2 * QBLK), 0)
    tb = t0 + lax.broadcasted_iota(jnp.int32, (nb, 2 * QBLK), 1) % QBLK
    imps = []
    for g in range(2):
        ps = p[:, (g * rep) * QBLK:(g * rep + 1) * QBLK]
        for r in range(1, rep):
            ps = ps + p[:, (g * rep + r) * QBLK:(g * rep + r + 1) * QBLK]
        imps.append(ps[:nb] + ps[nb:])
    imp = jnp.concatenate(imps, axis=1)
    imp = jnp.where((brow == 0) | (brow == tb // SEL_BLOCK), FORCE_SCORE, imp)
    imp = jnp.where(brow * SEL_BLOCK <= tb, imp, -1.0)
    sel = _select_blocks_t(imp, brow.astype(F32), min(N_SEL, nb))
    bias_t = jnp.where(sel > 0.5, 0.0, NEG_INF)
    if nb < LANES:
        bias_t = jnp.concatenate([bias_t, jnp.zeros((LANES - nb, 2 * QBLK), F32)], axis=0)
    qbias = []
    for g in range(2):
        qbias += [bias_t[:, g * QBLK:(g + 1) * QBLK].T.astype(BF16)] * rep
    qaug = jnp.concatenate([qall, jnp.concatenate(qbias, axis=0)], axis=1)

    def softmax_pv(s, v, carry):
        m_i, l_i, acc = carry
        m_new = jnp.maximum(m_i, jnp.max(s, axis=0, keepdims=True))
        alpha = jnp.exp2(m_i - m_new)
        e = jnp.exp2(s - m_new)
        l_new = alpha * l_i + jnp.sum(e, axis=0, keepdims=True)
        return m_new, l_new, alpha * acc + _dot_tn(v, e.astype(BF16))

    init = (jnp.full((1, ncol), NEG_INF, F32), jnp.zeros((1, ncol), F32), jnp.zeros((LANES, ncol), F32))
    chunk = NSA_SEL_CHUNK * QBLK
    nsub = chunk // NSA_SUB
    sub_rows = lambda kc, j: pl.ds(pl.multiple_of(kc * chunk + j * NSA_SUB, NSA_SUB), NSA_SUB)
    neg_row = jnp.full((1, ncol), NEG_INF, F32)

    def scores(kc, j):
        r = sub_rows(kc, j)
        return _dot_nt(jnp.concatenate([ks_ref[0, r, :], e_ref[r, :]], axis=1), qaug)

    def rescale(mx):
        m_old = m_ref[...]
        m_new = jnp.maximum(m_old, mx)
        alpha = jnp.exp2(m_old - m_new)
        m_ref[...] = m_new
        l_ref[...] = alpha * l_ref[...]
        acc_ref[...] = alpha * acc_ref[...]
        return m_new

    def consume(s, v, m_new):
        e = jnp.exp2(s - m_new)
        l_ref[...] += jnp.sum(e, axis=0, keepdims=True)
        acc_ref[...] += _dot_tn(v, e.astype(BF16))

    def stage(kc, cur, nxt):
        m_new = rescale(mx_ref[cur])
        mx = neg_row
        for j in range(nsub):
            s_next = scores(kc + 1, j)
            s_ref[nxt, j * NSA_SUB:(j + 1) * NSA_SUB, :] = s_next
            mx = jnp.maximum(mx, jnp.max(s_next, axis=0, keepdims=True))
            consume(s_ref[cur, j * NSA_SUB:(j + 1) * NSA_SUB, :], vs_ref[0, sub_rows(kc, j), :], m_new)
        mx_ref[nxt] = mx

    m_ref[...] = neg_row
    l_ref[...] = jnp.zeros((1, ncol), F32)
    acc_ref[...] = jnp.zeros((LANES, ncol), F32)
    mx = neg_row
    for j in range(nsub):
        s0 = scores(0, j)
        s_ref[0, j * NSA_SUB:(j + 1) * NSA_SUB, :] = s0
        mx = jnp.maximum(mx, jnp.max(s0, axis=0, keepdims=True))
    mx_ref[0] = mx

    last = qi // NSA_SEL_CHUNK

    def stage_pair(pair, carry):
        stage(2 * pair, 0, 1)
        stage(2 * pair + 1, 1, 0)
        return carry

    lax.fori_loop(0, last // 2, stage_pair, 0)

    @pl.when(last % 2 == 1)
    def _():
        stage(last - 1, 0, 1)

    slot = last % 2
    kpos = last * chunk + lax.broadcasted_iota(jnp.int32, (chunk, QBLK), 0)
    causal = jnp.where(kpos <= t0 + lax.broadcasted_iota(jnp.int32, (chunk, QBLK), 1), 0.0, NEG_INF)
    s_last = [s_ref[slot, j * NSA_SUB:(j + 1) * NSA_SUB, :] + tile8(causal[j * NSA_SUB:(j + 1) * NSA_SUB])
              for j in range(nsub)]
    mx = neg_row
    for s in s_last:
        mx = jnp.maximum(mx, jnp.max(s, axis=0, keepdims=True))
    m_new = rescale(mx)
    q_end = (qi % NSA_SEL_CHUNK + 1) * QBLK
    consume(s_last[0], vs_ref[0, sub_rows(last, 0), :], m_new)
    for j in range(1, nsub):
        @pl.when(j * NSA_SUB < q_end)
        def _(j=j):
            consume(s_last[j], vs_ref[0, sub_rows(last, j), :], m_new)
    o_sel = acc_ref[...] * (1.0 / l_ref[...])

    wkeys = NSA_WINDOW + QBLK
    wstart = jnp.maximum(qi - NSA_WINDOW // QBLK, 0) * QBLK
    dist = (t0 + lax.broadcasted_iota(jnp.int32, (wkeys, QBLK), 1)) \
        - (wstart + lax.broadcasted_iota(jnp.int32, (wkeys, QBLK), 0))
    wbias = tile8(jnp.where((dist >= 0) & (dist <= NSA_WINDOW), 0.0, NEG_INF))
    wrows = pl.ds(pl.multiple_of(wstart, QBLK), wkeys)
    _, l_w, acc_w = softmax_pv(_dot_nt(kw_ref[0, wrows, :], qall) + wbias, vw_ref[0, wrows, :], init)
    o_win = acc_w * (1.0 / l_w)

    gate = _sigmoid(gl_ref[0]).T
    for r in range(rep):
        og = []
        for g in range(2):
            c = (g * rep + r) * 3
            cols = slice((g * rep + r) * QBLK, (g * rep + r + 1) * QBLK)
            og.append(gate[c:c + 1] * o_cmp[:, cols] + gate[c + 1:c + 2] * o_sel[:, cols]
                      + gate[c + 2:c + 3] * o_win[:, cols])
        o_ref[0, :, r * LANES:(r + 1) * LANES] = jnp.where(sub < HEAD_DIM, og[0], og[1]).T.astype(o_ref.dtype)


def nsa_prompt_attn(q_bf, kc, vc, rows_bf, win_bf, gl):
    b, t, qw = q_bf.shape
    nc = kc.shape[1]
    nq = t // QBLK
    assert t % (NSA_SEL_CHUNK * QBLK) == 0 and t >= NSA_WINDOW + QBLK and t // SEL_BLOCK <= LANES
    ncol = 2 * (qw // LANES) * QBLK
    full = lambda k: pl.BlockSpec((1, t, LANES), lambda i, j: (i, 0, k))
    member = (jnp.arange(t)[:, None] // SEL_BLOCK == jnp.arange(LANES)[None, :]).astype(BF16)
    return pl.pallas_call(
        _nsa_prompt_kernel,
        grid=(b, nq),
        in_specs=[pl.BlockSpec((1, QBLK, qw), lambda i, j: (i, j, 0)),
                  pl.BlockSpec((1, nc, LANES), lambda i, j: (i, 0, 0)),
                  pl.BlockSpec((1, nc, LANES), lambda i, j: (i, 0, 0)),
                  full(2), full(3), full(0), full(1),
                  pl.BlockSpec((1, QBLK, LANES), lambda i, j: (i, j, 0)),
                  pl.BlockSpec((t, LANES), lambda i, j: (0, 0))],
        out_specs=pl.BlockSpec((1, QBLK, qw), lambda i, j: (i, j, 0)),
        out_shape=jax.ShapeDtypeStruct((b, t, qw), BF16),
        scratch_shapes=[pltpu.VMEM((2, NSA_SEL_CHUNK * QBLK, ncol), F32),
                        pltpu.VMEM((2, 1, ncol), F32),
                        pltpu.VMEM((1, ncol), F32), pltpu.VMEM((1, ncol), F32),
                        pltpu.VMEM((LANES, ncol), F32)],
        compiler_params=_cparams("arbitrary", "arbitrary"),
        name="nsa_prompt_attn",
    )(q_bf, kc, vc, rows_bf, rows_bf, win_bf, win_bf, gl, member)


def _rglru_gates(xc, wa_ref, ba_ref, wx_ref, bx_ref, lam_ref):
    xb = xc.astype(BF16)
    r = _sigmoid(_dot(xb, wa_ref[...]) + ba_ref[...])
    i = _sigmoid(_dot(xb, wx_ref[...]) + bx_ref[...])
    nl = -lam_ref[...]
    softplus = jnp.maximum(nl, 0.0) + jnp.log1p(jnp.exp(-jnp.abs(nl)))
    log_a = -RG_C * r * softplus
    a = jnp.exp(log_a)
    u = jnp.sqrt(-jnp.tanh(log_a) * (a * a + 1.0)) * (i * xc)
    return a, u


def _rglru_prompt_kernel(xr_ref, gr_ref, cw_ref, cb_ref, wa_ref, ba_ref, wx_ref, bx_ref, lam_ref,
                         o_ref, hl_ref, cl_ref, hcar, xcar):
    i = pl.program_id(1)
    tm, c = xr_ref.shape[1], xr_ref.shape[2]

    @pl.when(i == 0)
    def _():
        hcar[...] = jnp.zeros(hcar.shape, F32)
        xcar[...] = jnp.zeros(xcar.shape, F32)

    x = xr_ref[0]
    prev = xcar[...]
    row = lax.broadcasted_iota(jnp.int32, (tm, c), 0)
    xc = cb_ref[...] + cw_ref[3:4, :] * x
    for k in range(1, 4):
        cur = pltpu.roll(x, k, axis=0)
        old = jnp.tile(pltpu.roll(prev, k, axis=0), (tm // 8, 1))
        xc = xc + cw_ref[3 - k:4 - k, :] * jnp.where(row < k, old, cur)
    a, u = _rglru_gates(xc, wa_ref, ba_ref, wx_ref, bx_ref, lam_ref)

    s = 1
    while s < tm:
        a_sh = jnp.where(row >= s, pltpu.roll(a, s, axis=0), 1.0)
        u_sh = jnp.where(row >= s, pltpu.roll(u, s, axis=0), 0.0)
        u = a * u_sh + u
        a = a * a_sh
        s *= 2
    h = a * hcar[0:1, :] + u
    o_ref[0] = (h * _gelu(gr_ref[0])).astype(o_ref.dtype)
    hcar[...] = jnp.broadcast_to(h[tm - 1:tm, :], hcar.shape)
    xcar[...] = x[tm - 8:tm, :]
    hl_ref[0] = hcar[...]
    cl_ref[0] = xcar[...]


def rglru_prompt(xr, gr, rw, tm):
    b, t, c = xr.shape
    tm = min(tm, t)
    const = lambda a: pl.BlockSpec(a.shape, lambda i, j: (0,) * a.ndim)
    ws = (rw["conv_w"], rw["conv_b"], rw["wa"], rw["ba"], rw["wx"], rw["bx"], rw["lam"])
    tile = pl.BlockSpec((1, tm, c), lambda i, j: (i, j, 0))
    last = pl.BlockSpec((1, 8, c), lambda i, j: (i, 0, 0))
    return pl.pallas_call(
        _rglru_prompt_kernel,
        grid=(b, t // tm),
        in_specs=[tile, tile] + [const(a) for a in ws],
        out_specs=[tile, last, last],
        out_shape=[jax.ShapeDtypeStruct((b, t, c), BF16),
                   jax.ShapeDtypeStruct((b, 8, c), F32),
                   jax.ShapeDtypeStruct((b, 8, c), F32)],
        scratch_shapes=[pltpu.VMEM((8, c), F32), pltpu.VMEM((8, c), F32)],
        compiler_params=_cparams("arbitrary", "arbitrary"),
        name="rglru_prompt",
    )(xr, gr, *ws)


def _gmlp_v(v_raw, vg_ref):
    v = _gelu(v_raw)
    ms = jnp.mean(v * v, axis=-1, keepdims=True)
    return v * lax.rsqrt(ms + RMS_EPS) * vg_ref[...]


GMLP_CHUNKS = 4


def _gmlp_prompt_kernel(u_ref, v_ref, vg_ref, ws_ref, bs_ref, o_ref):
    lc = ws_ref.shape[1]
    tril = (lax.broadcasted_iota(jnp.int32, (lc, lc), 0) >= lax.broadcasted_iota(jnp.int32, (lc, lc), 1))
    lane = lax.broadcasted_iota(jnp.int32, (lc, LANES), 1)
    wts = [jnp.where(tril, ws_ref[g], 0.0).astype(BF16) for g in range(ws_ref.shape[0])]
    for ci in range(u_ref.shape[1] // lc):
        rows = slice(ci * lc, (ci + 1) * lc)
        v = _gmlp_v(v_ref[0, rows, :], vg_ref).astype(BF16)
        parts = []
        for j in range(v.shape[1] // LANES):
            vj = v[:, j * LANES:(j + 1) * LANES]
            parts.append(jnp.where(lane < HEAD_DIM, _dot(wts[2 * j], vj), _dot(wts[2 * j + 1], vj)))
        mixed = jnp.concatenate(parts, axis=1) + bs_ref[...]
        o_ref[0, rows, :] = (_gelu(u_ref[0, rows, :]) * mixed).astype(o_ref.dtype)


def gmlp_prompt(u, v, v_gain, ws, bs_exp):
    b, t, c = u.shape
    lc = ws.shape[1]
    rows = min(GMLP_CHUNKS * lc, t)
    tile = pl.BlockSpec((1, rows, c), lambda i, j: (i, j, 0))
    const = lambda a: pl.BlockSpec(a.shape, lambda i, j: (0,) * a.ndim)
    return pl.pallas_call(
        _gmlp_prompt_kernel,
        grid=(b, t // rows),
        in_specs=[tile, tile, const(v_gain), const(ws), const(bs_exp)],
        out_specs=tile,
        out_shape=jax.ShapeDtypeStruct((b, t, c), BF16),
        compiler_params=_cparams("arbitrary", "arbitrary"),
        name="gmlp_prompt",
    )(u, v, v_gain, ws, bs_exp)


DIL_SPAN = DIL_MAX
DIL_UNROLL = 4


def _dil_prompt_kernel(q_ref, k_ref, v_ref, o_ref, acc_ref, m_ref, l_ref):
    span = q_ref.shape[1]
    base = pl.program_id(2) * span
    scale = HEAD_DIM ** -0.5
    lane = lax.broadcasted_iota(jnp.int32, (QBLK, LANES), 1)
    row = lax.broadcasted_iota(jnp.int32, (QBLK, LANES), 0)
    lo = lane < HEAD_DIM
    band_prev = jnp.where(lane >= row, 0.0, NEG_INF)
    band_diag = jnp.where(lane <= row, 0.0, NEG_INF)
    band_prev = jnp.concatenate([band_prev, band_prev], axis=0)
    band_diag = jnp.concatenate([band_diag, band_diag], axis=0)

    for p, (window, dil) in enumerate(DIL_PATTERNS):
        assert window // dil == QBLK and span % (QBLK * dil) == 0

        def tile(idx, carry, p=p, dil=dil):
            start = (idx // dil) * (QBLK * dil) + idx % dil
            g0 = base + start
            rows = pl.ds(start, QBLK, stride=dil)
            q = (q_ref[0, rows, :] * scale).astype(BF16)
            zero = jnp.zeros((), BF16)
            q2 = jnp.concatenate([jnp.where(lo, q, zero), jnp.where(lo, zero, q)], axis=0)
            has_prev = g0 >= QBLK * dil
            d_rows = pl.ds(g0, QBLK, stride=dil)
            p_rows = pl.ds(jnp.where(has_prev, g0 - QBLK * dil, g0), QBLK, stride=dil)
            s_d = _dot_nt(q2, k_ref[0, d_rows, :].astype(BF16)) + band_diag
            s_p = _dot_nt(q2, k_ref[0, p_rows, :].astype(BF16)) + (band_prev + jnp.where(has_prev, 0.0, NEG_INF))
            m = jnp.max(jnp.maximum(s_d, s_p), axis=-1, keepdims=True)
            e_d = jnp.exp(s_d - m)
            e_p = jnp.exp(s_p - m)
            l = jnp.sum(e_d + e_p, axis=-1, keepdims=True)
            acc = (_dot(e_d.astype(BF16), v_ref[0, d_rows, :].astype(BF16))
                   + _dot(e_p.astype(BF16), v_ref[0, p_rows, :].astype(BF16)))
            acc_ref[p, rows, :] = jnp.where(lo, acc[:QBLK], acc[QBLK:])
            m_ref[p, rows, :] = jnp.where(lo, m[:QBLK], m[QBLK:])
            l_ref[p, rows, :] = jnp.where(lo, l[:QBLK], l[QBLK:])
            return carry

        lax.fori_loop(0, span // QBLK, tile, 0, unroll=DIL_UNROLL)

    def combine(c, carry):
        rows = pl.ds(pl.multiple_of(c * QBLK, QBLK), QBLK)
        ms = [m_ref[p, rows, :] for p in range(len(DIL_PATTERNS))]
        mx = functools.reduce(jnp.maximum, ms)
        num, den = 0.0, 0.0
        for p, mp in enumerate(ms):
            w = jnp.exp(mp - mx)
            num = num + w * acc_ref[p, rows, :]
            den = den + w * l_ref[p, rows, :]
        o_ref[0, rows, :] = (num * (1.0 / den)).astype(o_ref.dtype)
        return carry

    lax.fori_loop(0, span // QBLK, combine, 0)


def dil_prompt_attn(q, kv):
    b, t, w = q.shape
    npair = w // LANES
    span = min(DIL_SPAN, t)
    n_pat = len(DIL_PATTERNS)
    return pl.pallas_call(
        _dil_prompt_kernel,
        grid=(b, npair, t // span),
        in_specs=[pl.BlockSpec((1, span, LANES), lambda i, p, j: (i, j, p)),
                  pl.BlockSpec((1, t, LANES), lambda i, p, j: (i, 0, p)),
                  pl.BlockSpec((1, t, LANES), lambda i, p, j: (i, 0, npair + p))],
        out_specs=pl.BlockSpec((1, span, LANES), lambda i, p, j: (i, j, p)),
        out_shape=jax.ShapeDtypeStruct((b, t, w), BF16),
        scratch_shapes=[pltpu.VMEM((n_pat, span, LANES), F32)] * 3,
        compiler_params=_cparams("arbitrary", "arbitrary", "arbitrary"),
        name="dil_prompt_attn",
    )(q, kv, kv)


NSA_G, NSA_REP = 2, 4
NSA_QW = NSA_G * NSA_REP * HEAD_DIM
QPERM = np.arange(NSA_QW).reshape(NSA_G, NSA_REP, HEAD_DIM).transpose(1, 0, 2).reshape(-1)


def _block_diag(blocks):
    n, a, b = blocks.shape
    out = jnp.zeros((n, a, n, b), blocks.dtype)
    out = out.at[jnp.arange(n), :, jnp.arange(n), :].set(blocks)
    return out.reshape(n * a, n * b)


def prep_layer0(p):
    w_in = p["w_in"]
    d = w_in.shape[0]
    n_gate = NSA_G * NSA_REP * 3
    kv_w = 6 * NSA_G * HEAD_DIM
    c_gl = NSA_QW + kv_w
    c_xr = c_gl + n_gate
    d_rnn = (w_in.shape[1] - c_xr) // 2
    cols = np.concatenate([QPERM, np.arange(NSA_QW, c_gl), np.arange(c_xr, c_xr + 2 * d_rnn),
                           np.arange(c_gl, c_xr)])
    w = jnp.concatenate([w_in[:, cols], jnp.zeros((d, LANES - n_gate), w_in.dtype)], axis=1).astype(BF16)
    n = w.shape[1]
    hg = jnp.ones((n,), F32)
    hg = hg.at[0:NSA_QW].set(jnp.tile(p["q_gain"], NSA_QW // HEAD_DIM))
    hg = hg.at[NSA_QW + 2 * LANES:NSA_QW + 3 * LANES].set(jnp.tile(p["k_gain"][1], 2))
    hg = hg.at[NSA_QW + 4 * LANES:NSA_QW + 5 * LANES].set(jnp.tile(p["k_gain"][2], 2))
    qb = NSA_QW // LANES
    c_rows, c_win, c_x, c_g, c_l = NSA_QW, NSA_QW + 4 * LANES, NSA_QW + 6 * LANES, NSA_QW + 6 * LANES + d_rnn, \
        NSA_QW + 6 * LANES + 2 * d_rnn
    outs = ((0, NSA_QW, BF16), (c_rows, 4 * LANES, F32), (c_rows, 4 * LANES, BF16), (c_win, 2 * LANES, F32),
            (c_win, 2 * LANES, BF16), (c_x, d_rnn, F32), (c_g, d_rnn, F32), (c_l, LANES, F32))
    cw = {}
    for c, nm in enumerate("kv"):
        w1 = p["cmp_w1"][c]
        big = jnp.zeros((CMP_BLOCK, NSA_G, HEAD_DIM, NSA_G, HEAD_DIM), F32)
        for g in range(NSA_G):
            big = big.at[:, g, :, g, :].set(w1)
        cw["w1" + nm] = big.reshape(CMP_BLOCK * LANES, LANES).astype(BF16)
        cw["pos" + nm] = jnp.tile(p["cmp_pos"][c], (1, NSA_G)).reshape(1, CMP_BLOCK * LANES)
        cw["w2" + nm] = _block_diag(jnp.stack([p["cmp_w2"][c]] * NSA_G)).astype(BF16)
    cw["kg0"] = jnp.tile(p["k_gain"][0], 2).reshape(1, LANES)
    rw = dict(conv_w=p["conv_w"], conv_b=p["conv_b"].reshape(1, -1),
              wa=_block_diag(p["wa"]).astype(BF16), ba=p["ba"].reshape(1, -1),
              wx=_block_diag(p["wx"]).astype(BF16), bx=p["bx"].reshape(1, -1), lam=p["lam"].reshape(1, -1))
    w_out = p["w_out"]
    return dict(w=w, hg=hg.reshape(1, n), norm_blocks=tuple(range(qb)) + (qb + 2, qb + 4), outs=outs,
                cw=cw, rw=rw, wo_a=w_out[QPERM].astype(BF16), wo_b=w_out[NSA_QW:].astype(BF16))


def prep_layer1(p):
    w = p["w_in"].astype(BF16)
    n = w.shape[1]
    c_w = p["v_gain"].shape[0]
    dil_w = (n - 2 * c_w) // 3
    hg = jnp.ones((n,), F32)
    hg = hg.at[2 * c_w:2 * c_w + dil_w].set(jnp.tile(p["q_gain"], dil_w // HEAD_DIM))
    hg = hg.at[2 * c_w + dil_w:2 * c_w + 2 * dil_w].set(jnp.tile(p["k_gain"], dil_w // HEAD_DIM))
    b0 = 2 * c_w // LANES
    nbq = dil_w // LANES
    outs = ((0, c_w, F32), (c_w, c_w, F32), (2 * c_w, dil_w, F32), (2 * c_w + dil_w, 2 * dil_w, F32))
    w_out = p["w_out"]
    gw = c_w // p["ws"].shape[0]
    return dict(w=w, hg=hg.reshape(1, n), norm_blocks=tuple(range(b0, b0 + 2 * nbq)), outs=outs,
                v_gain=p["v_gain"].reshape(1, c_w), ws=p["ws"], bs_exp=jnp.repeat(p["bs"].T, gw, axis=1),
                ws_diag=jnp.repeat(p["ws"][:, 0, 0], gw).reshape(1, c_w),
                bs0=jnp.repeat(p["bs"][:, 0], gw).reshape(1, c_w),
                wo_a=w_out[:c_w].astype(BF16), wo_b=w_out[c_w:].astype(BF16))


def layer0_prompt(x, mod, ln_mix, ln_ffn, ffn_w, pp):
    b, t, _ = x.shape
    q_bf, rows, rows_bf, win, win_bf, xr, gr, gl = in_proj(
        x, mod, 0, 1, ln_mix, pp["w"], pp["hg"], pp["norm_blocks"], pp["outs"], tm=512)
    kc, vc = nsa_compress(rows, pp["cw"])
    o_nsa = nsa_prompt_attn(q_bf, kc, vc, rows_bf, win_bf, gl)
    o_rnn, h_last, conv_last = rglru_prompt(xr, gr, pp["rw"], tm=256)
    y = out_ffn(o_nsa, o_rnn, x, mod, ln_ffn, pp["wo_a"], pp["wo_b"], *ffn_w, tm=512)
    nwin = min(NSA_WINDOW, t)
    state = (rows.reshape(b, t, 4, NSA_G, HEAD_DIM), win[:, t - nwin:].reshape(b, nwin, 2, NSA_G, HEAD_DIM),
             h_last[:, 0], conv_last[:, 5:8])
    return y, state


def layer1_prompt(x, mod, ln_mix, ln_ffn, ffn_w, pp):
    b, t, _ = x.shape
    u, v, q, kv = in_proj(x, mod, 0, 1, ln_mix, pp["w"], pp["hg"], pp["norm_blocks"], pp["outs"], tm=512)
    o_c = gmlp_prompt(u, v, pp["v_gain"], pp["ws"], pp["bs_exp"])
    o_d = dil_prompt_attn(q, kv)
    y = out_ffn(o_c, o_d, x, mod, ln_ffn, pp["wo_a"], pp["wo_b"], *ffn_w, tm=512)
    nkv = min(DIL_MAX, t)
    heads = kv.shape[2] // (2 * HEAD_DIM)
    return y, kv[:, t - nkv:].reshape(b, nkv, 2, heads, HEAD_DIM)


def _row_softmax_parts(s_list, mask_list, s_new):
    m = s_new[0]
    for sn in s_new[1:]:
        m = jnp.maximum(m, sn)
    for s, mk in zip(s_list, mask_list):
        sm = s if mk is None else jnp.where(mk, s, NEG_INF)
        m = jnp.maximum(m, jnp.max(sm, axis=-1, keepdims=True))
    es, den = [], 0.0
    for s, mk in zip(s_list, mask_list):
        e = jnp.exp(s - m)
        if mk is not None:
            e = jnp.where(mk, e, 0.0)
        es.append(e)
        den = den + jnp.sum(e, axis=-1, keepdims=True)
    en = [jnp.exp(sn - m) for sn in s_new]
    for e in en:
        den = den + e
    return es, en, 1.0 / den


def _nsa_decode_kernel(pt_ref, cache_ref, q_ref, rn_ref, wn_ref, wnc_ref, gl_ref, sw_ref,
                       w1k_ref, posk_ref, w2k_ref, w1v_ref, posv_ref, w2v_ref, kg_ref, ex_ref,
                       o_ref, wo_ref, buf, xk, xv, sem):
    b = pl.program_id(0)
    nbatch = pl.num_programs(0)
    slot = b % 2
    n_pages = pt_ref.shape[1]
    page = cache_ref.shape[3]
    past = n_pages * page
    nc = past // CMP_BLOCK
    nb = past // SEL_BLOCK
    scale = HEAD_DIM ** -0.5
    rep = q_ref.shape[2] // LANES
    nh = 2 * rep

    nseq = q_ref.shape[0]

    def copies(step, sl):
        return [pltpu.make_async_copy(cache_ref.at[pt_ref[step * nseq + u, j]], buf.at[sl, u, j], sem.at[sl, u, j])
                for u in range(nseq) for j in range(n_pages)]

    @pl.when(b == 0)
    def _():
        for c in copies(0, 0):
            c.start()

    @pl.when(b + 1 < nbatch)
    def _():
        for c in copies(b + 1, 1 - slot):
            c.start()

    for c in copies(b, slot):
        c.wait()

    lane = lax.broadcasted_iota(jnp.int32, (nh, LANES), 1)
    hrow = lax.broadcasted_iota(jnp.int32, (nh, LANES), 0)
    blk_t = lax.broadcasted_iota(jnp.int32, (LANES, LANES), 0)
    wb = sw_ref.shape[3]
    last = lax.broadcasted_iota(jnp.int32, (LANES, wb), 1) == wb - 1

    def padded(c2):
        z = jnp.zeros((LANES - nc // 2, LANES), F32)
        return jnp.concatenate([c2[:nc // 2], z, c2[nc // 2:], z], axis=0)

    def one_sequence(u):
        q = q_ref[u]
        qrows = []
        for g in range(2):
            for r in range(rep):
                qrows.append(q[:, r * LANES:(r + 1) * LANES])
        qm = jnp.concatenate(qrows, axis=0)
        qm = jnp.where((lane // HEAD_DIM) == (hrow // rep), qm, jnp.zeros((), qm.dtype))
        qf = qm.astype(F32)

        for j in range(n_pages):
            xk[u, pl.ds(j * page, page), :] = buf[slot, u, j, 0].T
            xv[u, pl.ds(j * page, page), :] = buf[slot, u, j, 1].T
        ldk = lambda s, n, st: xk[u, pl.ds(s, n, stride=st), :]
        ldv = lambda s, n, st: xv[u, pl.ds(s, n, stride=st), :]
        kc = _head_norm(_compress_rows(ldk, nc // 2, w1k_ref, posk_ref, w2k_ref), kg_ref[...])
        vc = _compress_rows(ldv, nc // 2, w1v_ref, posv_ref, w2v_ref)
        kcp = padded(kc).astype(BF16)
        vcp = padded(vc).astype(BF16)
        s_c = _dot_nt(qm, kcp) * scale
        cl = lax.broadcasted_iota(jnp.int32, (nh, 2 * LANES), 1)
        cmask = (cl % LANES) < nc // 2
        s_c = jnp.where(cmask, s_c, NEG_INF)
        e_c = jnp.where(cmask, jnp.exp(s_c - jnp.max(s_c, axis=-1, keepdims=True)), 0.0)
        p_c = e_c * (1.0 / jnp.maximum(jnp.sum(e_c, axis=-1, keepdims=True), 1e-30))
        o_cmp = _dot(p_c.astype(BF16), vcp)

        pp = p_c[:, :LANES] + p_c[:, LANES:]
        imps = []
        for g in range(2):
            ig = pp[g * rep:g * rep + 1]
            for r in range(1, rep):
                ig = ig + pp[g * rep + r:g * rep + r + 1]
            imps += [ig] * rep
        imp = jnp.concatenate(imps + [jnp.zeros((LANES - nh, LANES), F32)], axis=0)
        imp_t = imp.T
        imp_t = jnp.where((blk_t == 0) | (blk_t == nb), FORCE_SCORE, imp_t)
        imp_t = jnp.where(blk_t <= nb, imp_t, -3.0)
        sel = _select_blocks_t(imp_t, blk_t.astype(F32), min(N_SEL, nb + 1)).T[:nh]
        selm = _dot(sel.astype(BF16), ex_ref[...]) > 0.5

        rn = rn_ref[u]
        kst = jnp.concatenate([buf[slot, u, j, 2] for j in range(n_pages)], axis=1).astype(BF16)
        vst = jnp.concatenate([buf[slot, u, j, 3] for j in range(n_pages)], axis=1).astype(BF16)
        s_sel = _dot(qm, kst) * scale
        s_sel_new = jnp.sum(qf * rn[:, 2 * LANES:3 * LANES], axis=-1, keepdims=True) * scale
        (e_s,), (e_sn,), inv_s = _row_softmax_parts([s_sel], [selm], [s_sel_new])
        o_sel = (_dot_nt(e_s.astype(BF16), vst) + e_sn * rn[:, 3 * LANES:4 * LANES]) * inv_s

        kwt, vwt = sw_ref[u, 0], sw_ref[u, 1]
        wn = wn_ref[u]
        s_w = _dot(qm, kwt.astype(BF16)) * scale
        s_w_new = jnp.sum(qf * wn[:, :LANES], axis=-1, keepdims=True) * scale
        (e_w,), (e_wn,), inv_w = _row_softmax_parts([s_w], [None], [s_w_new])
        o_win = (_dot_nt(e_w.astype(BF16), vwt.astype(BF16)) + e_wn * wn[:, LANES:]) * inv_w

        gate = _sigmoid(gl_ref[u])
        o = gate[:, 0:1] * o_cmp + gate[:, 1:2] * o_sel + gate[:, 2:3] * o_win
        l1 = lax.broadcasted_iota(jnp.int32, (1, LANES), 1)
        o_ref[u] = jnp.concatenate([jnp.where(l1 < HEAD_DIM, o[r:r + 1], o[rep + r:rep + r + 1])
                                    for r in range(rep)], axis=1).astype(o_ref.dtype)
        wo_ref[u, 0] = jnp.where(last, wnc_ref[u, 0], pltpu.roll(kwt, wb - 1, axis=1))
        wo_ref[u, 1] = jnp.where(last, wnc_ref[u, 1], pltpu.roll(vwt, wb - 1, axis=1))

    for u in range(nseq):
        one_sequence(u)


def nsa_decode(page_table, cache, q_bf, rows_new, win_new, gl3, state_win, cw):
    b, n_pages = page_table.shape
    n_pool, page, ncomp, ng, hd = cache.shape
    past = n_pages * page
    nb = past // SEL_BLOCK
    assert past % SEL_BLOCK == 0 and nb < LANES and past // CMP_BLOCK <= 2 * LANES and ng * hd == LANES
    wb = state_win.shape[1]
    assert wb <= NSA_WINDOW
    cache_t = jnp.transpose(cache, (0, 2, 3, 4, 1)).reshape(n_pool, ncomp, LANES, page)
    win_t = jnp.transpose(state_win, (0, 2, 3, 4, 1)).reshape(b, 2, LANES, wb)
    win_col = win_new.reshape(b, 2, LANES, 1)
    expand = (jnp.arange(LANES)[:, None] == (jnp.arange(past) // SEL_BLOCK)[None, :]).astype(BF16)
    ws = (cw["w1k"], cw["posk"], cw["w2k"], cw["w1v"], cw["posv"], cw["w2v"], cw["kg0"], expand)
    ns = NSA_DECODE_SEQS
    assert b % ns == 0
    per_b = lambda a: pl.BlockSpec((ns,) + a.shape[1:], lambda i, pt: (i,) + (0,) * (a.ndim - 1))
    const = lambda a: pl.BlockSpec(a.shape, lambda i, pt: (0,) * a.ndim)
    grid_spec = pltpu.PrefetchScalarGridSpec(
        num_scalar_prefetch=1,
        grid=(b // ns,),
        in_specs=[pl.BlockSpec(memory_space=pl.ANY), per_b(q_bf), per_b(rows_new), per_b(win_new), per_b(win_col),
                  per_b(gl3), per_b(win_t)] + [const(a) for a in ws],
        out_specs=[pl.BlockSpec((ns, 1, q_bf.shape[2]), lambda i, pt: (i, 0, 0)),
                   pl.BlockSpec((ns, 2, LANES, wb), lambda i, pt: (i, 0, 0, 0))],
        scratch_shapes=[pltpu.VMEM((2, ns, n_pages, ncomp, LANES, page), F32), pltpu.VMEM((ns, past, LANES), F32),
                        pltpu.VMEM((ns, past, LANES), F32), pltpu.SemaphoreType.DMA((2, ns, n_pages))],
    )
    o, win_out_t = pl.pallas_call(
        _nsa_decode_kernel,
        grid_spec=grid_spec,
        out_shape=[jax.ShapeDtypeStruct((b, 1, q_bf.shape[2]), BF16),
                   jax.ShapeDtypeStruct(win_t.shape, F32)],
        compiler_params=_cparams("arbitrary"),
        name="nsa_decode",
    )(page_table, cache_t, q_bf, rows_new, win_new, win_col, gl3, win_t, *ws)
    return o, jnp.transpose(win_out_t.reshape(b, 2, ng, hd, wb), (0, 4, 1, 2, 3))


def _rglru_decode_kernel(xr_ref, gr_ref, cs_ref, h0_ref, cw_ref, cb_ref, wa_ref, ba_ref, wx_ref, bx_ref, lam_ref,
                         o_ref, h_ref, cn_ref):
    x = xr_ref[...]
    xc = cb_ref[...] + cw_ref[3:4, :] * x
    for k in range(3):
        xc = xc + cw_ref[k:k + 1, :] * cs_ref[k]
    a, u = _rglru_gates(xc, wa_ref, ba_ref, wx_ref, bx_ref, lam_ref)
    h = a * h0_ref[...] + u
    h_ref[...] = h
    o_ref[...] = (h * _gelu(gr_ref[...])).astype(o_ref.dtype)
    cn_ref[0] = cs_ref[1]
    cn_ref[1] = cs_ref[2]
    cn_ref[2] = x


def rglru_decode(xr, gr, conv_t, h0, rw):
    ws = (rw["conv_w"], rw["conv_b"], rw["wa"], rw["ba"], rw["wx"], rw["bx"], rw["lam"])
    return pl.pallas_call(
        _rglru_decode_kernel,
        out_shape=[jax.ShapeDtypeStruct(xr.shape, BF16), jax.ShapeDtypeStruct(xr.shape, F32),
                   jax.ShapeDtypeStruct(conv_t.shape, F32)],
        compiler_params=pltpu.CompilerParams(vmem_limit_bytes=VMEM_LIMIT),
        name="rglru_decode",
    )(xr, gr, conv_t, h0, *ws)


def _gmlp_decode_kernel(u_ref, v_ref, vg_ref, wd_ref, b0_ref, o_ref, vn_ref):
    v = _gmlp_v(v_ref[...], vg_ref)
    vn_ref[...] = v
    o_ref[...] = (_gelu(u_ref[...]) * (wd_ref[...] * v + b0_ref[...])).astype(o_ref.dtype)


def gmlp_decode(u, v, v_gain, ws_diag, bs0):
    return pl.pallas_call(
        _gmlp_decode_kernel,
        out_shape=[jax.ShapeDtypeStruct(u.shape, BF16), jax.ShapeDtypeStruct(u.shape, F32)],
        compiler_params=pltpu.CompilerParams(vmem_limit_bytes=VMEM_LIMIT),
        name="gmlp_decode",
    )(u, v, v_gain, ws_diag, bs0)


def _dil_decode_kernel(q_ref, kvn_ref, st_ref, o_ref, so_ref):
    nh, hd, wb = st_ref.shape[2], st_ref.shape[3], st_ref.shape[4]
    scale = HEAD_DIM ** -0.5
    n_pat = len(DIL_PATTERNS)
    lane1 = lax.broadcasted_iota(jnp.int32, (1, wb), 1)
    dist = wb - lane1
    cnt = jnp.zeros((1, wb), F32)
    for window, dil in DIL_PATTERNS:
        cnt = cnt + jnp.where((dist <= window) & ((dist & (dil - 1)) == 0), 1.0, 0.0)
    last = lax.broadcasted_iota(jnp.int32, (hd, wb), 1) == wb - 1

    def shifted(x, new_col):
        return jnp.where(last, new_col, pltpu.roll(x, wb - 1, axis=1))

    s_rows, s_new_rows = [], []
    for h in range(nh):
        kh = st_ref[0, 0, h]
        qh = q_ref[0, h] * scale
        kn = kvn_ref[0, 0, h]
        s_rows.append(jnp.sum(kh * qh, axis=0, keepdims=True))
        s_new_rows.append(jnp.sum(kn * qh, axis=0, keepdims=True))
        so_ref[0, 0, h] = shifted(kh, kn)
    s = jnp.concatenate(s_rows, axis=0)
    s_new = jnp.concatenate(s_new_rows, axis=0)
    s = jnp.where(cnt > 0.0, s, NEG_INF)
    m = jnp.maximum(jnp.max(s, axis=-1, keepdims=True), s_new)
    e = cnt * jnp.exp(s - m)
    e_new = n_pat * jnp.exp(s_new - m)
    inv = 1.0 / (jnp.sum(e, axis=-1, keepdims=True) + e_new)
    for h in range(nh):
        vh = st_ref[0, 1, h]
        vn = kvn_ref[0, 1, h]
        acc = jnp.sum(vh * e[h:h + 1, :], axis=1, keepdims=True) + e_new[h:h + 1, :] * vn
        o_ref[0, h] = acc * inv[h:h + 1, :]
        so_ref[0, 1, h] = shifted(vh, vn)


def dil_decode(q, kv_new, state):
    b, wb, _, nh, hd = state.shape
    assert wb == DIL_MAX
    st_t = jnp.transpose(state, (0, 2, 3, 4, 1))
    o, so_t = pl.pallas_call(
        _dil_decode_kernel,
        grid=(b,),
        in_specs=[pl.BlockSpec((1, nh, hd, 1), lambda i: (i, 0, 0, 0)),
                  pl.BlockSpec((1, 2, nh, hd, 1), lambda i: (i, 0, 0, 0, 0)),
                  pl.BlockSpec((1, 2, nh, hd, wb), lambda i: (i, 0, 0, 0, 0))],
        out_specs=[pl.BlockSpec((1, nh, hd, 1), lambda i: (i, 0, 0, 0)),
                   pl.BlockSpec((1, 2, nh, hd, wb), lambda i: (i, 0, 0, 0, 0))],
        out_shape=[jax.ShapeDtypeStruct((b, nh, hd, 1), F32), jax.ShapeDtypeStruct(st_t.shape, F32)],
        compiler_params=_cparams("arbitrary"),
        name="dil_decode",
    )(q[..., None], kv_new[..., None], st_t)
    return o[..., 0], jnp.transpose(so_t, (0, 4, 1, 2, 3))


def layer0_sample(x, mod, ln_mix, ln_ffn, ffn_w, pp, cache, page_table, state_win, state_h, state_conv):
    b, _, d = x.shape
    xs = x.reshape(1, b, d)
    q_bf, rows, _, win, _, xr, gr, gl = in_proj(
        xs, mod, 0, 1, ln_mix, pp["w"], pp["hg"], pp["norm_blocks"], pp["outs"], tm=b)
    n_gate = NSA_G * NSA_REP * 3
    gl3 = gl[0, :, :n_gate].reshape(b, NSA_G * NSA_REP, 3)
    o_nsa, win_out = nsa_decode(page_table, cache, q_bf.reshape(b, 1, -1), rows.reshape(b, 1, -1),
                                win.reshape(b, 1, -1), gl3, state_win, pp["cw"])
    o_rnn, h_new, conv_new = rglru_decode(xr[0], gr[0], state_conv.transpose(1, 0, 2), state_h, pp["rw"])
    y = out_ffn(o_nsa.reshape(1, b, -1), o_rnn[None], xs, mod, ln_ffn, pp["wo_a"], pp["wo_b"], *ffn_w, tm=b)
    state = (rows.reshape(b, 1, 4, NSA_G, HEAD_DIM), win_out, h_new, conv_new.transpose(1, 0, 2))
    return y.reshape(b, 1, d), state


def layer1_sample(x, mod, ln_mix, ln_ffn, ffn_w, pp, state_dil):
    b, _, d = x.shape
    xs = x.reshape(1, b, d)
    u, v, q, kv = in_proj(xs, mod, 0, 1, ln_mix, pp["w"], pp["hg"], pp["norm_blocks"], pp["outs"], tm=b)
    o_c, v_n = gmlp_decode(u[0], v[0], pp["v_gain"], pp["ws_diag"], pp["bs0"])
    heads = state_dil.shape[3]
    o_d, dil_out = dil_decode(q.reshape(b, heads, HEAD_DIM), kv.reshape(b, 2, heads, HEAD_DIM), state_dil)
    y = out_ffn(o_c[None], o_d.reshape(1, b, -1).astype(BF16), xs, mod, ln_ffn, pp["wo_a"], pp["wo_b"], *ffn_w,
                tm=b)
    return y.reshape(b, 1, d), (dil_out, v_n.reshape(b, 1, -1))


def kernel(x_prompt, x_sample, cache_nsa_kv, state_nsa_win, state_rglru_h, state_rglru_conv, state_dil_kv, page_table, c_prompt, c_sample, norm_mix_g, norm_ffn_g, w_ada, b_ada, w_ffn_gate, w_ffn_up, w_ffn_down, w_in_ab, w_out_ab, nsa_q_gain, nsa_k_gain, nsa_cmp_w1, nsa_cmp_w2, nsa_cmp_pos, rg_conv_w, rg_conv_b, rg_wa, rg_ba, rg_wx, rg_bx, rg_lambda, w_in_cd, w_out_cd, gmlp_v_gain, gmlp_ws, gmlp_bs, dil_q_gain, dil_k_gain):
    depth = norm_mix_g.shape[0]
    bp, bs = x_prompt.shape[0], x_sample.shape[0]
    pad = -(bp + bs) % 8
    c_all = jnp.concatenate([c_prompt, c_sample, jnp.zeros((pad, c_prompt.shape[1]), F32)], axis=0)
    mod_all = ada_mod(c_all, w_ada.astype(BF16), b_ada)
    yp, ys = x_prompt, x_sample
    kv_p, kv_s, win_p, win_s, h_p, h_s, conv_p, conv_s, dil_p, dil_s, gv_s = ([] for _ in range(11))
    for layer in range(depth):
        i = layer // 2
        mod_p = mod_all[layer, :bp, None, :]
        mod_s = mod_all[layer, None, bp:bp + bs, :]
        ffn_w = (w_ffn_gate[layer].astype(BF16), w_ffn_up[layer].astype(BF16), w_ffn_down[layer].astype(BF16))
        if layer % 2 == 0:
            pp = prep_layer0(dict(w_in=w_in_ab[i], w_out=w_out_ab[i], q_gain=nsa_q_gain[i], k_gain=nsa_k_gain[i],
                                  cmp_w1=nsa_cmp_w1[i], cmp_w2=nsa_cmp_w2[i], cmp_pos=nsa_cmp_pos[i],
                                  conv_w=rg_conv_w[i], conv_b=rg_conv_b[i], wa=rg_wa[i], ba=rg_ba[i],
                                  wx=rg_wx[i], bx=rg_bx[i], lam=rg_lambda[i]))
            yp, st = layer0_prompt(yp, mod_p, norm_mix_g[layer], norm_ffn_g[layer], ffn_w, pp)
            kv_p.append(st[0]); win_p.append(st[1]); h_p.append(st[2]); conv_p.append(st[3])
            ys, st = layer0_sample(ys, mod_s, norm_mix_g[layer], norm_ffn_g[layer], ffn_w, pp, cache_nsa_kv[i],
                                   page_table, state_nsa_win[i], state_rglru_h[i], state_rglru_conv[i])
            kv_s.append(st[0]); win_s.append(st[1]); h_s.append(st[2]); conv_s.append(st[3])
        else:
            pp = prep_layer1(dict(w_in=w_in_cd[i], w_out=w_out_cd[i], v_gain=gmlp_v_gain[i], ws=gmlp_ws[i],
                                  bs=gmlp_bs[i], q_gain=dil_q_gain[i], k_gain=dil_k_gain[i]))
            yp, st = layer1_prompt(yp, mod_p, norm_mix_g[layer], norm_ffn_g[layer], ffn_w, pp)
            dil_p.append(st)
            ys, st = layer1_sample(ys, mod_s, norm_mix_g[layer], norm_ffn_g[layer], ffn_w, pp, state_dil_kv[i])
            dil_s.append(st[0]); gv_s.append(st[1])
    return (yp, ys, jnp.stack(kv_p), jnp.stack(kv_s), jnp.stack(win_p), jnp.stack(win_s),
            jnp.stack(h_p), jnp.stack(h_s), jnp.stack(conv_p), jnp.stack(conv_s),
            jnp.stack(dil_p), jnp.stack(dil_s), jnp.stack(gv_s))
```

```python
import functools

import numpy as np
import jax
import jax.numpy as jnp
from jax import lax
from jax.experimental import pallas as pl
from jax.experimental.pallas import tpu as pltpu

F32 = jnp.float32
BF16 = jnp.bfloat16

LANES = 128
HEAD_DIM = 64
QBLK = 128
CMP_BLOCK = 32
SEL_BLOCK = 64
N_SEL = 16
NSA_WINDOW = 512
FORCE_SCORE = 1.0e4
DIL_PATTERNS = ((128, 1), (512, 4), (2048, 16))
DIL_MAX = 2048
RG_C = 8.0
RMS_EPS = 1e-6
NEG_INF = -1e30
LOG2_E = 1.4426950408889634
VMEM_LIMIT = 56 * 1024 * 1024


def _cparams(*sem):
    return pltpu.CompilerParams(dimension_semantics=sem, vmem_limit_bytes=VMEM_LIMIT)


def _dot(a, b):
    return jnp.dot(a, b, preferred_element_type=F32)


def _dot_nt(a, b):
    return lax.dot_general(a, b, (((1,), (1,)), ((), ())), preferred_element_type=F32)


def _gelu(x):
    return 0.5 * x * (1.0 + jnp.tanh(np.sqrt(2.0 / np.pi) * (x + 0.044715 * (x * x * x))))


def _sigmoid(x):
    return 1.0 / (1.0 + jnp.exp(-x))


def _head_norm(z, gain):
    lo = lax.broadcasted_iota(jnp.int32, z.shape, 1) < HEAD_DIM
    z2 = z * z
    s_lo = jnp.sum(jnp.where(lo, z2, 0.0), axis=-1, keepdims=True)
    s_hi = jnp.sum(jnp.where(lo, 0.0, z2), axis=-1, keepdims=True)
    inv = lax.rsqrt(jnp.where(lo, s_lo, s_hi) * (1.0 / HEAD_DIM) + RMS_EPS)
    return z * inv * gain


def _ada_kernel(c_ref, w_ref, b_ref, o_ref):
    c = c_ref[...]
    s = c * _sigmoid(c)
    o_ref[0] = _dot(s.astype(BF16), w_ref[0]) + b_ref[0]


def ada_mod(c_all, w_ada, b_ada):
    m, d = c_all.shape
    nl, _, n = w_ada.shape
    tn = 1536
    return pl.pallas_call(
        _ada_kernel,
        grid=(nl, n // tn),
        in_specs=[pl.BlockSpec((m, d), lambda l, j: (0, 0)),
                  pl.BlockSpec((1, d, tn), lambda l, j: (l, 0, j)),
                  pl.BlockSpec((1, 1, tn), lambda l, j: (l, 0, j))],
        out_specs=pl.BlockSpec((1, m, tn), lambda l, j: (l, 0, j)),
        out_shape=jax.ShapeDtypeStruct((nl, m, n), F32),
        compiler_params=_cparams("arbitrary", "arbitrary"),
        name="ada_mod",
    )(c_all, w_ada, b_ada.reshape(nl, 1, n))


def _in_proj_kernel(x_ref, sc_ref, sh_ref, g_ref, w_ref, hg_ref, *out_refs, norm_blocks, outs):
    x = x_ref[0]
    ms = jnp.mean(x * x, axis=-1, keepdims=True)
    h = x * lax.rsqrt(ms + RMS_EPS) * g_ref[...]
    h = h * (1.0 + sc_ref[0]) + sh_ref[0]
    z = _dot(h.astype(BF16), w_ref[...])
    nblk = z.shape[1] // LANES
    blocks = []
    for j in range(nblk):
        zb = z[:, j * LANES:(j + 1) * LANES]
        if j in norm_blocks:
            zb = _head_norm(zb, hg_ref[:, j * LANES:(j + 1) * LANES])
        blocks.append(zb)
    for o_ref, (c0, width, _) in zip(out_refs, outs):
        for j in range(width // LANES):
            o_ref[0, :, j * LANES:(j + 1) * LANES] = blocks[c0 // LANES + j].astype(o_ref.dtype)


def in_proj(x, mod, sh_idx, sc_idx, g, w, head_gain, norm_blocks, outs, tm):
    bm, t, d = x.shape
    r = mod.shape[1]
    n = w.shape[1]
    tm = min(tm, t)
    if r == 1:
        mod_spec = lambda k: pl.BlockSpec((1, 1, d), lambda b, i: (b, 0, k))
    else:
        assert r == t and tm == t
        mod_spec = lambda k: pl.BlockSpec((1, tm, d), lambda b, i: (b, 0, k))
    kern = functools.partial(_in_proj_kernel, norm_blocks=tuple(norm_blocks), outs=tuple(outs))
    return pl.pallas_call(
        kern,
        grid=(bm, t // tm),
        in_specs=[pl.BlockSpec((1, tm, d), lambda b, i: (b, i, 0)),
                  mod_spec(sc_idx), mod_spec(sh_idx),
                  pl.BlockSpec((1, d), lambda b, i: (0, 0)),
                  pl.BlockSpec((d, n), lambda b, i: (0, 0)),
                  pl.BlockSpec((1, n), lambda b, i: (0, 0))],
        out_specs=[pl.BlockSpec((1, tm, wd), lambda b, i: (b, i, 0)) for (_, wd, _) in outs],
        out_shape=[jax.ShapeDtypeStruct((bm, t, wd), dt) for (_, wd, dt) in outs],
        compiler_params=_cparams("arbitrary", "arbitrary"),
        name="in_proj",
    )(x, mod, mod, g.reshape(1, d), w, head_gain)


def _out_ffn_kernel(ma_ref, mb_ref, x_ref, gm_ref, shf_ref, scf_ref, gf_ref, lnf_ref,
                    woa_ref, wob_ref, wg_ref, wu_ref, wd_ref, o_ref, *, hidden_chunk):
    mix = _dot(ma_ref[0], woa_ref[...]) + _dot(mb_ref[0], wob_ref[...])
    x1 = x_ref[0] + gm_ref[0] * mix
    ms = jnp.mean(x1 * x1, axis=-1, keepdims=True)
    hf = x1 * lax.rsqrt(ms + RMS_EPS) * lnf_ref[...]
    hf = (hf * (1.0 + scf_ref[0]) + shf_ref[0]).astype(BF16)
    hidden = wg_ref.shape[1]
    ffn = jnp.zeros(x1.shape, F32)
    for c0 in range(0, hidden, hidden_chunk):
        gt = _dot(hf, wg_ref[:, c0:c0 + hidden_chunk])
        up = _dot(hf, wu_ref[:, c0:c0 + hidden_chunk])
        act = (gt * _sigmoid(gt) * up).astype(BF16)
        ffn = ffn + _dot(act, wd_ref[c0:c0 + hidden_chunk, :])
    o_ref[0] = x1 + gf_ref[0] * ffn


def out_ffn(mix_a, mix_b, x, mod, ln_ffn, wo_a, wo_b, wg, wu, wd, tm):
    bm, t, d = x.shape
    r = mod.shape[1]
    tm = min(tm, t)
    ka, kb = mix_a.shape[2], mix_b.shape[2]
    hidden = wg.shape[1]
    if r == 1:
        mod_spec = lambda k: pl.BlockSpec((1, 1, d), lambda b, i: (b, 0, k))
    else:
        assert r == t and tm == t
        mod_spec = lambda k: pl.BlockSpec((1, tm, d), lambda b, i: (b, 0, k))
    const = lambda shape: pl.BlockSpec(shape, lambda b, i: (0,) * len(shape), pipeline_mode=pl.Buffered(1))
    kern = functools.partial(_out_ffn_kernel, hidden_chunk=hidden // 2)
    return pl.pallas_call(
        kern,
        grid=(bm, t // tm),
        in_specs=[pl.BlockSpec((1, tm, ka), lambda b, i: (b, i, 0)),
                  pl.BlockSpec((1, tm, kb), lambda b, i: (b, i, 0)),
                  pl.BlockSpec((1, tm, d), lambda b, i: (b, i, 0)),
                  mod_spec(2), mod_spec(3), mod_spec(4), mod_spec(5),
                  const((1, d)), const((ka, d)), const((kb, d)),
                  const((d, hidden)), const((d, hidden)), const((hidden, d))],
        out_specs=pl.BlockSpec((1, tm, d), lambda b, i: (b, i, 0)),
        out_shape=jax.ShapeDtypeStruct((bm, t, d), F32),
        compiler_params=_cparams("arbitrary", "arbitrary"),
        name="out_ffn",
    )(mix_a, mix_b, x, mod, mod, mod, mod, ln_ffn.reshape(1, d), wo_a, wo_b, wg, wu, wd)


def _compress_rows(load, nh, w1_ref, pos_ref, w2_ref):
    halves = []
    for parity in range(2):
        halves.append(jnp.concatenate(
            [load(parity * CMP_BLOCK + l, nh, 2 * CMP_BLOCK) for l in range(CMP_BLOCK)], axis=1))
    xs = (jnp.concatenate(halves, axis=0) + pos_ref[...]).astype(BF16)
    hid = _gelu(_dot(xs, w1_ref[...]))
    return _dot(hid.astype(BF16), w2_ref[...])


def _nsa_compress_kernel(rk_ref, rv_ref, w1k_ref, posk_ref, w2k_ref, w1v_ref, posv_ref, w2v_ref, kg_ref,
                         kc_ref, vc_ref):
    nh = kc_ref.shape[1] // 2
    ldk = lambda s, n, st: rk_ref[0, pl.ds(s, n, stride=st), :]
    ldv = lambda s, n, st: rv_ref[0, pl.ds(s, n, stride=st), :]
    kc = _compress_rows(ldk, nh, w1k_ref, posk_ref, w2k_ref)
    kc_ref[0] = _head_norm(kc, kg_ref[...]).astype(kc_ref.dtype)
    vc_ref[0] = _compress_rows(ldv, nh, w1v_ref, posv_ref, w2v_ref).astype(vc_ref.dtype)


def nsa_compress(rows, cw):
    b, t, _ = rows.shape
    nc = t // CMP_BLOCK
    const = lambda a: pl.BlockSpec(a.shape, lambda i: (0,) * a.ndim)
    ws = (cw["w1k"], cw["posk"], cw["w2k"], cw["w1v"], cw["posv"], cw["w2v"], cw["kg0"])
    return pl.pallas_call(
        _nsa_compress_kernel,
        grid=(b,),
        in_specs=[pl.BlockSpec((1, t, LANES), lambda i: (i, 0, 0)),
                  pl.BlockSpec((1, t, LANES), lambda i: (i, 0, 1))] + [const(a) for a in ws],
        out_specs=[pl.BlockSpec((1, nc, LANES), lambda i: (i, 0, 0))] * 2,
        out_shape=[jax.ShapeDtypeStruct((b, nc, LANES), BF16)] * 2,
        compiler_params=_cparams("arbitrary"),
        name="nsa_compress",
    )(rows, rows, *ws)


def _select_blocks(imp, blk_f, n_pick):
    sel = jnp.zeros(imp.shape, F32)
    work = imp
    for _ in range(n_pick):
        m = jnp.max(work, axis=-1, keepdims=True)
        idx = jnp.min(jnp.where(work == m, blk_f, 1e9), axis=-1, keepdims=True)
        pick = blk_f == idx
        sel = jnp.where(pick, 1.0, sel)
        work = jnp.where(pick, -2.0, work)
    return sel


def _select_blocks_t(imp, blk_f, n_pick):
    sel = jnp.zeros(imp.shape, F32)
    work = imp
    for _ in range(n_pick):
        m = jnp.max(work, axis=0, keepdims=True)
        idx = jnp.min(jnp.where(work == m, blk_f, 1e9), axis=0, keepdims=True)
        pick = blk_f == idx
        sel = jnp.where(pick, 1.0, sel)
        work = jnp.where(pick, -2.0, work)
    return sel


NSA_SEL_CHUNK = 8
NSA_SUB = 256
NSA_DECODE_SEQS = 4

def _dot_tn(a, b):
    return lax.dot_general(a, b, (((0,), (0,)), ((), ())), preferred_element_type=F32)


def _nsa_prompt_kernel(q_ref, kc_ref, vc_ref, ks_ref, vs_ref, kw_ref, vw_ref, gl_ref, e_ref, o_ref,
                       s_ref, mx_ref, m_ref, l_ref, acc_ref):
    qi = pl.program_id(1)
    t0 = qi * QBLK
    nc = kc_ref.shape[1]
    nb = nc // 2
    scale = HEAD_DIM ** -0.5
    rep = q_ref.shape[2] // LANES
    nhead = 2 * rep
    ncol = nhead * QBLK

    lane = lax.broadcasted_iota(jnp.int32, (QBLK, LANES), 1)
    sub = lax.broadcasted_iota(jnp.int32, (QBLK, LANES), 0)
    tile8 = lambda x: jnp.concatenate([x] * nhead, axis=1)
    q = q_ref[0]
    qs = []
    for g in range(2):
        gm = (lane >= g * HEAD_DIM) & (lane < (g + 1) * HEAD_DIM)
        for r in range(rep):
            qs.append(jnp.where(gm, q[:, r * LANES:(r + 1) * LANES], jnp.zeros((), q.dtype)))
    qall = (jnp.concatenate(qs, axis=0).astype(F32) * (scale * LOG2_E)).astype(BF16)

    crow = lax.broadcasted_iota(jnp.int32, (nc, QBLK), 0)
    cidx = jnp.where(crow < nb, 2 * crow, 2 * (crow - nb) + 1)
    cvalid = ((cidx + 1) * CMP_BLOCK - 1) <= t0 + lax.broadcasted_iota(jnp.int32, (nc, QBLK), 1)
    cbias = tile8(jnp.where(cvalid, 0.0, NEG_INF))
    cone = tile8(jnp.where(cvalid, 1.0, 0.0))
    wkeys = NSA_WINDOW + QBLK
    wstart = jnp.maximum(qi - NSA_WINDOW // QBLK, 0) * QBLK
    dist = (t0 + lax.broadcasted_iota(jnp.int32, (wkeys, QBLK), 1)) \
        - (wstart + lax.broadcasted_iota(jnp.int32, (wkeys, QBLK), 0))
    wbias = tile8(jnp.where((dist >= 0) & (dist <= NSA_WINDOW), 0.0, NEG_INF))
    wrows = pl.ds(pl.multiple_of(wstart, QBLK), wkeys)
    sc = _dot_nt(kc_ref[0], qall) + cbias
    sw = _dot_nt(kw_ref[0, wrows, :], qall) + wbias
    mc = jnp.max(sc, axis=0, keepdims=True)
    mw = jnp.max(sw, axis=0, keepdims=True)
    ec = jnp.exp2(sc - mc) * cone
    ew = jnp.exp2(sw - mw)
    p = ec * (1.0 / jnp.maximum(jnp.sum(ec, axis=0, keepdims=True), 1e-30))
    lw = jnp.sum(ew, axis=0, keepdims=True)
    o_cmp = _dot_tn(vc_ref[0], p.astype(BF16))
    o_win = _dot_tn(vw_ref[0, wrows, :], ew.astype(BF16)) * (1.0 / lw)

    brow = lax.broadcasted_iota(jnp.int32, (nb, 2 * QBLK), 0)
    tb = t0 + lax.broadcasted_iota(jnp.int32, (nb, 2 * QBLK), 1) % QBLK
    imps = []
    for g in range(2):
        ps = p[:, (g * rep) * QBLK:(g * rep + 1) * QBLK]
        for r in range(1, rep):
            ps = ps + p[:, (g * rep + r) * QBLK:(g * rep + r + 1) * QBLK]
        imps.append(ps[:nb] + ps[nb:])
    imp = jnp.concatenate(imps, axis=1)
    imp = jnp.where((brow == 0) | (brow == tb // SEL_BLOCK), FORCE_SCORE, imp)
    imp = jnp.where(brow * SEL_BLOCK <= tb, imp, -1.0)
    sel = _select_blocks_t(imp, brow.astype(F32), min(N_SEL, nb))
    bias_t = jnp.where(sel > 0.5, 0.0, NEG_INF)
    if nb < LANES:
        bias_t = jnp.concatenate([bias_t, jnp.zeros((LANES - nb, 2 * QBLK), F32)], axis=0)
    qbias = []
    for g in range(2):
        qbias += [bias_t[:, g * QBLK:(g + 1) * QBLK].T.astype(BF16)] * rep
    qaug = jnp.concatenate([qall, jnp.concatenate(qbias, axis=0)], axis=1)

    chunk = NSA_SEL_CHUNK * QBLK
    nsub = chunk // NSA_SUB
    sub_rows = lambda kc, j: pl.ds(pl.multiple_of(kc * chunk + j * NSA_SUB, NSA_SUB), NSA_SUB)
    neg_row = jnp.full((1, ncol), NEG_INF, F32)

    def scores(kc, j):
        r = sub_rows(kc, j)
        return _dot_nt(jnp.concatenate([ks_ref[0, r, :], e_ref[r, :]], axis=1), qaug)

    def rescale(mx):
        m_old = m_ref[...]
        m_new = jnp.maximum(m_old, mx)
        alpha = jnp.exp2(m_old - m_new)
        m_ref[...] = m_new
        l_ref[...] = alpha * l_ref[...]
        acc_ref[...] = alpha * acc_ref[...]
        return m_new

    def consume(s, v, m_new):
        e = jnp.exp2(s - m_new)
        l_ref[...] += jnp.sum(e, axis=0, keepdims=True)
        acc_ref[...] += _dot_tn(v, e.astype(BF16))

    def stage(kc, cur, nxt):
        m_new = rescale(mx_ref[cur])
        mx = neg_row
        for j in range(nsub):
            s_next = scores(kc + 1, j)
            s_ref[nxt, j * NSA_SUB:(j + 1) * NSA_SUB, :] = s_next
            mx = jnp.maximum(mx, jnp.max(s_next, axis=0, keepdims=True))
            consume(s_ref[cur, j * NSA_SUB:(j + 1) * NSA_SUB, :], vs_ref[0, sub_rows(kc, j), :], m_new)
        mx_ref[nxt] = mx

    m_ref[...] = neg_row
    l_ref[...] = jnp.zeros((1, ncol), F32)
    acc_ref[...] = jnp.zeros((LANES, ncol), F32)
    mx = neg_row
    for j in range(nsub):
        s0 = scores(0, j)
        s_ref[0, j * NSA_SUB:(j + 1) * NSA_SUB, :] = s0
        mx = jnp.maximum(mx, jnp.max(s0, axis=0, keepdims=True))
    mx_ref[0] = mx

    last = qi // NSA_SEL_CHUNK

    def stage_pair(pair, carry):
        stage(2 * pair, 0, 1)
        stage(2 * pair + 1, 1, 0)
        return carry

    lax.fori_loop(0, last // 2, stage_pair, 0)

    @pl.when(last % 2 == 1)
    def _():
        stage(last - 1, 0, 1)

    slot = last % 2
    kpos = last * chunk + lax.broadcasted_iota(jnp.int32, (chunk, QBLK), 0)
    causal = jnp.where(kpos <= t0 + lax.broadcasted_iota(jnp.int32, (chunk, QBLK), 1), 0.0, NEG_INF)
    s_last = [s_ref[slot, j * NSA_SUB:(j + 1) * NSA_SUB, :] + tile8(causal[j * NSA_SUB:(j + 1) * NSA_SUB])
              for j in range(nsub)]
    mx = neg_row
    for s in s_last:
        mx = jnp.maximum(mx, jnp.max(s, axis=0, keepdims=True))
    m_new = rescale(mx)
    q_end = (qi % NSA_SEL_CHUNK + 1) * QBLK
    consume(s_last[0], vs_ref[0, sub_rows(last, 0), :], m_new)
    for j in range(1, nsub):
        @pl.when(j * NSA_SUB < q_end)
        def _(j=j):
            consume(s_last[j], vs_ref[0, sub_rows(last, j), :], m_new)
    o_sel = acc_ref[...] * (1.0 / l_ref[...])

    gate = _sigmoid(gl_ref[0]).T
    for r in range(rep):
        og = []
        for g in range(2):
            c = (g * rep + r) * 3
            cols = slice((g * rep + r) * QBLK, (g * rep + r + 1) * QBLK)
            og.append(gate[c:c + 1] * o_cmp[:, cols] + gate[c + 1:c + 2] * o_sel[:, cols]
                      + gate[c + 2:c + 3] * o_win[:, cols])
        o_ref[0, :, r * LANES:(r + 1) * LANES] = jnp.where(sub < HEAD_DIM, og[0], og[1]).T.astype(o_ref.dtype)


def nsa_prompt_attn(q_bf, kc, vc, rows_bf, win_bf, gl):
    b, t, qw = q_bf.shape
    nc = kc.shape[1]
    nq = t // QBLK
    assert t % (NSA_SEL_CHUNK * QBLK) == 0 and t >= NSA_WINDOW + QBLK and t // SEL_BLOCK <= LANES
    ncol = 2 * (qw // LANES) * QBLK
    full = lambda k: pl.BlockSpec((1, t, LANES), lambda i, j: (i, 0, k))
    member = (jnp.arange(t)[:, None] // SEL_BLOCK == jnp.arange(LANES)[None, :]).astype(BF16)
    return pl.pallas_call(
        _nsa_prompt_kernel,
        grid=(b, nq),
        in_specs=[pl.BlockSpec((1, QBLK, qw), lambda i, j: (i, j, 0)),
                  pl.BlockSpec((1, nc, LANES), lambda i, j: (i, 0, 0)),
                  pl.BlockSpec((1, nc, LANES), lambda i, j: (i, 0, 0)),
                  full(2), full(3), full(0), full(1),
                  pl.BlockSpec((1, QBLK, LANES), lambda i, j: (i, j, 0)),
                  pl.BlockSpec((t, LANES), lambda i, j: (0, 0))],
        out_specs=pl.BlockSpec((1, QBLK, qw), lambda i, j: (i, j, 0)),
        out_shape=jax.ShapeDtypeStruct((b, t, qw), BF16),
        scratch_shapes=[pltpu.VMEM((2, NSA_SEL_CHUNK * QBLK, ncol), F32),
                        pltpu.VMEM((2, 1, ncol), F32),
                        pltpu.VMEM((1, ncol), F32), pltpu.VMEM((1, ncol), F32),
                        pltpu.VMEM((LANES, ncol), F32)],
        compiler_params=_cparams("arbitrary", "arbitrary"),
        name="nsa_prompt_attn",
    )(q_bf, kc, vc, rows_bf, rows_bf, win_bf, win_bf, gl, member)


def _rglru_gates(xc, wa_ref, ba_ref, wx_ref, bx_ref, lam_ref):
    xb = xc.astype(BF16)
    r = _sigmoid(_dot(xb, wa_ref[...]) + ba_ref[...])
    i = _sigmoid(_dot(xb, wx_ref[...]) + bx_ref[...])
    nl = -lam_ref[...]
    softplus = jnp.maximum(nl, 0.0) + jnp.log1p(jnp.exp(-jnp.abs(nl)))
    log_a = -RG_C * r * softplus
    a = jnp.exp(log_a)
    u = jnp.sqrt(-jnp.tanh(log_a) * (a * a + 1.0)) * (i * xc)
    return a, u


def _rglru_prompt_kernel(xr_ref, gr_ref, cw_ref, cb_ref, wa_ref, ba_ref, wx_ref, bx_ref, lam_ref,
                         o_ref, hl_ref, cl_ref, hcar, xcar):
    i = pl.program_id(1)
    tm, c = xr_ref.shape[1], xr_ref.shape[2]

    @pl.when(i == 0)
    def _():
        hcar[...] = jnp.zeros(hcar.shape, F32)
        xcar[...] = jnp.zeros(xcar.shape, F32)

    x = xr_ref[0]
    prev = xcar[...]
    row = lax.broadcasted_iota(jnp.int32, (tm, c), 0)
    xc = cb_ref[...] + cw_ref[3:4, :] * x
    for k in range(1, 4):
        cur = pltpu.roll(x, k, axis=0)
        old = jnp.tile(pltpu.roll(prev, k, axis=0), (tm // 8, 1))
        xc = xc + cw_ref[3 - k:4 - k, :] * jnp.where(row < k, old, cur)
    a, u = _rglru_gates(xc, wa_ref, ba_ref, wx_ref, bx_ref, lam_ref)

    s = 1
    while s < tm:
        a_sh = jnp.where(row >= s, pltpu.roll(a, s, axis=0), 1.0)
        u_sh = jnp.where(row >= s, pltpu.roll(u, s, axis=0), 0.0)
        u = a * u_sh + u
        a = a * a_sh
        s *= 2
    h = a * hcar[0:1, :] + u
    o_ref[0] = (h * _gelu(gr_ref[0])).astype(o_ref.dtype)
    hcar[...] = jnp.broadcast_to(h[tm - 1:tm, :], hcar.shape)
    xcar[...] = x[tm - 8:tm, :]
    hl_ref[0] = hcar[...]
    cl_ref[0] = xcar[...]


def rglru_prompt(xr, gr, rw, tm):
    b, t, c = xr.shape
    tm = min(tm, t)
    const = lambda a: pl.BlockSpec(a.shape, lambda i, j: (0,) * a.ndim)
    ws = (rw["conv_w"], rw["conv_b"], rw["wa"], rw["ba"], rw["wx"], rw["bx"], rw["lam"])
    tile = pl.BlockSpec((1, tm, c), lambda i, j: (i, j, 0))
    last = pl.BlockSpec((1, 8, c), lambda i, j: (i, 0, 0))
    return pl.pallas_call(
        _rglru_prompt_kernel,
        grid=(b, t // tm),
        in_specs=[tile, tile] + [const(a) for a in ws],
        out_specs=[tile, last, last],
        out_shape=[jax.ShapeDtypeStruct((b, t, c), BF16),
                   jax.ShapeDtypeStruct((b, 8, c), F32),
                   jax.ShapeDtypeStruct((b, 8, c), F32)],
        scratch_shapes=[pltpu.VMEM((8, c), F32), pltpu.VMEM((8, c), F32)],
        compiler_params=_cparams("arbitrary", "arbitrary"),
        name="rglru_prompt",
    )(xr, gr, *ws)


def _gmlp_v(v_raw, vg_ref):
    v = _gelu(v_raw)
    ms = jnp.mean(v * v, axis=-1, keepdims=True)
    return v * lax.rsqrt(ms + RMS_EPS) * vg_ref[...]


GMLP_CHUNKS = 4


def _gmlp_prompt_kernel(u_ref, v_ref, vg_ref, ws_ref, bs_ref, o_ref):
    lc = ws_ref.shape[1]
    tril = (lax.broadcasted_iota(jnp.int32, (lc, lc), 0) >= lax.broadcasted_iota(jnp.int32, (lc, lc), 1))
    lane = lax.broadcasted_iota(jnp.int32, (lc, LANES), 1)
    wts = [jnp.where(tril, ws_ref[g], 0.0).astype(BF16) for g in range(ws_ref.shape[0])]
    for ci in range(u_ref.shape[1] // lc):
        rows = slice(ci * lc, (ci + 1) * lc)
        v = _gmlp_v(v_ref[0, rows, :], vg_ref).astype(BF16)
        parts = []
        for j in range(v.shape[1] // LANES):
            vj = v[:, j * LANES:(j + 1) * LANES]
            parts.append(jnp.where(lane < HEAD_DIM, _dot(wts[2 * j], vj), _dot(wts[2 * j + 1], vj)))
        mixed = jnp.concatenate(parts, axis=1) + bs_ref[...]
        o_ref[0, rows, :] = (_gelu(u_ref[0, rows, :]) * mixed).astype(o_ref.dtype)


def gmlp_prompt(u, v, v_gain, ws, bs_exp):
    b, t, c = u.shape
    lc = ws.shape[1]
    rows = min(GMLP_CHUNKS * lc, t)
    tile = pl.BlockSpec((1, rows, c), lambda i, j: (i, j, 0))
    const = lambda a: pl.BlockSpec(a.shape, lambda i, j: (0,) * a.ndim)
    return pl.pallas_call(
        _gmlp_prompt_kernel,
        grid=(b, t // rows),
        in_specs=[tile, tile, const(v_gain), const(ws), const(bs_exp)],
        out_specs=tile,
        out_shape=jax.ShapeDtypeStruct((b, t, c), BF16),
        compiler_params=_cparams("arbitrary", "arbitrary"),
        name="gmlp_prompt",
    )(u, v, v_gain, ws, bs_exp)


DIL_SPAN = DIL_MAX
DIL_LOCKSTEP = 4


def _dil_prompt_kernel(q_ref, k_ref, v_ref, o_ref, acc_ref, m_ref, l_ref):
    span = q_ref.shape[1]
    base = pl.program_id(2) * span
    scale = HEAD_DIM ** -0.5
    lane = lax.broadcasted_iota(jnp.int32, (QBLK, LANES), 1)
    row = lax.broadcasted_iota(jnp.int32, (QBLK, LANES), 0)
    lo = lane < HEAD_DIM
    band1 = jnp.concatenate([jnp.where(lane >= row, 0.0, NEG_INF), jnp.where(lane <= row, 0.0, NEG_INF)], axis=1)
    band = jnp.concatenate([band1, band1], axis=0)
    prev_cols = lax.broadcasted_iota(jnp.int32, (2 * QBLK, 2 * QBLK), 1) < QBLK
    zero = jnp.zeros((), BF16)

    for p, (window, dil) in enumerate(DIL_PATTERNS):
        assert window // dil == QBLK and span % (QBLK * dil) == 0

        def tiles(step, carry, p=p, dil=dil):
            rows, q2, kcat, vcat, bias = [], [], [], [], []
            for u in range(DIL_LOCKSTEP):
                idx = step * DIL_LOCKSTEP + u
                start = (idx // dil) * (QBLK * dil) + idx % dil
                g0 = base + start
                has_prev = g0 >= QBLK * dil
                d_rows = pl.ds(g0, QBLK, stride=dil)
                p_rows = pl.ds(jnp.where(has_prev, g0 - QBLK * dil, g0), QBLK, stride=dil)
                rows.append(pl.ds(start, QBLK, stride=dil))
                q = (q_ref[0, rows[u], :] * scale).astype(BF16)
                q2.append(jnp.concatenate([jnp.where(lo, q, zero), jnp.where(lo, zero, q)], axis=0))
                kcat.append(jnp.concatenate([k_ref[0, p_rows, :], k_ref[0, d_rows, :]], axis=0).astype(BF16))
                vcat.append(jnp.concatenate([v_ref[0, p_rows, :], v_ref[0, d_rows, :]], axis=0).astype(BF16))
                bias.append(band + jnp.where(prev_cols, jnp.where(has_prev, 0.0, NEG_INF), 0.0))
            lock = range(DIL_LOCKSTEP)
            s = [_dot_nt(q2[u], kcat[u]) + bias[u] for u in lock]
            m = [jnp.max(jnp.maximum(s[u][:, :QBLK], s[u][:, QBLK:]), axis=-1, keepdims=True) for u in lock]
            e = [jnp.exp(s[u] - m[u]) for u in lock]
            l = [jnp.sum(e[u][:, :QBLK] + e[u][:, QBLK:], axis=-1, keepdims=True) for u in lock]
            acc = [_dot(e[u].astype(BF16), vcat[u]) for u in lock]
            for u in lock:
                acc_ref[p, rows[u], :] = jnp.where(lo, acc[u][:QBLK], acc[u][QBLK:])
                m_ref[p, rows[u], :] = jnp.where(lo, m[u][:QBLK], m[u][QBLK:])
                l_ref[p, rows[u], :] = jnp.where(lo, l[u][:QBLK], l[u][QBLK:])
            return carry

        lax.fori_loop(0, span // QBLK // DIL_LOCKSTEP, tiles, 0)

    def combine(c, carry):
        rows = pl.ds(pl.multiple_of(c * QBLK, QBLK), QBLK)
        ms = [m_ref[p, rows, :] for p in range(len(DIL_PATTERNS))]
        mx = functools.reduce(jnp.maximum, ms)
        num, den = 0.0, 0.0
        for p, mp in enumerate(ms):
            w = jnp.exp(mp - mx)
            num = num + w * acc_ref[p, rows, :]
            den = den + w * l_ref[p, rows, :]
        o_ref[0, rows, :] = (num * (1.0 / den)).astype(o_ref.dtype)
        return carry

    lax.fori_loop(0, span // QBLK, combine, 0)


def dil_prompt_attn(q, kv):
    b, t, w = q.shape
    npair = w // LANES
    span = min(DIL_SPAN, t)
    n_pat = len(DIL_PATTERNS)
    return pl.pallas_call(
        _dil_prompt_kernel,
        grid=(b, npair, t // span),
        in_specs=[pl.BlockSpec((1, span, LANES), lambda i, p, j: (i, j, p)),
                  pl.BlockSpec((1, t, LANES), lambda i, p, j: (i, 0, p)),
                  pl.BlockSpec((1, t, LANES), lambda i, p, j: (i, 0, npair + p))],
        out_specs=pl.BlockSpec((1, span, LANES), lambda i, p, j: (i, j, p)),
        out_shape=jax.ShapeDtypeStruct((b, t, w), BF16),
        scratch_shapes=[pltpu.VMEM((n_pat, span, LANES), F32)] * 3,
        compiler_params=_cparams("arbitrary", "arbitrary", "arbitrary"),
        name="dil_prompt_attn",
    )(q, kv, kv)


NSA_G, NSA_REP = 2, 4
NSA_QW = NSA_G * NSA_REP * HEAD_DIM
QPERM = np.arange(NSA_QW).reshape(NSA_G, NSA_REP, HEAD_DIM).transpose(1, 0, 2).reshape(-1)


def _block_diag(blocks):
    n, a, b = blocks.shape
    out = jnp.zeros((n, a, n, b), blocks.dtype)
    out = out.at[jnp.arange(n), :, jnp.arange(n), :].set(blocks)
    return out.reshape(n * a, n * b)


def prep_layer0(p):
    w_in = p["w_in"]
    d = w_in.shape[0]
    n_gate = NSA_G * NSA_REP * 3
    kv_w = 6 * NSA_G * HEAD_DIM
    c_gl = NSA_QW + kv_w
    c_xr = c_gl + n_gate
    d_rnn = (w_in.shape[1] - c_xr) // 2
    cols = np.concatenate([QPERM, np.arange(NSA_QW, c_gl), np.arange(c_xr, c_xr + 2 * d_rnn),
                           np.arange(c_gl, c_xr)])
    w = jnp.concatenate([w_in[:, cols], jnp.zeros((d, LANES - n_gate), w_in.dtype)], axis=1).astype(BF16)
    n = w.shape[1]
    hg = jnp.ones((n,), F32)
    hg = hg.at[0:NSA_QW].set(jnp.tile(p["q_gain"], NSA_QW // HEAD_DIM))
    hg = hg.at[NSA_QW + 2 * LANES:NSA_QW + 3 * LANES].set(jnp.tile(p["k_gain"][1], 2))
    hg = hg.at[NSA_QW + 4 * LANES:NSA_QW + 5 * LANES].set(jnp.tile(p["k_gain"][2], 2))
    qb = NSA_QW // LANES
    c_rows, c_win, c_x, c_g, c_l = NSA_QW, NSA_QW + 4 * LANES, NSA_QW + 6 * LANES, NSA_QW + 6 * LANES + d_rnn, \
        NSA_QW + 6 * LANES + 2 * d_rnn
    outs = ((0, NSA_QW, BF16), (c_rows, 4 * LANES, F32), (c_rows, 4 * LANES, BF16), (c_win, 2 * LANES, F32),
            (c_win, 2 * LANES, BF16), (c_x, d_rnn, F32), (c_g, d_rnn, F32), (c_l, LANES, F32))
    cw = {}
    for c, nm in enumerate("kv"):
        w1 = p["cmp_w1"][c]
        big = jnp.zeros((CMP_BLOCK, NSA_G, HEAD_DIM, NSA_G, HEAD_DIM), F32)
        for g in range(NSA_G):
            big = big.at[:, g, :, g, :].set(w1)
        cw["w1" + nm] = big.reshape(CMP_BLOCK * LANES, LANES).astype(BF16)
        cw["pos" + nm] = jnp.tile(p["cmp_pos"][c], (1, NSA_G)).reshape(1, CMP_BLOCK * LANES)
        cw["w2" + nm] = _block_diag(jnp.stack([p["cmp_w2"][c]] * NSA_G)).astype(BF16)
    cw["kg0"] = jnp.tile(p["k_gain"][0], 2).reshape(1, LANES)
    rw = dict(conv_w=p["conv_w"], conv_b=p["conv_b"].reshape(1, -1),
              wa=_block_diag(p["wa"]).astype(BF16), ba=p["ba"].reshape(1, -1),
              wx=_block_diag(p["wx"]).astype(BF16), bx=p["bx"].reshape(1, -1), lam=p["lam"].reshape(1, -1))
    w_out = p["w_out"]
    return dict(w=w, hg=hg.reshape(1, n), norm_blocks=tuple(range(qb)) + (qb + 2, qb + 4), outs=outs,
                cw=cw, rw=rw, wo_a=w_out[QPERM].astype(BF16), wo_b=w_out[NSA_QW:].astype(BF16))


def prep_layer1(p):
    w = p["w_in"].astype(BF16)
    n = w.shape[1]
    c_w = p["v_gain"].shape[0]
    dil_w = (n - 2 * c_w) // 3
    hg = jnp.ones((n,), F32)
    hg = hg.at[2 * c_w:2 * c_w + dil_w].set(jnp.tile(p["q_gain"], dil_w // HEAD_DIM))
    hg = hg.at[2 * c_w + dil_w:2 * c_w + 2 * dil_w].set(jnp.tile(p["k_gain"], dil_w // HEAD_DIM))
    b0 = 2 * c_w // LANES
    nbq = dil_w // LANES
    outs = ((0, c_w, F32), (c_w, c_w, F32), (2 * c_w, dil_w, F32), (2 * c_w + dil_w, 2 * dil_w, F32))
    w_out = p["w_out"]
    gw = c_w // p["ws"].shape[0]
    return dict(w=w, hg=hg.reshape(1, n), norm_blocks=tuple(range(b0, b0 + 2 * nbq)), outs=outs,
                v_gain=p["v_gain"].reshape(1, c_w), ws=p["ws"], bs_exp=jnp.repeat(p["bs"].T, gw, axis=1),
                ws_diag=jnp.repeat(p["ws"][:, 0, 0], gw).reshape(1, c_w),
                bs0=jnp.repeat(p["bs"][:, 0], gw).reshape(1, c_w),
                wo_a=w_out[:c_w].astype(BF16), wo_b=w_out[c_w:].astype(BF16))


def layer0_prompt(x, mod, ln_mix, ln_ffn, ffn_w, pp):
    b, t, _ = x.shape
    q_bf, rows, rows_bf, win, win_bf, xr, gr, gl = in_proj(
        x, mod, 0, 1, ln_mix, pp["w"], pp["hg"], pp["norm_blocks"], pp["outs"], tm=512)
    kc, vc = nsa_compress(rows, pp["cw"])
    o_nsa = nsa_prompt_attn(q_bf, kc, vc, rows_bf, win_bf, gl)
    o_rnn, h_last, conv_last = rglru_prompt(xr, gr, pp["rw"], tm=256)
    y = out_ffn(o_nsa, o_rnn, x, mod, ln_ffn, pp["wo_a"], pp["wo_b"], *ffn_w, tm=512)
    nwin = min(NSA_WINDOW, t)
    state = (rows.reshape(b, t, 4, NSA_G, HEAD_DIM), win[:, t - nwin:].reshape(b, nwin, 2, NSA_G, HEAD_DIM),
             h_last[:, 0], conv_last[:, 5:8])
    return y, state


def layer1_prompt(x, mod, ln_mix, ln_ffn, ffn_w, pp):
    b, t, _ = x.shape
    u, v, q, kv = in_proj(x, mod, 0, 1, ln_mix, pp["w"], pp["hg"], pp["norm_blocks"], pp["outs"], tm=512)
    o_c = gmlp_prompt(u, v, pp["v_gain"], pp["ws"], pp["bs_exp"])
    o_d = dil_prompt_attn(q, kv)
    y = out_ffn(o_c, o_d, x, mod, ln_ffn, pp["wo_a"], pp["wo_b"], *ffn_w, tm=512)
    nkv = min(DIL_MAX, t)
    heads = kv.shape[2] // (2 * HEAD_DIM)
    return y, kv[:, t - nkv:].reshape(b, nkv, 2, heads, HEAD_DIM)


def _row_softmax_parts(s_list, mask_list, s_new):
    m = s_new[0]
    for sn in s_new[1:]:
        m = jnp.maximum(m, sn)
    for s, mk in zip(s_list, mask_list):
        sm = s if mk is None else jnp.where(mk, s, NEG_INF)
        m = jnp.maximum(m, jnp.max(sm, axis=-1, keepdims=True))
    es, den = [], 0.0
    for s, mk in zip(s_list, mask_list):
        e = jnp.exp(s - m)
        if mk is not None:
            e = jnp.where(mk, e, 0.0)
        es.append(e)
        den = den + jnp.sum(e, axis=-1, keepdims=True)
    en = [jnp.exp(sn - m) for sn in s_new]
    for e in en:
        den = den + e
    return es, en, 1.0 / den


def _nsa_decode_kernel(pt_ref, cache_ref, q_ref, rn_ref, wn_ref, wnc_ref, gl_ref, sw_ref,
                       w1k_ref, posk_ref, w2k_ref, w1v_ref, posv_ref, w2v_ref, kg_ref, ex_ref,
                       o_ref, wo_ref, buf, xk, xv, sem):
    b = pl.program_id(0)
    nbatch = pl.num_programs(0)
    slot = b % 2
    n_pages = pt_ref.shape[1]
    page = cache_ref.shape[3]
    past = n_pages * page
    nc = past // CMP_BLOCK
    nb = past // SEL_BLOCK
    scale = HEAD_DIM ** -0.5
    rep = q_ref.shape[2] // LANES
    nh = 2 * rep

    nseq = q_ref.shape[0]

    def copies(step, sl):
        return [pltpu.make_async_copy(cache_ref.at[pt_ref[step * nseq + u, j]], buf.at[sl, u, j], sem.at[sl, u, j])
                for u in range(nseq) for j in range(n_pages)]

    @pl.when(b == 0)
    def _():
        for c in copies(0, 0):
            c.start()

    @pl.when(b + 1 < nbatch)
    def _():
        for c in copies(b + 1, 1 - slot):
            c.start()

    for c in copies(b, slot):
        c.wait()

    lane = lax.broadcasted_iota(jnp.int32, (nh, LANES), 1)
    hrow = lax.broadcasted_iota(jnp.int32, (nh, LANES), 0)
    blk_t = lax.broadcasted_iota(jnp.int32, (LANES, LANES), 0)
    wb = sw_ref.shape[3]
    last = lax.broadcasted_iota(jnp.int32, (LANES, wb), 1) == wb - 1

    def padded(c2):
        z = jnp.zeros((LANES - nc // 2, LANES), F32)
        return jnp.concatenate([c2[:nc // 2], z, c2[nc // 2:], z], axis=0)

    def one_sequence(u):
        q = q_ref[u]
        qrows = []
        for g in range(2):
            for r in range(rep):
                qrows.append(q[:, r * LANES:(r + 1) * LANES])
        qm = jnp.concatenate(qrows, axis=0)
        qm = jnp.where((lane // HEAD_DIM) == (hrow // rep), qm, jnp.zeros((), qm.dtype))
        qf = qm.astype(F32)

        kwt, vwt = sw_ref[u, 0], sw_ref[u, 1]
        wn = wn_ref[u]
        s_w = _dot(qm, kwt.astype(BF16)) * scale
        s_w_new = jnp.sum(qf * wn[:, :LANES], axis=-1, keepdims=True) * scale
        yield

        for j in range(n_pages):
            xk[u, pl.ds(j * page, page), :] = buf[slot, u, j, 0].T
            xv[u, pl.ds(j * page, page), :] = buf[slot, u, j, 1].T
            yield
        ldk = lambda s, n, st: xk[u, pl.ds(s, n, stride=st), :]
        ldv = lambda s, n, st: xv[u, pl.ds(s, n, stride=st), :]
        kc = _head_norm(_compress_rows(ldk, nc // 2, w1k_ref, posk_ref, w2k_ref), kg_ref[...])
        yield
        vc = _compress_rows(ldv, nc // 2, w1v_ref, posv_ref, w2v_ref)
        yield
        kcp = padded(kc).astype(BF16)
        vcp = padded(vc).astype(BF16)
        s_c = _dot_nt(qm, kcp) * scale
        yield
        cl = lax.broadcasted_iota(jnp.int32, (nh, 2 * LANES), 1)
        cmask = (cl % LANES) < nc // 2
        s_c = jnp.where(cmask, s_c, NEG_INF)
        e_c = jnp.where(cmask, jnp.exp(s_c - jnp.max(s_c, axis=-1, keepdims=True)), 0.0)
        yield
        p_c = e_c * (1.0 / jnp.maximum(jnp.sum(e_c, axis=-1, keepdims=True), 1e-30))
        o_cmp = _dot(p_c.astype(BF16), vcp)
        yield

        pp = p_c[:, :LANES] + p_c[:, LANES:]
        imps = []
        for g in range(2):
            ig = pp[g * rep:g * rep + 1]
            for r in range(1, rep):
                ig = ig + pp[g * rep + r:g * rep + r + 1]
            imps += [ig] * rep
        imp = jnp.concatenate(imps + [jnp.zeros((LANES - nh, LANES), F32)], axis=0)
        imp_t = imp.T
        imp_t = jnp.where((blk_t == 0) | (blk_t == nb), FORCE_SCORE, imp_t)
        imp_t = jnp.where(blk_t <= nb, imp_t, -3.0)
        yield

        rn = rn_ref[u]
        kst = jnp.concatenate([buf[slot, u, j, 2] for j in range(n_pages)], axis=1).astype(BF16)
        s_sel = _dot(qm, kst) * scale
        s_sel_new = jnp.sum(qf * rn[:, 2 * LANES:3 * LANES], axis=-1, keepdims=True) * scale
        yield
        (e_w,), (e_wn,), inv_w = _row_softmax_parts([s_w], [None], [s_w_new])
        o_win = (_dot_nt(e_w.astype(BF16), vwt.astype(BF16)) + e_wn * wn[:, LANES:]) * inv_w
        yield

        blk_f = blk_t.astype(F32)
        sel_t, work = jnp.zeros(imp_t.shape, F32), imp_t
        for _ in range(min(N_SEL, nb + 1)):
            top = jnp.max(work, axis=0, keepdims=True)
            pick = blk_f == jnp.min(jnp.where(work == top, blk_f, 1e9), axis=0, keepdims=True)
            sel_t = jnp.where(pick, 1.0, sel_t)
            work = jnp.where(pick, -2.0, work)
            yield
        selm = _dot(sel_t.T[:nh].astype(BF16), ex_ref[...]) > 0.5
        yield
        (e_s,), (e_sn,), inv_s = _row_softmax_parts([s_sel], [selm], [s_sel_new])
        yield
        vst = jnp.concatenate([buf[slot, u, j, 3] for j in range(n_pages)], axis=1).astype(BF16)
        o_sel = (_dot_nt(e_s.astype(BF16), vst) + e_sn * rn[:, 3 * LANES:4 * LANES]) * inv_s
        yield

        gate = _sigmoid(gl_ref[u])
        o = gate[:, 0:1] * o_cmp + gate[:, 1:2] * o_sel + gate[:, 2:3] * o_win
        l1 = lax.broadcasted_iota(jnp.int32, (1, LANES), 1)
        o_ref[u] = jnp.concatenate([jnp.where(l1 < HEAD_DIM, o[r:r + 1], o[rep + r:rep + r + 1])
                                    for r in range(rep)], axis=1).astype(o_ref.dtype)
        wo_ref[u, 0] = jnp.where(last, wnc_ref[u, 0], pltpu.roll(kwt, wb - 1, axis=1))
        wo_ref[u, 1] = jnp.where(last, wnc_ref[u, 1], pltpu.roll(vwt, wb - 1, axis=1))

    phases = [one_sequence(u) for u in range(nseq)]
    while phases:
        phases = [g for g in phases if next(g, True) is None]


def nsa_decode(page_table, cache, q_bf, rows_new, win_new, gl3, state_win, cw):
    b, n_pages = page_table.shape
    n_pool, page, ncomp, ng, hd = cache.shape
    past = n_pages * page
    nb = past // SEL_BLOCK
    assert past % SEL_BLOCK == 0 and nb < LANES and past // CMP_BLOCK <= 2 * LANES and ng * hd == LANES
    wb = state_win.shape[1]
    assert wb <= NSA_WINDOW
    cache_t = jnp.transpose(cache, (0, 2, 3, 4, 1)).reshape(n_pool, ncomp, LANES, page)
    win_t = jnp.transpose(state_win, (0, 2, 3, 4, 1)).reshape(b, 2, LANES, wb)
    win_col = win_new.reshape(b, 2, LANES, 1)
    expand = (jnp.arange(LANES)[:, None] == (jnp.arange(past) // SEL_BLOCK)[None, :]).astype(BF16)
    ws = (cw["w1k"], cw["posk"], cw["w2k"], cw["w1v"], cw["posv"], cw["w2v"], cw["kg0"], expand)
    ns = NSA_DECODE_SEQS
    assert b % ns == 0
    per_b = lambda a: pl.BlockSpec((ns,) + a.shape[1:], lambda i, pt: (i,) + (0,) * (a.ndim - 1))
    const = lambda a: pl.BlockSpec(a.shape, lambda i, pt: (0,) * a.ndim)
    grid_spec = pltpu.PrefetchScalarGridSpec(
        num_scalar_prefetch=1,
        grid=(b // ns,),
        in_specs=[pl.BlockSpec(memory_space=pl.ANY), per_b(q_bf), per_b(rows_new), per_b(win_new), per_b(win_col),
                  per_b(gl3), per_b(win_t)] + [const(a) for a in ws],
        out_specs=[pl.BlockSpec((ns, 1, q_bf.shape[2]), lambda i, pt: (i, 0, 0)),
                   pl.BlockSpec((ns, 2, LANES, wb), lambda i, pt: (i, 0, 0, 0))],
        scratch_shapes=[pltpu.VMEM((2, ns, n_pages, ncomp, LANES, page), F32), pltpu.VMEM((ns, past, LANES), F32),
                        pltpu.VMEM((ns, past, LANES), F32), pltpu.SemaphoreType.DMA((2, ns, n_pages))],
    )
    o, win_out_t = pl.pallas_call(
        _nsa_decode_kernel,
        grid_spec=grid_spec,
        out_shape=[jax.ShapeDtypeStruct((b, 1, q_bf.shape[2]), BF16),
                   jax.ShapeDtypeStruct(win_t.shape, F32)],
        compiler_params=_cparams("arbitrary"),
        name="nsa_decode",
    )(page_table, cache_t, q_bf, rows_new, win_new, win_col, gl3, win_t, *ws)
    return o, jnp.transpose(win_out_t.reshape(b, 2, ng, hd, wb), (0, 4, 1, 2, 3))


def _rglru_decode_kernel(xr_ref, gr_ref, cs_ref, h0_ref, cw_ref, cb_ref, wa_ref, ba_ref, wx_ref, bx_ref, lam_ref,
                         o_ref, h_ref, cn_ref):
    x = xr_ref[...]
    xc = cb_ref[...] + cw_ref[3:4, :] * x
    for k in range(3):
        xc = xc + cw_ref[k:k + 1, :] * cs_ref[k]
    a, u = _rglru_gates(xc, wa_ref, ba_ref, wx_ref, bx_ref, lam_ref)
    h = a * h0_ref[...] + u
    h_ref[...] = h
    o_ref[...] = (h * _gelu(gr_ref[...])).astype(o_ref.dtype)
    cn_ref[0] = cs_ref[1]
    cn_ref[1] = cs_ref[2]
    cn_ref[2] = x


def rglru_decode(xr, gr, conv_t, h0, rw):
    ws = (rw["conv_w"], rw["conv_b"], rw["wa"], rw["ba"], rw["wx"], rw["bx"], rw["lam"])
    return pl.pallas_call(
        _rglru_decode_kernel,
        out_shape=[jax.ShapeDtypeStruct(xr.shape, BF16), jax.ShapeDtypeStruct(xr.shape, F32),
                   jax.ShapeDtypeStruct(conv_t.shape, F32)],
        compiler_params=pltpu.CompilerParams(vmem_limit_bytes=VMEM_LIMIT),
        name="rglru_decode",
    )(xr, gr, conv_t, h0, *ws)


def _gmlp_decode_kernel(u_ref, v_ref, vg_ref, wd_ref, b0_ref, o_ref, vn_ref):
    v = _gmlp_v(v_ref[...], vg_ref)
    vn_ref[...] = v
    o_ref[...] = (_gelu(u_ref[...]) * (wd_ref[...] * v + b0_ref[...])).astype(o_ref.dtype)


def gmlp_decode(u, v, v_gain, ws_diag, bs0):
    return pl.pallas_call(
        _gmlp_decode_kernel,
        out_shape=[jax.ShapeDtypeStruct(u.shape, BF16), jax.ShapeDtypeStruct(u.shape, F32)],
        compiler_params=pltpu.CompilerParams(vmem_limit_bytes=VMEM_LIMIT),
        name="gmlp_decode",
    )(u, v, v_gain, ws_diag, bs0)


def _dil_decode_kernel(q_ref, kvn_ref, st_ref, o_ref, so_ref):
    nh, hd, wb = st_ref.shape[2], st_ref.shape[3], st_ref.shape[4]
    scale = HEAD_DIM ** -0.5
    n_pat = len(DIL_PATTERNS)
    lane1 = lax.broadcasted_iota(jnp.int32, (1, wb), 1)
    dist = wb - lane1
    cnt = jnp.zeros((1, wb), F32)
    for window, dil in DIL_PATTERNS:
        cnt = cnt + jnp.where((dist <= window) & ((dist & (dil - 1)) == 0), 1.0, 0.0)
    last = lax.broadcasted_iota(jnp.int32, (hd, wb), 1) == wb - 1

    def shifted(x, new_col):
        return jnp.where(last, new_col, pltpu.roll(x, wb - 1, axis=1))

    s_rows, s_new_rows = [], []
    for h in range(nh):
        kh = st_ref[0, 0, h]
        qh = q_ref[0, h] * scale
        kn = kvn_ref[0, 0, h]
        s_rows.append(jnp.sum(kh * qh, axis=0, keepdims=True))
        s_new_rows.append(jnp.sum(kn * qh, axis=0, keepdims=True))
        so_ref[0, 0, h] = shifted(kh, kn)
    s = jnp.concatenate(s_rows, axis=0)
    s_new = jnp.concatenate(s_new_rows, axis=0)
    s = jnp.where(cnt > 0.0, s, NEG_INF)
    m = jnp.maximum(jnp.max(s, axis=-1, keepdims=True), s_new)
    e = cnt * jnp.exp(s - m)
    e_new = n_pat * jnp.exp(s_new - m)
    inv = 1.0 / (jnp.sum(e, axis=-1, keepdims=True) + e_new)
    for h in range(nh):
        vh = st_ref[0, 1, h]
        vn = kvn_ref[0, 1, h]
        acc = jnp.sum(vh * e[h:h + 1, :], axis=1, keepdims=True) + e_new[h:h + 1, :] * vn
        o_ref[0, h] = acc * inv[h:h + 1, :]
        so_ref[0, 1, h] = shifted(vh, vn)


def dil_decode(q, kv_new, state):
    b, wb, _, nh, hd = state.shape
    assert wb == DIL_MAX
    st_t = jnp.transpose(state, (0, 2, 3, 4, 1))
    o, so_t = pl.pallas_call(
        _dil_decode_kernel,
        grid=(b,),
        in_specs=[pl.BlockSpec((1, nh, hd, 1), lambda i: (i, 0, 0, 0)),
                  pl.BlockSpec((1, 2, nh, hd, 1), lambda i: (i, 0, 0, 0, 0)),
                  pl.BlockSpec((1, 2, nh, hd, wb), lambda i: (i, 0, 0, 0, 0))],
        out_specs=[pl.BlockSpec((1, nh, hd, 1), lambda i: (i, 0, 0, 0)),
                   pl.BlockSpec((1, 2, nh, hd, wb), lambda i: (i, 0, 0, 0, 0))],
        out_shape=[jax.ShapeDtypeStruct((b, nh, hd, 1), F32), jax.ShapeDtypeStruct(st_t.shape, F32)],
        compiler_params=_cparams("arbitrary"),
        name="dil_decode",
    )(q[..., None], kv_new[..., None], st_t)
    return o[..., 0], jnp.transpose(so_t, (0, 4, 1, 2, 3))


def layer0_sample(x, mod, ln_mix, ln_ffn, ffn_w, pp, cache, page_table, state_win, state_h, state_conv):
    b, _, d = x.shape
    xs = x.reshape(1, b, d)
    q_bf, rows, _, win, _, xr, gr, gl = in_proj(
        xs, mod, 0, 1, ln_mix, pp["w"], pp["hg"], pp["norm_blocks"], pp["outs"], tm=b)
    n_gate = NSA_G * NSA_REP * 3
    gl3 = gl[0, :, :n_gate].reshape(b, NSA_G * NSA_REP, 3)
    o_nsa, win_out = nsa_decode(page_table, cache, q_bf.reshape(b, 1, -1), rows.reshape(b, 1, -1),
                                win.reshape(b, 1, -1), gl3, state_win, pp["cw"])
    o_rnn, h_new, conv_new = rglru_decode(xr[0], gr[0], state_conv.transpose(1, 0, 2), state_h, pp["rw"])
    y = out_ffn(o_nsa.reshape(1, b, -1), o_rnn[None], xs, mod, ln_ffn, pp["wo_a"], pp["wo_b"], *ffn_w, tm=b)
    state = (rows.reshape(b, 1, 4, NSA_G, HEAD_DIM), win_out, h_new, conv_new.transpose(1, 0, 2))
    return y.reshape(b, 1, d), state


def layer1_sample(x, mod, ln_mix, ln_ffn, ffn_w, pp, state_dil):
    b, _, d = x.shape
    xs = x.reshape(1, b, d)
    u, v, q, kv = in_proj(xs, mod, 0, 1, ln_mix, pp["w"], pp["hg"], pp["norm_blocks"], pp["outs"], tm=b)
    o_c, v_n = gmlp_decode(u[0], v[0], pp["v_gain"], pp["ws_diag"], pp["bs0"])
    heads = state_dil.shape[3]
    o_d, dil_out = dil_decode(q.reshape(b, heads, HEAD_DIM), kv.reshape(b, 2, heads, HEAD_DIM), state_dil)
    y = out_ffn(o_c[None], o_d.reshape(1, b, -1).astype(BF16), xs, mod, ln_ffn, pp["wo_a"], pp["wo_b"], *ffn_w,
                tm=b)
    return y.reshape(b, 1, d), (dil_out, v_n.reshape(b, 1, -1))


def kernel(x_prompt, x_sample, cache_nsa_kv, state_nsa_win, state_rglru_h, state_rglru_conv, state_dil_kv, page_table, c_prompt, c_sample, norm_mix_g, norm_ffn_g, w_ada, b_ada, w_ffn_gate, w_ffn_up, w_ffn_down, w_in_ab, w_out_ab, nsa_q_gain, nsa_k_gain, nsa_cmp_w1, nsa_cmp_w2, nsa_cmp_pos, rg_conv_w, rg_conv_b, rg_wa, rg_ba, rg_wx, rg_bx, rg_lambda, w_in_cd, w_out_cd, gmlp_v_gain, gmlp_ws, gmlp_bs, dil_q_gain, dil_k_gain):
    depth = norm_mix_g.shape[0]
    bp, bs = x_prompt.shape[0], x_sample.shape[0]
    pad = -(bp + bs) % 8
    c_all = jnp.concatenate([c_prompt, c_sample, jnp.zeros((pad, c_prompt.shape[1]), F32)], axis=0)
    mod_all = ada_mod(c_all, w_ada.astype(BF16), b_ada)
    yp, ys = x_prompt, x_sample
    kv_p, kv_s, win_p, win_s, h_p, h_s, conv_p, conv_s, dil_p, dil_s, gv_s = ([] for _ in range(11))
    for layer in range(depth):
        i = layer // 2
        mod_p = mod_all[layer, :bp, None, :]
        mod_s = mod_all[layer, None, bp:bp + bs, :]
        ffn_w = (w_ffn_gate[layer].astype(BF16), w_ffn_up[layer].astype(BF16), w_ffn_down[layer].astype(BF16))
        if layer % 2 == 0:
            pp = prep_layer0(dict(w_in=w_in_ab[i], w_out=w_out_ab[i], q_gain=nsa_q_gain[i], k_gain=nsa_k_gain[i],
                                  cmp_w1=nsa_cmp_w1[i], cmp_w2=nsa_cmp_w2[i], cmp_pos=nsa_cmp_pos[i],
                                  conv_w=rg_conv_w[i], conv_b=rg_conv_b[i], wa=rg_wa[i], ba=rg_ba[i],
                                  wx=rg_wx[i], bx=rg_bx[i], lam=rg_lambda[i]))
            yp, st = layer0_prompt(yp, mod_p, norm_mix_g[layer], norm_ffn_g[layer], ffn_w, pp)
            kv_p.append(st[0]); win_p.append(st[1]); h_p.append(st[2]); conv_p.append(st[3])
            ys, st = layer0_sample(ys, mod_s, norm_mix_g[layer], norm_ffn_g[layer], ffn_w, pp, cache_nsa_kv[i],
                                   page_table, state_nsa_win[i], state_rglru_h[i], state_rglru_conv[i])
            kv_s.append(st[0]); win_s.append(st[1]); h_s.append(st[2]); conv_s.append(st[3])
        else:
            pp = prep_layer1(dict(w_in=w_in_cd[i], w_out=w_out_cd[i], v_gain=gmlp_v_gain[i], ws=gmlp_ws[i],
                                  bs=gmlp_bs[i], q_gain=dil_q_gain[i], k_gain=dil_k_gain[i]))
            yp, st = layer1_prompt(yp, mod_p, norm_mix_g[layer], norm_ffn_g[layer], ffn_w, pp)
            dil_p.append(st)
            ys, st = layer1_sample(ys, mod_s, norm_mix_g[layer], norm_ffn_g[layer], ffn_w, pp, state_dil_kv[i])
            dil_s.append(st[0]); gv_s.append(st[1])
    return (yp, ys, jnp.stack(kv_p), jnp.stack(kv_s), jnp.stack(win_p), jnp.stack(win_s),
            jnp.stack(h_p), jnp.stack(h_s), jnp.stack(conv_p), jnp.stack(conv_s),
            jnp.stack(dil_p), jnp.stack(dil_s), jnp.stack(gv_s))
```

```python
import functools

import numpy as np
import jax
import jax.numpy as jnp
from jax import lax
from jax.experimental import pallas as pl
from jax.experimental.pallas import tpu as pltpu

F32 = jnp.float32
BF16 = jnp.bfloat16

LANES = 128
HEAD_DIM = 64
QBLK = 128
CMP_BLOCK = 32
SEL_BLOCK = 64
N_SEL = 16
NSA_WINDOW = 512
FORCE_SCORE = 1.0e4
DIL_PATTERNS = ((128, 1), (512, 4), (2048, 16))
DIL_MAX = 2048
RG_C = 8.0
RMS_EPS = 1e-6
NEG_INF = -1e30
LOG2_E = 1.4426950408889634
VMEM_LIMIT = 56 * 1024 * 1024


def _cparams(*sem):
    return pltpu.CompilerParams(dimension_semantics=sem, vmem_limit_bytes=VMEM_LIMIT)


def _dot(a, b):
    return jnp.dot(a, b, preferred_element_type=F32)


def _dot_nt(a, b):
    return lax.dot_general(a, b, (((1,), (1,)), ((), ())), preferred_element_type=F32)


def _gelu(x):
    return 0.5 * x * (1.0 + jnp.tanh(np.sqrt(2.0 / np.pi) * (x + 0.044715 * (x * x * x))))


def _sigmoid(x):
    return 1.0 / (1.0 + jnp.exp(-x))


def _head_norm(z, gain):
    lo = lax.broadcasted_iota(jnp.int32, z.shape, 1) < HEAD_DIM
    z2 = z * z
    s_lo = jnp.sum(jnp.where(lo, z2, 0.0), axis=-1, keepdims=True)
    s_hi = jnp.sum(jnp.where(lo, 0.0, z2), axis=-1, keepdims=True)
    inv = lax.rsqrt(jnp.where(lo, s_lo, s_hi) * (1.0 / HEAD_DIM) + RMS_EPS)
    return z * inv * gain


def _ada_kernel(c_ref, w_ref, b_ref, o_ref):
    c = c_ref[...]
    s = c * _sigmoid(c)
    o_ref[0] = _dot(s.astype(BF16), w_ref[0]) + b_ref[0]


def ada_mod(c_all, w_ada, b_ada):
    m, d = c_all.shape
    nl, _, n = w_ada.shape
    tn = 1536
    return pl.pallas_call(
        _ada_kernel,
        grid=(nl, n // tn),
        in_specs=[pl.BlockSpec((m, d), lambda l, j: (0, 0)),
                  pl.BlockSpec((1, d, tn), lambda l, j: (l, 0, j)),
                  pl.BlockSpec((1, 1, tn), lambda l, j: (l, 0, j))],
        out_specs=pl.BlockSpec((1, m, tn), lambda l, j: (l, 0, j)),
        out_shape=jax.ShapeDtypeStruct((nl, m, n), F32),
        compiler_params=_cparams("arbitrary", "arbitrary"),
        name="ada_mod",
    )(c_all, w_ada, b_ada.reshape(nl, 1, n))


def _in_proj_kernel(x_ref, sc_ref, sh_ref, g_ref, w_ref, hg_ref, *out_refs, norm_blocks, outs, touts):
    i = pl.program_id(1)
    x = x_ref[0]
    ms = jnp.mean(x * x, axis=-1, keepdims=True)
    h = x * lax.rsqrt(ms + RMS_EPS) * g_ref[...]
    h = h * (1.0 + sc_ref[0]) + sh_ref[0]
    z = _dot(h.astype(BF16), w_ref[...])
    nblk = z.shape[1] // LANES
    blocks = []
    for j in range(nblk):
        zb = z[:, j * LANES:(j + 1) * LANES]
        if j in norm_blocks:
            zb = _head_norm(zb, hg_ref[:, j * LANES:(j + 1) * LANES])
        blocks.append(zb)
    for o_ref, (c0, width, _) in zip(out_refs, outs):
        for j in range(width // LANES):
            o_ref[0, :, j * LANES:(j + 1) * LANES] = blocks[c0 // LANES + j].astype(o_ref.dtype)
    for o_ref, (c0, width, first) in zip(out_refs[len(outs):], touts):
        @pl.when(i >= first)
        def _(o_ref=o_ref, c0=c0, width=width):
            for j in range(width // LANES):
                o_ref[0, j * LANES:(j + 1) * LANES, :] = blocks[c0 // LANES + j].T


def in_proj(x, mod, sh_idx, sc_idx, g, w, head_gain, norm_blocks, outs, tm, touts=()):
    bm, t, d = x.shape
    r = mod.shape[1]
    n = w.shape[1]
    tm = min(tm, t)
    if r == 1:
        mod_spec = lambda k: pl.BlockSpec((1, 1, d), lambda b, i: (b, 0, k))
    else:
        assert r == t and tm == t
        mod_spec = lambda k: pl.BlockSpec((1, tm, d), lambda b, i: (b, 0, k))
    tfirst = []
    for (_, _, npos) in touts:
        assert npos % tm == 0 and npos <= t
        tfirst.append((t - npos) // tm)
    kern = functools.partial(_in_proj_kernel, norm_blocks=tuple(norm_blocks), outs=tuple(outs),
                             touts=tuple((c0, wd, f) for (c0, wd, _), f in zip(touts, tfirst)))
    return pl.pallas_call(
        kern,
        grid=(bm, t // tm),
        in_specs=[pl.BlockSpec((1, tm, d), lambda b, i: (b, i, 0)),
                  mod_spec(sc_idx), mod_spec(sh_idx),
                  pl.BlockSpec((1, d), lambda b, i: (0, 0)),
                  pl.BlockSpec((d, n), lambda b, i: (0, 0)),
                  pl.BlockSpec((1, n), lambda b, i: (0, 0))],
        out_specs=[pl.BlockSpec((1, tm, wd), lambda b, i: (b, i, 0)) for (_, wd, _) in outs]
        + [pl.BlockSpec((1, wd, tm), lambda b, i, f=f: (b, 0, jnp.maximum(i - f, 0)))
           for (_, wd, _), f in zip(touts, tfirst)],
        out_shape=[jax.ShapeDtypeStruct((bm, t, wd), dt) for (_, wd, dt) in outs]
        + [jax.ShapeDtypeStruct((bm, wd, npos), F32) for (_, wd, npos) in touts],
        compiler_params=_cparams("arbitrary", "arbitrary"),
        name="in_proj",
    )(x, mod, mod, g.reshape(1, d), w, head_gain)


def _out_ffn_kernel(ma_ref, mb_ref, x_ref, gm_ref, shf_ref, scf_ref, gf_ref, lnf_ref,
                    woa_ref, wob_ref, wg_ref, wu_ref, wd_ref, o_ref, *, hidden_chunk):
    mix = _dot(ma_ref[0], woa_ref[...]) + _dot(mb_ref[0], wob_ref[...])
    x1 = x_ref[0] + gm_ref[0] * mix
    ms = jnp.mean(x1 * x1, axis=-1, keepdims=True)
    hf = x1 * lax.rsqrt(ms + RMS_EPS) * lnf_ref[...]
    hf = (hf * (1.0 + scf_ref[0]) + shf_ref[0]).astype(BF16)
    hidden = wg_ref.shape[1]
    ffn = jnp.zeros(x1.shape, F32)
    for c0 in range(0, hidden, hidden_chunk):
        gt = _dot(hf, wg_ref[:, c0:c0 + hidden_chunk])
        up = _dot(hf, wu_ref[:, c0:c0 + hidden_chunk])
        act = (gt * _sigmoid(gt) * up).astype(BF16)
        ffn = ffn + _dot(act, wd_ref[c0:c0 + hidden_chunk, :])
    o_ref[0] = x1 + gf_ref[0] * ffn


def out_ffn(mix_a, mix_b, x, mod, ln_ffn, wo_a, wo_b, wg, wu, wd, tm):
    bm, t, d = x.shape
    r = mod.shape[1]
    tm = min(tm, t)
    ka, kb = mix_a.shape[2], mix_b.shape[2]
    hidden = wg.shape[1]
    if r == 1:
        mod_spec = lambda k: pl.BlockSpec((1, 1, d), lambda b, i: (b, 0, k))
    else:
        assert r == t and tm == t
        mod_spec = lambda k: pl.BlockSpec((1, tm, d), lambda b, i: (b, 0, k))
    const = lambda shape: pl.BlockSpec(shape, lambda b, i: (0,) * len(shape), pipeline_mode=pl.Buffered(1))
    kern = functools.partial(_out_ffn_kernel, hidden_chunk=hidden // 2)
    return pl.pallas_call(
        kern,
        grid=(bm, t // tm),
        in_specs=[pl.BlockSpec((1, tm, ka), lambda b, i: (b, i, 0)),
                  pl.BlockSpec((1, tm, kb), lambda b, i: (b, i, 0)),
                  pl.BlockSpec((1, tm, d), lambda b, i: (b, i, 0)),
                  mod_spec(2), mod_spec(3), mod_spec(4), mod_spec(5),
                  const((1, d)), const((ka, d)), const((kb, d)),
                  const((d, hidden)), const((d, hidden)), const((hidden, d))],
        out_specs=pl.BlockSpec((1, tm, d), lambda b, i: (b, i, 0)),
        out_shape=jax.ShapeDtypeStruct((bm, t, d), F32),
        compiler_params=_cparams("arbitrary", "arbitrary"),
        name="out_ffn",
    )(mix_a, mix_b, x, mod, mod, mod, mod, ln_ffn.reshape(1, d), wo_a, wo_b, wg, wu, wd)


def _compress_rows(load, nh, w1_ref, pos_ref, w2_ref):
    halves = []
    for parity in range(2):
        halves.append(jnp.concatenate(
            [load(parity * CMP_BLOCK + l, nh, 2 * CMP_BLOCK) for l in range(CMP_BLOCK)], axis=1))
    xs = (jnp.concatenate(halves, axis=0) + pos_ref[...]).astype(BF16)
    hid = _gelu(_dot(xs, w1_ref[...]))
    return _dot(hid.astype(BF16), w2_ref[...])


def _nsa_compress_kernel(rk_ref, rv_ref, w1k_ref, posk_ref, w2k_ref, w1v_ref, posv_ref, w2v_ref, kg_ref,
                         kc_ref, vc_ref):
    nh = kc_ref.shape[1] // 2
    ldk = lambda s, n, st: rk_ref[0, pl.ds(s, n, stride=st), :]
    ldv = lambda s, n, st: rv_ref[0, pl.ds(s, n, stride=st), :]
    kc = _compress_rows(ldk, nh, w1k_ref, posk_ref, w2k_ref)
    kc_ref[0] = _head_norm(kc, kg_ref[...]).astype(kc_ref.dtype)
    vc_ref[0] = _compress_rows(ldv, nh, w1v_ref, posv_ref, w2v_ref).astype(vc_ref.dtype)


def nsa_compress(rows, cw):
    b, t, _ = rows.shape
    nc = t // CMP_BLOCK
    const = lambda a: pl.BlockSpec(a.shape, lambda i: (0,) * a.ndim)
    ws = (cw["w1k"], cw["posk"], cw["w2k"], cw["w1v"], cw["posv"], cw["w2v"], cw["kg0"])
    return pl.pallas_call(
        _nsa_compress_kernel,
        grid=(b,),
        in_specs=[pl.BlockSpec((1, t, LANES), lambda i: (i, 0, 0)),
                  pl.BlockSpec((1, t, LANES), lambda i: (i, 0, 1))] + [const(a) for a in ws],
        out_specs=[pl.BlockSpec((1, nc, LANES), lambda i: (i, 0, 0))] * 2,
        out_shape=[jax.ShapeDtypeStruct((b, nc, LANES), BF16)] * 2,
        compiler_params=_cparams("arbitrary"),
        name="nsa_compress",
    )(rows, rows, *ws)


PICKED = -2.0


def _pick_top(work, blk_f):
    top = jnp.max(work, axis=0, keepdims=True)
    first = jnp.min(jnp.where(work == top, blk_f, 1e9), axis=0, keepdims=True)
    return jnp.where(blk_f == first, PICKED, work)


def _select_blocks_t(imp, blk_f, n_pick):
    work = imp
    for _ in range(n_pick):
        work = _pick_top(work, blk_f)
    return work == PICKED


NSA_SEL_CHUNK = 8
NSA_SUB = 256
NSA_DECODE_SEQS = 4

def _dot_tn(a, b):
    return lax.dot_general(a, b, (((0,), (0,)), ((), ())), preferred_element_type=F32)


def _nsa_prompt_kernel(q_ref, kc_ref, vc_ref, ks_ref, vs_ref, kw_ref, vw_ref, gl_ref, e_ref, o_ref,
                       s_ref, mx_ref, m_ref, l_ref, acc_ref):
    qi = pl.program_id(1)
    t0 = qi * QBLK
    nc = kc_ref.shape[1]
    nb = nc // 2
    scale = HEAD_DIM ** -0.5
    rep = q_ref.shape[2] // LANES
    nhead = 2 * rep
    ncol = nhead * QBLK

    lane = lax.broadcasted_iota(jnp.int32, (QBLK, LANES), 1)
    sub = lax.broadcasted_iota(jnp.int32, (QBLK, LANES), 0)
    tile8 = lambda x: jnp.concatenate([x] * nhead, axis=1)
    q = q_ref[0]
    qs = []
    for g in range(2):
        gm = (lane >= g * HEAD_DIM) & (lane < (g + 1) * HEAD_DIM)
        for r in range(rep):
            qs.append(jnp.where(gm, q[:, r * LANES:(r + 1) * LANES], jnp.zeros((), q.dtype)))
    qall = (jnp.concatenate(qs, axis=0).astype(F32) * (scale * LOG2_E)).astype(BF16)

    crow = lax.broadcasted_iota(jnp.int32, (nc, QBLK), 0)
    cidx = jnp.where(crow < nb, 2 * crow, 2 * (crow - nb) + 1)
    cvalid = ((cidx + 1) * CMP_BLOCK - 1) <= t0 + lax.broadcasted_iota(jnp.int32, (nc, QBLK), 1)
    cbias = tile8(jnp.where(cvalid, 0.0, NEG_INF))
    cone = tile8(jnp.where(cvalid, 1.0, 0.0))
    wkeys = NSA_WINDOW + QBLK
    wstart = jnp.maximum(qi - NSA_WINDOW // QBLK, 0) * QBLK
    dist = (t0 + lax.broadcasted_iota(jnp.int32, (wkeys, QBLK), 1)) \
        - (wstart + lax.broadcasted_iota(jnp.int32, (wkeys, QBLK), 0))
    wbias = tile8(jnp.where((dist >= 0) & (dist <= NSA_WINDOW), 0.0, NEG_INF))
    wrows = pl.ds(pl.multiple_of(wstart, QBLK), wkeys)
    sc = _dot_nt(kc_ref[0], qall) + cbias
    sw = _dot_nt(kw_ref[0, wrows, :], qall) + wbias
    mc = jnp.max(sc, axis=0, keepdims=True)
    mw = jnp.max(sw, axis=0, keepdims=True)
    ec = jnp.exp2(sc - mc) * cone
    ew = jnp.exp2(sw - mw)
    p = ec * (1.0 / jnp.maximum(jnp.sum(ec, axis=0, keepdims=True), 1e-30))
    lw = jnp.sum(ew, axis=0, keepdims=True)
    o_cmp = _dot_tn(vc_ref[0], p.astype(BF16))
    o_win = _dot_tn(vw_ref[0, wrows, :], ew.astype(BF16)) * (1.0 / lw)

    brow = lax.broadcasted_iota(jnp.int32, (nb, 2 * QBLK), 0)
    tb = t0 + lax.broadcasted_iota(jnp.int32, (nb, 2 * QBLK), 1) % QBLK
    imps = []
    for g in range(2):
        ps = p[:, (g * rep) * QBLK:(g * rep + 1) * QBLK]
        for r in range(1, rep):
            ps = ps + p[:, (g * rep + r) * QBLK:(g * rep + r + 1) * QBLK]
        imps.append(ps[:nb] + ps[nb:])
    imp = jnp.concatenate(imps, axis=1)
    imp = jnp.where((brow == 0) | (brow == tb // SEL_BLOCK), FORCE_SCORE, imp)
    imp = jnp.where(brow * SEL_BLOCK <= tb, imp, -1.0)
    sel = _select_blocks_t(imp, brow.astype(F32), min(N_SEL, nb))
    bias_t = jnp.where(sel, 0.0, NEG_INF)
    if nb < LANES:
        bias_t = jnp.concatenate([bias_t, jnp.zeros((LANES - nb, 2 * QBLK), F32)], axis=0)
    qbias = []
    for g in range(2):
        qbias += [bias_t[:, g * QBLK:(g + 1) * QBLK].T.astype(BF16)] * rep
    qaug = jnp.concatenate([qall, jnp.concatenate(qbias, axis=0)], axis=1)

    chunk = NSA_SEL_CHUNK * QBLK
    nsub = chunk // NSA_SUB
    sub_rows = lambda kc, j: pl.ds(pl.multiple_of(kc * chunk + j * NSA_SUB, NSA_SUB), NSA_SUB)
    neg_row = jnp.full((1, ncol), NEG_INF, F32)

    def scores(kc, j):
        r = sub_rows(kc, j)
        return _dot_nt(jnp.concatenate([ks_ref[0, r, :], e_ref[r, :]], axis=1), qaug)

    def rescale(mx):
        m_old = m_ref[...]
        m_new = jnp.maximum(m_old, mx)
        alpha = jnp.exp2(m_old - m_new)
        m_ref[...] = m_new
        l_ref[...] = alpha * l_ref[...]
        acc_ref[...] = alpha * acc_ref[...]
        return m_new

    def consume(s, v, m_new):
        e = jnp.exp2(s - m_new)
        l_ref[...] += jnp.sum(e, axis=0, keepdims=True)
        acc_ref[...] += _dot_tn(v, e.astype(BF16))

    def stage(kc, cur, nxt):
        m_new = rescale(mx_ref[cur])
        mx = neg_row
        for j in range(nsub):
            s_next = scores(kc + 1, j)
            s_ref[nxt, j * NSA_SUB:(j + 1) * NSA_SUB, :] = s_next
            mx = jnp.maximum(mx, jnp.max(s_next, axis=0, keepdims=True))
            consume(s_ref[cur, j * NSA_SUB:(j + 1) * NSA_SUB, :], vs_ref[0, sub_rows(kc, j), :], m_new)
        mx_ref[nxt] = mx

    m_ref[...] = neg_row
    l_ref[...] = jnp.zeros((1, ncol), F32)
    acc_ref[...] = jnp.zeros((LANES, ncol), F32)
    mx = neg_row
    for j in range(nsub):
        s0 = scores(0, j)
        s_ref[0, j * NSA_SUB:(j + 1) * NSA_SUB, :] = s0
        mx = jnp.maximum(mx, jnp.max(s0, axis=0, keepdims=True))
    mx_ref[0] = mx

    last = qi // NSA_SEL_CHUNK

    def stage_pair(pair, carry):
        stage(2 * pair, 0, 1)
        stage(2 * pair + 1, 1, 0)
        return carry

    lax.fori_loop(0, last // 2, stage_pair, 0)

    @pl.when(last % 2 == 1)
    def _():
        stage(last - 1, 0, 1)

    slot = last % 2
    kpos = last * chunk + lax.broadcasted_iota(jnp.int32, (chunk, QBLK), 0)
    causal = jnp.where(kpos <= t0 + lax.broadcasted_iota(jnp.int32, (chunk, QBLK), 1), 0.0, NEG_INF)
    s_last = [s_ref[slot, j * NSA_SUB:(j + 1) * NSA_SUB, :] + tile8(causal[j * NSA_SUB:(j + 1) * NSA_SUB])
              for j in range(nsub)]
    mx = neg_row
    for s in s_last:
        mx = jnp.maximum(mx, jnp.max(s, axis=0, keepdims=True))
    m_new = rescale(mx)
    q_end = (qi % NSA_SEL_CHUNK + 1) * QBLK
    consume(s_last[0], vs_ref[0, sub_rows(last, 0), :], m_new)
    for j in range(1, nsub):
        @pl.when(j * NSA_SUB < q_end)
        def _(j=j):
            consume(s_last[j], vs_ref[0, sub_rows(last, j), :], m_new)
    o_sel = acc_ref[...] * (1.0 / l_ref[...])

    gate = _sigmoid(gl_ref[0]).T
    for r in range(rep):
        og = []
        for g in range(2):
            c = (g * rep + r) * 3
            cols = slice((g * rep + r) * QBLK, (g * rep + r + 1) * QBLK)
            og.append(gate[c:c + 1] * o_cmp[:, cols] + gate[c + 1:c + 2] * o_sel[:, cols]
                      + gate[c + 2:c + 3] * o_win[:, cols])
        o_ref[0, :, r * LANES:(r + 1) * LANES] = jnp.where(sub < HEAD_DIM, og[0], og[1]).T.astype(o_ref.dtype)


def nsa_prompt_attn(q_bf, kc, vc, rows_bf, win_bf, gl):
    b, t, qw = q_bf.shape
    nc = kc.shape[1]
    nq = t // QBLK
    assert t % (NSA_SEL_CHUNK * QBLK) == 0 and t >= NSA_WINDOW + QBLK and t // SEL_BLOCK <= LANES
    ncol = 2 * (qw // LANES) * QBLK
    full = lambda k: pl.BlockSpec((1, t, LANES), lambda i, j: (i, 0, k))
    member = (jnp.arange(t)[:, None] // SEL_BLOCK == jnp.arange(LANES)[None, :]).astype(BF16)
    return pl.pallas_call(
        _nsa_prompt_kernel,
        grid=(b, nq),
        in_specs=[pl.BlockSpec((1, QBLK, qw), lambda i, j: (i, j, 0)),
                  pl.BlockSpec((1, nc, LANES), lambda i, j: (i, 0, 0)),
                  pl.BlockSpec((1, nc, LANES), lambda i, j: (i, 0, 0)),
                  full(2), full(3), full(0), full(1),
                  pl.BlockSpec((1, QBLK, LANES), lambda i, j: (i, j, 0)),
                  pl.BlockSpec((t, LANES), lambda i, j: (0, 0))],
        out_specs=pl.BlockSpec((1, QBLK, qw), lambda i, j: (i, j, 0)),
        out_shape=jax.ShapeDtypeStruct((b, t, qw), BF16),
        scratch_shapes=[pltpu.VMEM((2, NSA_SEL_CHUNK * QBLK, ncol), F32),
                        pltpu.VMEM((2, 1, ncol), F32),
                        pltpu.VMEM((1, ncol), F32), pltpu.VMEM((1, ncol), F32),
                        pltpu.VMEM((LANES, ncol), F32)],
        compiler_params=_cparams("arbitrary", "arbitrary"),
        name="nsa_prompt_attn",
    )(q_bf, kc, vc, rows_bf, rows_bf, win_bf, win_bf, gl, member)


def _rglru_gates(xc, wa_ref, ba_ref, wx_ref, bx_ref, lam_ref):
    xb = xc.astype(BF16)
    r = _sigmoid(_dot(xb, wa_ref[...]) + ba_ref[...])
    i = _sigmoid(_dot(xb, wx_ref[...]) + bx_ref[...])
    nl = -lam_ref[...]
    softplus = jnp.maximum(nl, 0.0) + jnp.log1p(jnp.exp(-jnp.abs(nl)))
    log_a = -RG_C * r * softplus
    a = jnp.exp(log_a)
    u = jnp.sqrt(-jnp.tanh(log_a) * (a * a + 1.0)) * (i * xc)
    return a, u


def _rglru_prompt_kernel(xr_ref, gr_ref, cw_ref, cb_ref, wa_ref, ba_ref, wx_ref, bx_ref, lam_ref,
                         o_ref, hl_ref, cl_ref, hcar, xcar):
    i = pl.program_id(1)
    tm, c = xr_ref.shape[1], xr_ref.shape[2]

    @pl.when(i == 0)
    def _():
        hcar[...] = jnp.zeros(hcar.shape, F32)
        xcar[...] = jnp.zeros(xcar.shape, F32)

    x = xr_ref[0]
    prev = xcar[...]
    row = lax.broadcasted_iota(jnp.int32, (tm, c), 0)
    xc = cb_ref[...] + cw_ref[3:4, :] * x
    for k in range(1, 4):
        cur = pltpu.roll(x, k, axis=0)
        old = jnp.tile(pltpu.roll(prev, k, axis=0), (tm // 8, 1))
        xc = xc + cw_ref[3 - k:4 - k, :] * jnp.where(row < k, old, cur)
    a, u = _rglru_gates(xc, wa_ref, ba_ref, wx_ref, bx_ref, lam_ref)

    s = 1
    while s < 8:
        keep = (row % 8) >= s
        a_sh = jnp.where(keep, pltpu.roll(a, s, axis=0), 1.0)
        u_sh = jnp.where(keep, pltpu.roll(u, s, axis=0), 0.0)
        u = a * u_sh + u
        a = a * a_sh
        s *= 2
    carry = hcar[0:1, :]
    groups = []
    for g in range(tm // 8):
        hg = a[g * 8:(g + 1) * 8] * carry + u[g * 8:(g + 1) * 8]
        groups.append(hg)
        carry = hg[7:8, :]
    h = jnp.concatenate(groups, axis=0)
    o_ref[0] = (h * _gelu(gr_ref[0])).astype(o_ref.dtype)
    hcar[...] = jnp.broadcast_to(carry, hcar.shape)
    xcar[...] = x[tm - 8:tm, :]
    hl_ref[0] = hcar[...]
    cl_ref[0] = xcar[...]


def rglru_prompt(xr, gr, rw, tm):
    b, t, c = xr.shape
    tm = min(tm, t)
    const = lambda a: pl.BlockSpec(a.shape, lambda i, j: (0,) * a.ndim)
    ws = (rw["conv_w"], rw["conv_b"], rw["wa"], rw["ba"], rw["wx"], rw["bx"], rw["lam"])
    tile = pl.BlockSpec((1, tm, c), lambda i, j: (i, j, 0))
    last = pl.BlockSpec((1, 8, c), lambda i, j: (i, 0, 0))
    return pl.pallas_call(
        _rglru_prompt_kernel,
        grid=(b, t // tm),
        in_specs=[tile, tile] + [const(a) for a in ws],
        out_specs=[tile, last, last],
        out_shape=[jax.ShapeDtypeStruct((b, t, c), BF16),
                   jax.ShapeDtypeStruct((b, 8, c), F32),
                   jax.ShapeDtypeStruct((b, 8, c), F32)],
        scratch_shapes=[pltpu.VMEM((8, c), F32), pltpu.VMEM((8, c), F32)],
        compiler_params=_cparams("arbitrary", "arbitrary"),
        name="rglru_prompt",
    )(xr, gr, *ws)


def _gmlp_v(v_raw, vg_ref):
    v = _gelu(v_raw)
    ms = jnp.mean(v * v, axis=-1, keepdims=True)
    return v * lax.rsqrt(ms + RMS_EPS) * vg_ref[...]


GMLP_CHUNKS = 4


def _gmlp_prompt_kernel(u_ref, v_ref, vg_ref, ws_ref, bs_ref, o_ref):
    lc = ws_ref.shape[1]
    tril = (lax.broadcasted_iota(jnp.int32, (lc, lc), 0) >= lax.broadcasted_iota(jnp.int32, (lc, lc), 1))
    lane = lax.broadcasted_iota(jnp.int32, (lc, LANES), 1)
    wts = [jnp.where(tril, ws_ref[g], 0.0).astype(BF16) for g in range(ws_ref.shape[0])]
    for ci in range(u_ref.shape[1] // lc):
        rows = slice(ci * lc, (ci + 1) * lc)
        v = _gmlp_v(v_ref[0, rows, :], vg_ref).astype(BF16)
        parts = []
        for j in range(v.shape[1] // LANES):
            vj = v[:, j * LANES:(j + 1) * LANES]
            parts.append(jnp.where(lane < HEAD_DIM, _dot(wts[2 * j], vj), _dot(wts[2 * j + 1], vj)))
        mixed = jnp.concatenate(parts, axis=1) + bs_ref[...]
        o_ref[0, rows, :] = (_gelu(u_ref[0, rows, :]) * mixed).astype(o_ref.dtype)


def gmlp_prompt(u, v, v_gain, ws, bs_exp):
    b, t, c = u.shape
    lc = ws.shape[1]
    rows = min(GMLP_CHUNKS * lc, t)
    tile = pl.BlockSpec((1, rows, c), lambda i, j: (i, j, 0))
    const = lambda a: pl.BlockSpec(a.shape, lambda i, j: (0,) * a.ndim)
    return pl.pallas_call(
        _gmlp_prompt_kernel,
        grid=(b, t // rows),
        in_specs=[tile, tile, const(v_gain), const(ws), const(bs_exp)],
        out_specs=tile,
        out_shape=jax.ShapeDtypeStruct((b, t, c), BF16),
        compiler_params=_cparams("arbitrary", "arbitrary"),
        name="gmlp_prompt",
    )(u, v, v_gain, ws, bs_exp)


DIL_SPAN = DIL_MAX
DIL_LOCKSTEP = 4


def _dil_prompt_kernel(q_ref, k_ref, v_ref, o_ref, acc_ref, m_ref, l_ref):
    span = q_ref.shape[1]
    base = pl.program_id(2) * span
    scale = HEAD_DIM ** -0.5
    lane = lax.broadcasted_iota(jnp.int32, (QBLK, LANES), 1)
    row = lax.broadcasted_iota(jnp.int32, (QBLK, LANES), 0)
    lo = lane < HEAD_DIM
    band1 = jnp.concatenate([jnp.where(lane >= row, 0.0, NEG_INF), jnp.where(lane <= row, 0.0, NEG_INF)], axis=1)
    band = jnp.concatenate([band1, band1], axis=0)
    prev_cols = lax.broadcasted_iota(jnp.int32, (2 * QBLK, 2 * QBLK), 1) < QBLK
    zero = jnp.zeros((), BF16)

    for p, (window, dil) in enumerate(DIL_PATTERNS):
        assert window // dil == QBLK and span % (QBLK * dil) == 0

        def tiles(step, carry, p=p, dil=dil):
            rows, q2, kcat, vcat, bias = [], [], [], [], []
            for u in range(DIL_LOCKSTEP):
                idx = step * DIL_LOCKSTEP + u
                start = (idx // dil) * (QBLK * dil) + idx % dil
                g0 = base + start
                has_prev = g0 >= QBLK * dil
                d_rows = pl.ds(g0, QBLK, stride=dil)
                p_rows = pl.ds(jnp.where(has_prev, g0 - QBLK * dil, g0), QBLK, stride=dil)
                rows.append(pl.ds(start, QBLK, stride=dil))
                q = (q_ref[0, rows[u], :] * scale).astype(BF16)
                q2.append(jnp.concatenate([jnp.where(lo, q, zero), jnp.where(lo, zero, q)], axis=0))
                kcat.append(jnp.concatenate([k_ref[0, p_rows, :], k_ref[0, d_rows, :]], axis=0).astype(BF16))
                vcat.append(jnp.concatenate([v_ref[0, p_rows, :], v_ref[0, d_rows, :]], axis=0).astype(BF16))
                bias.append(band + jnp.where(prev_cols, jnp.where(has_prev, 0.0, NEG_INF), 0.0))
            lock = range(DIL_LOCKSTEP)
            s = [_dot_nt(q2[u], kcat[u]) + bias[u] for u in lock]
            m = [jnp.max(jnp.maximum(s[u][:, :QBLK], s[u][:, QBLK:]), axis=-1, keepdims=True) for u in lock]
            e = [jnp.exp(s[u] - m[u]) for u in lock]
            l = [jnp.sum(e[u][:, :QBLK] + e[u][:, QBLK:], axis=-1, keepdims=True) for u in lock]
            acc = [_dot(e[u].astype(BF16), vcat[u]) for u in lock]
            for u in lock:
                acc_ref[p, rows[u], :] = jnp.where(lo, acc[u][:QBLK], acc[u][QBLK:])
                m_ref[p, rows[u], :] = jnp.where(lo, m[u][:QBLK], m[u][QBLK:])
                l_ref[p, rows[u], :] = jnp.where(lo, l[u][:QBLK], l[u][QBLK:])
            return carry

        lax.fori_loop(0, span // QBLK // DIL_LOCKSTEP, tiles, 0)

    def combine(c, carry):
        rows = pl.ds(pl.multiple_of(c * QBLK, QBLK), QBLK)
        ms = [m_ref[p, rows, :] for p in range(len(DIL_PATTERNS))]
        mx = functools.reduce(jnp.maximum, ms)
        num, den = 0.0, 0.0
        for p, mp in enumerate(ms):
            w = jnp.exp(mp - mx)
            num = num + w * acc_ref[p, rows, :]
            den = den + w * l_ref[p, rows, :]
        o_ref[0, rows, :] = (num * (1.0 / den)).astype(o_ref.dtype)
        return carry

    lax.fori_loop(0, span // QBLK, combine, 0)


def dil_prompt_attn(q, kv):
    b, t, w = q.shape
    npair = w // LANES
    span = min(DIL_SPAN, t)
    n_pat = len(DIL_PATTERNS)
    return pl.pallas_call(
        _dil_prompt_kernel,
        grid=(b, npair, t // span),
        in_specs=[pl.BlockSpec((1, span, LANES), lambda i, p, j: (i, j, p)),
                  pl.BlockSpec((1, t, LANES), lambda i, p, j: (i, 0, p)),
                  pl.BlockSpec((1, t, LANES), lambda i, p, j: (i, 0, npair + p))],
        out_specs=pl.BlockSpec((1, span, LANES), lambda i, p, j: (i, j, p)),
        out_shape=jax.ShapeDtypeStruct((b, t, w), BF16),
        scratch_shapes=[pltpu.VMEM((n_pat, span, LANES), F32)] * 3,
        compiler_params=_cparams("arbitrary", "arbitrary", "arbitrary"),
        name="dil_prompt_attn",
    )(q, kv, kv)


NSA_G, NSA_REP = 2, 4
NSA_QW = NSA_G * NSA_REP * HEAD_DIM
QPERM = np.arange(NSA_QW).reshape(NSA_G, NSA_REP, HEAD_DIM).transpose(1, 0, 2).reshape(-1)


def _block_diag(blocks):
    n, a, b = blocks.shape
    out = jnp.zeros((n, a, n, b), blocks.dtype)
    out = out.at[jnp.arange(n), :, jnp.arange(n), :].set(blocks)
    return out.reshape(n * a, n * b)


def prep_layer0(p):
    w_in = p["w_in"]
    d = w_in.shape[0]
    n_gate = NSA_G * NSA_REP * 3
    kv_w = 6 * NSA_G * HEAD_DIM
    c_gl = NSA_QW + kv_w
    c_xr = c_gl + n_gate
    d_rnn = (w_in.shape[1] - c_xr) // 2
    cols = np.concatenate([QPERM, np.arange(NSA_QW, c_gl), np.arange(c_xr, c_xr + 2 * d_rnn),
                           np.arange(c_gl, c_xr)])
    w = jnp.concatenate([w_in[:, cols], jnp.zeros((d, LANES - n_gate), w_in.dtype)], axis=1).astype(BF16)
    n = w.shape[1]
    hg = jnp.ones((n,), F32)
    hg = hg.at[0:NSA_QW].set(jnp.tile(p["q_gain"], NSA_QW // HEAD_DIM))
    hg = hg.at[NSA_QW + 2 * LANES:NSA_QW + 3 * LANES].set(jnp.tile(p["k_gain"][1], 2))
    hg = hg.at[NSA_QW + 4 * LANES:NSA_QW + 5 * LANES].set(jnp.tile(p["k_gain"][2], 2))
    qb = NSA_QW // LANES
    c_rows, c_win, c_x, c_g, c_l = NSA_QW, NSA_QW + 4 * LANES, NSA_QW + 6 * LANES, NSA_QW + 6 * LANES + d_rnn, \
        NSA_QW + 6 * LANES + 2 * d_rnn
    outs = ((0, NSA_QW, BF16), (c_rows, 4 * LANES, F32), (c_rows, 4 * LANES, BF16), (c_win, 2 * LANES, F32),
            (c_win, 2 * LANES, BF16), (c_x, d_rnn, F32), (c_g, d_rnn, F32), (c_l, LANES, F32))
    cw = {}
    for c, nm in enumerate("kv"):
        w1 = p["cmp_w1"][c]
        big = jnp.zeros((CMP_BLOCK, NSA_G, HEAD_DIM, NSA_G, HEAD_DIM), F32)
        for g in range(NSA_G):
            big = big.at[:, g, :, g, :].set(w1)
        cw["w1" + nm] = big.reshape(CMP_BLOCK * LANES, LANES).astype(BF16)
        cw["pos" + nm] = jnp.tile(p["cmp_pos"][c], (1, NSA_G)).reshape(1, CMP_BLOCK * LANES)
        cw["w2" + nm] = _block_diag(jnp.stack([p["cmp_w2"][c]] * NSA_G)).astype(BF16)
    cw["kg0"] = jnp.tile(p["k_gain"][0], 2).reshape(1, LANES)
    rw = dict(conv_w=p["conv_w"], conv_b=p["conv_b"].reshape(1, -1),
              wa=_block_diag(p["wa"]).astype(BF16), ba=p["ba"].reshape(1, -1),
              wx=_block_diag(p["wx"]).astype(BF16), bx=p["bx"].reshape(1, -1), lam=p["lam"].reshape(1, -1))
    w_out = p["w_out"]
    return dict(w=w, hg=hg.reshape(1, n), norm_blocks=tuple(range(qb)) + (qb + 2, qb + 4), outs=outs,
                cw=cw, rw=rw, wo_a=w_out[QPERM].astype(BF16), wo_b=w_out[NSA_QW:].astype(BF16))


def prep_layer1(p):
    w = p["w_in"].astype(BF16)
    n = w.shape[1]
    c_w = p["v_gain"].shape[0]
    dil_w = (n - 2 * c_w) // 3
    hg = jnp.ones((n,), F32)
    hg = hg.at[2 * c_w:2 * c_w + dil_w].set(jnp.tile(p["q_gain"], dil_w // HEAD_DIM))
    hg = hg.at[2 * c_w + dil_w:2 * c_w + 2 * dil_w].set(jnp.tile(p["k_gain"], dil_w // HEAD_DIM))
    b0 = 2 * c_w // LANES
    nbq = dil_w // LANES
    outs = ((0, c_w, F32), (c_w, c_w, F32), (2 * c_w, dil_w, F32), (2 * c_w + dil_w, 2 * dil_w, F32))
    w_out = p["w_out"]
    gw = c_w // p["ws"].shape[0]
    return dict(w=w, hg=hg.reshape(1, n), norm_blocks=tuple(range(b0, b0 + 2 * nbq)), outs=outs,
                v_gain=p["v_gain"].reshape(1, c_w), ws=p["ws"], bs_exp=jnp.repeat(p["bs"].T, gw, axis=1),
                ws_diag=jnp.repeat(p["ws"][:, 0, 0], gw).reshape(1, c_w),
                bs0=jnp.repeat(p["bs"][:, 0], gw).reshape(1, c_w),
                wo_a=w_out[:c_w].astype(BF16), wo_b=w_out[c_w:].astype(BF16))


def layer0_prompt(x, mod, ln_mix, ln_ffn, ffn_w, pp):
    b, t, _ = x.shape
    nwin = min(NSA_WINDOW, t)
    rows_spec, win_spec = pp["outs"][1], pp["outs"][3]
    q_bf, rows, rows_bf, _, win_bf, xr, gr, gl, rows_t, win_t = in_proj(
        x, mod, 0, 1, ln_mix, pp["w"], pp["hg"], pp["norm_blocks"], pp["outs"], tm=512,
        touts=((rows_spec[0], rows_spec[1], t), (win_spec[0], win_spec[1], nwin)))
    kc, vc = nsa_compress(rows, pp["cw"])
    o_nsa = nsa_prompt_attn(q_bf, kc, vc, rows_bf, win_bf, gl)
    o_rnn, h_last, conv_last = rglru_prompt(xr, gr, pp["rw"], tm=256)
    y = out_ffn(o_nsa, o_rnn, x, mod, ln_ffn, pp["wo_a"], pp["wo_b"], *ffn_w, tm=512)
    state = (jnp.transpose(rows_t.reshape(b, 4, NSA_G, HEAD_DIM, t), (0, 4, 1, 2, 3)),
             jnp.transpose(win_t.reshape(b, 2, NSA_G, HEAD_DIM, nwin), (0, 4, 1, 2, 3)),
             h_last[:, 0], conv_last[:, 5:8])
    return y, state


def layer1_prompt(x, mod, ln_mix, ln_ffn, ffn_w, pp):
    b, t, _ = x.shape
    nkv = min(DIL_MAX, t)
    kv_spec = pp["outs"][3]
    u, v, q, kv, kv_t = in_proj(x, mod, 0, 1, ln_mix, pp["w"], pp["hg"], pp["norm_blocks"], pp["outs"], tm=512,
                                touts=((kv_spec[0], kv_spec[1], nkv),))
    o_c = gmlp_prompt(u, v, pp["v_gain"], pp["ws"], pp["bs_exp"])
    o_d = dil_prompt_attn(q, kv)
    y = out_ffn(o_c, o_d, x, mod, ln_ffn, pp["wo_a"], pp["wo_b"], *ffn_w, tm=512)
    heads = kv.shape[2] // (2 * HEAD_DIM)
    return y, jnp.transpose(kv_t.reshape(b, 2, heads, HEAD_DIM, nkv), (0, 4, 1, 2, 3))


def _row_softmax_parts(s_list, mask_list, s_new):
    m = s_new[0]
    for sn in s_new[1:]:
        m = jnp.maximum(m, sn)
    for s, mk in zip(s_list, mask_list):
        sm = s if mk is None else jnp.where(mk, s, NEG_INF)
        m = jnp.maximum(m, jnp.max(sm, axis=-1, keepdims=True))
    es, den = [], 0.0
    for s, mk in zip(s_list, mask_list):
        e = jnp.exp(s - m)
        if mk is not None:
            e = jnp.where(mk, e, 0.0)
        es.append(e)
        den = den + jnp.sum(e, axis=-1, keepdims=True)
    en = [jnp.exp(sn - m) for sn in s_new]
    for e in en:
        den = den + e
    return es, en, 1.0 / den


def _nsa_decode_kernel(pt_ref, cache_ref, q_ref, rn_ref, wn_ref, wnc_ref, gl_ref, sw_ref,
                       w1k_ref, posk_ref, w2k_ref, w1v_ref, posv_ref, w2v_ref, kg_ref, ex_ref,
                       o_ref, wo_ref, buf, xk, xv, sem):
    b = pl.program_id(0)
    nbatch = pl.num_programs(0)
    slot = b % 2
    n_pages = pt_ref.shape[1]
    page = cache_ref.shape[3]
    past = n_pages * page
    nc = past // CMP_BLOCK
    nb = past // SEL_BLOCK
    scale = HEAD_DIM ** -0.5
    rep = q_ref.shape[2] // LANES
    nh = 2 * rep

    nseq = q_ref.shape[0]

    def copies(step, sl):
        return [pltpu.make_async_copy(cache_ref.at[pt_ref[step * nseq + u, j]], buf.at[sl, u, j], sem.at[sl, u, j])
                for u in range(nseq) for j in range(n_pages)]

    @pl.when(b == 0)
    def _():
        for c in copies(0, 0):
            c.start()

    @pl.when(b + 1 < nbatch)
    def _():
        for c in copies(b + 1, 1 - slot):
            c.start()

    for c in copies(b, slot):
        c.wait()

    lane = lax.broadcasted_iota(jnp.int32, (nh, LANES), 1)
    hrow = lax.broadcasted_iota(jnp.int32, (nh, LANES), 0)
    blk_t = lax.broadcasted_iota(jnp.int32, (LANES, LANES), 0)
    wb = sw_ref.shape[3]
    last = lax.broadcasted_iota(jnp.int32, (LANES, wb), 1) == wb - 1

    def padded(c2):
        z = jnp.zeros((LANES - nc // 2, LANES), F32)
        return jnp.concatenate([c2[:nc // 2], z, c2[nc // 2:], z], axis=0)

    def one_sequence(u):
        q = q_ref[u]
        qrows = []
        for g in range(2):
            for r in range(rep):
                qrows.append(q[:, r * LANES:(r + 1) * LANES])
        qm = jnp.concatenate(qrows, axis=0)
        qm = jnp.where((lane // HEAD_DIM) == (hrow // rep), qm, jnp.zeros((), qm.dtype))
        qf = qm.astype(F32)

        kwt, vwt = sw_ref[u, 0], sw_ref[u, 1]
        wn = wn_ref[u]
        s_w = _dot(qm, kwt.astype(BF16)) * scale
        s_w_new = jnp.sum(qf * wn[:, :LANES], axis=-1, keepdims=True) * scale
        yield

        for j in range(n_pages):
            xk[u, pl.ds(j * page, page), :] = buf[slot, u, j, 0].T
            xv[u, pl.ds(j * page, page), :] = buf[slot, u, j, 1].T
            yield
        ldk = lambda s, n, st: xk[u, pl.ds(s, n, stride=st), :]
        ldv = lambda s, n, st: xv[u, pl.ds(s, n, stride=st), :]
        kc = _head_norm(_compress_rows(ldk, nc // 2, w1k_ref, posk_ref, w2k_ref), kg_ref[...])
        yield
        vc = _compress_rows(ldv, nc // 2, w1v_ref, posv_ref, w2v_ref)
        yield
        kcp = padded(kc).astype(BF16)
        vcp = padded(vc).astype(BF16)
        s_c = _dot_nt(qm, kcp) * scale
        yield
        cl = lax.broadcasted_iota(jnp.int32, (nh, 2 * LANES), 1)
        cmask = (cl % LANES) < nc // 2
        s_c = jnp.where(cmask, s_c, NEG_INF)
        e_c = jnp.where(cmask, jnp.exp(s_c - jnp.max(s_c, axis=-1, keepdims=True)), 0.0)
        yield
        p_c = e_c * (1.0 / jnp.maximum(jnp.sum(e_c, axis=-1, keepdims=True), 1e-30))
        o_cmp = _dot(p_c.astype(BF16), vcp)
        yield

        pp = p_c[:, :LANES] + p_c[:, LANES:]
        imps = []
        for g in range(2):
            ig = pp[g * rep:g * rep + 1]
            for r in range(1, rep):
                ig = ig + pp[g * rep + r:g * rep + r + 1]
            imps += [ig] * rep
        imp = jnp.concatenate(imps + [jnp.zeros((LANES - nh, LANES), F32)], axis=0)
        imp_t = imp.T
        imp_t = jnp.where((blk_t == 0) | (blk_t == nb), FORCE_SCORE, imp_t)
        imp_t = jnp.where(blk_t <= nb, imp_t, -3.0)
        yield

        rn = rn_ref[u]
        kst = jnp.concatenate([buf[slot, u, j, 2] for j in range(n_pages)], axis=1).astype(BF16)
        s_sel = _dot(qm, kst) * scale
        s_sel_new = jnp.sum(qf * rn[:, 2 * LANES:3 * LANES], axis=-1, keepdims=True) * scale
        yield
        (e_w,), (e_wn,), inv_w = _row_softmax_parts([s_w], [None], [s_w_new])
        o_win = (_dot_nt(e_w.astype(BF16), vwt.astype(BF16)) + e_wn * wn[:, LANES:]) * inv_w
        yield

        work = imp_t
        for _ in range(min(N_SEL, nb + 1)):
            work = _pick_top(work, blk_t.astype(F32))
            yield
        sel_t = jnp.where(work == PICKED, 1.0, 0.0)
        selm = _dot(sel_t.T[:nh].astype(BF16), ex_ref[...]) > 0.5
        yield
        (e_s,), (e_sn,), inv_s = _row_softmax_parts([s_sel], [selm], [s_sel_new])
        yield
        vst = jnp.concatenate([buf[slot, u, j, 3] for j in range(n_pages)], axis=1).astype(BF16)
        o_sel = (_dot_nt(e_s.astype(BF16), vst) + e_sn * rn[:, 3 * LANES:4 * LANES]) * inv_s
        yield

        gate = _sigmoid(gl_ref[u])
        o = gate[:, 0:1] * o_cmp + gate[:, 1:2] * o_sel + gate[:, 2:3] * o_win
        l1 = lax.broadcasted_iota(jnp.int32, (1, LANES), 1)
        o_ref[u] = jnp.concatenate([jnp.where(l1 < HEAD_DIM, o[r:r + 1], o[rep + r:rep + r + 1])
                                    for r in range(rep)], axis=1).astype(o_ref.dtype)
        wo_ref[u, 0] = jnp.where(last, wnc_ref[u, 0], pltpu.roll(kwt, wb - 1, axis=1))
        wo_ref[u, 1] = jnp.where(last, wnc_ref[u, 1], pltpu.roll(vwt, wb - 1, axis=1))

    phases = [one_sequence(u) for u in range(nseq)]
    while phases:
        phases = [g for g in phases if next(g, True) is None]


def nsa_decode(page_table, cache, q_bf, rows_new, win_new, gl3, state_win, cw):
    b, n_pages = page_table.shape
    n_pool, page, ncomp, ng, hd = cache.shape
    past = n_pages * page
    nb = past // SEL_BLOCK
    assert past % SEL_BLOCK == 0 and nb < LANES and past // CMP_BLOCK <= 2 * LANES and ng * hd == LANES
    wb = state_win.shape[1]
    assert wb <= NSA_WINDOW
    cache_t = jnp.transpose(cache, (0, 2, 3, 4, 1)).reshape(n_pool, ncomp, LANES, page)
    win_t = jnp.transpose(state_win, (0, 2, 3, 4, 1)).reshape(b, 2, LANES, wb)
    win_col = win_new.reshape(b, 2, LANES, 1)
    expand = (jnp.arange(LANES)[:, None] == (jnp.arange(past) // SEL_BLOCK)[None, :]).astype(BF16)
    ws = (cw["w1k"], cw["posk"], cw["w2k"], cw["w1v"], cw["posv"], cw["w2v"], cw["kg0"], expand)
    ns = NSA_DECODE_SEQS
    assert b % ns == 0
    per_b = lambda a: pl.BlockSpec((ns,) + a.shape[1:], lambda i, pt: (i,) + (0,) * (a.ndim - 1))
    const = lambda a: pl.BlockSpec(a.shape, lambda i, pt: (0,) * a.ndim)
    grid_spec = pltpu.PrefetchScalarGridSpec(
        num_scalar_prefetch=1,
        grid=(b // ns,),
        in_specs=[pl.BlockSpec(memory_space=pl.ANY), per_b(q_bf), per_b(rows_new), per_b(win_new), per_b(win_col),
                  per_b(gl3), per_b(win_t)] + [const(a) for a in ws],
        out_specs=[pl.BlockSpec((ns, 1, q_bf.shape[2]), lambda i, pt: (i, 0, 0)),
                   pl.BlockSpec((ns, 2, LANES, wb), lambda i, pt: (i, 0, 0, 0))],
        scratch_shapes=[pltpu.VMEM((2, ns, n_pages, ncomp, LANES, page), F32), pltpu.VMEM((ns, past, LANES), F32),
                        pltpu.VMEM((ns, past, LANES), F32), pltpu.SemaphoreType.DMA((2, ns, n_pages))],
    )
    o, win_out_t = pl.pallas_call(
        _nsa_decode_kernel,
        grid_spec=grid_spec,
        out_shape=[jax.ShapeDtypeStruct((b, 1, q_bf.shape[2]), BF16),
                   jax.ShapeDtypeStruct(win_t.shape, F32)],
        compiler_params=_cparams("arbitrary"),
        name="nsa_decode",
    )(page_table, cache_t, q_bf, rows_new, win_new, win_col, gl3, win_t, *ws)
    return o, jnp.transpose(win_out_t.reshape(b, 2, ng, hd, wb), (0, 4, 1, 2, 3))


def _rglru_decode_kernel(xr_ref, gr_ref, cs_ref, h0_ref, cw_ref, cb_ref, wa_ref, ba_ref, wx_ref, bx_ref, lam_ref,
                         o_ref, h_ref, cn_ref):
    x = xr_ref[...]
    xc = cb_ref[...] + cw_ref[3:4, :] * x
    for k in range(3):
        xc = xc + cw_ref[k:k + 1, :] * cs_ref[k]
    a, u = _rglru_gates(xc, wa_ref, ba_ref, wx_ref, bx_ref, lam_ref)
    h = a * h0_ref[...] + u
    h_ref[...] = h
    o_ref[...] = (h * _gelu(gr_ref[...])).astype(o_ref.dtype)
    cn_ref[0] = cs_ref[1]
    cn_ref[1] = cs_ref[2]
    cn_ref[2] = x


def rglru_decode(xr, gr, conv_t, h0, rw):
    ws = (rw["conv_w"], rw["conv_b"], rw["wa"], rw["ba"], rw["wx"], rw["bx"], rw["lam"])
    return pl.pallas_call(
        _rglru_decode_kernel,
        out_shape=[jax.ShapeDtypeStruct(xr.shape, BF16), jax.ShapeDtypeStruct(xr.shape, F32),
                   jax.ShapeDtypeStruct(conv_t.shape, F32)],
        compiler_params=pltpu.CompilerParams(vmem_limit_bytes=VMEM_LIMIT),
        name="rglru_decode",
    )(xr, gr, conv_t, h0, *ws)


def _gmlp_decode_kernel(u_ref, v_ref, vg_ref, wd_ref, b0_ref, o_ref, vn_ref):
    v = _gmlp_v(v_ref[...], vg_ref)
    vn_ref[...] = v
    o_ref[...] = (_gelu(u_ref[...]) * (wd_ref[...] * v + b0_ref[...])).astype(o_ref.dtype)


def gmlp_decode(u, v, v_gain, ws_diag, bs0):
    return pl.pallas_call(
        _gmlp_decode_kernel,
        out_shape=[jax.ShapeDtypeStruct(u.shape, BF16), jax.ShapeDtypeStruct(u.shape, F32)],
        compiler_params=pltpu.CompilerParams(vmem_limit_bytes=VMEM_LIMIT),
        name="gmlp_decode",
    )(u, v, v_gain, ws_diag, bs0)


def _dil_decode_kernel(q_ref, kvn_ref, st_ref, o_ref, so_ref):
    nh, hd, wb = st_ref.shape[2], st_ref.shape[3], st_ref.shape[4]
    scale = HEAD_DIM ** -0.5
    n_pat = len(DIL_PATTERNS)
    lane1 = lax.broadcasted_iota(jnp.int32, (1, wb), 1)
    dist = wb - lane1
    cnt = jnp.zeros((1, wb), F32)
    for window, dil in DIL_PATTERNS:
        cnt = cnt + jnp.where((dist <= window) & ((dist & (dil - 1)) == 0), 1.0, 0.0)
    last = lax.broadcasted_iota(jnp.int32, (hd, wb), 1) == wb - 1

    def shifted(x, new_col):
        return jnp.where(last, new_col, pltpu.roll(x, wb - 1, axis=1))

    s_rows, s_new_rows = [], []
    for h in range(nh):
        kh = st_ref[0, 0, h]
        qh = q_ref[0, h] * scale
        kn = kvn_ref[0, 0, h]
        s_rows.append(jnp.sum(kh * qh, axis=0, keepdims=True))
        s_new_rows.append(jnp.sum(kn * qh, axis=0, keepdims=True))
        so_ref[0, 0, h] = shifted(kh, kn)
    s = jnp.concatenate(s_rows, axis=0)
    s_new = jnp.concatenate(s_new_rows, axis=0)
    s = jnp.where(cnt > 0.0, s, NEG_INF)
    m = jnp.maximum(jnp.max(s, axis=-1, keepdims=True), s_new)
    e = cnt * jnp.exp(s - m)
    e_new = n_pat * jnp.exp(s_new - m)
    inv = 1.0 / (jnp.sum(e, axis=-1, keepdims=True) + e_new)
    for h in range(nh):
        vh = st_ref[0, 1, h]
        vn = kvn_ref[0, 1, h]
        acc = jnp.sum(vh * e[h:h + 1, :], axis=1, keepdims=True) + e_new[h:h + 1, :] * vn
        o_ref[0, h] = acc * inv[h:h + 1, :]
        so_ref[0, 1, h] = shifted(vh, vn)


def dil_decode(q, kv_new, state):
    b, wb, _, nh, hd = state.shape
    assert wb == DIL_MAX
    st_t = jnp.transpose(state, (0, 2, 3, 4, 1))
    o, so_t = pl.pallas_call(
        _dil_decode_kernel,
        grid=(b,),
        in_specs=[pl.BlockSpec((1, nh, hd, 1), lambda i: (i, 0, 0, 0)),
                  pl.BlockSpec((1, 2, nh, hd, 1), lambda i: (i, 0, 0, 0, 0)),
                  pl.BlockSpec((1, 2, nh, hd, wb), lambda i: (i, 0, 0, 0, 0))],
        out_specs=[pl.BlockSpec((1, nh, hd, 1), lambda i: (i, 0, 0, 0)),
                   pl.BlockSpec((1, 2, nh, hd, wb), lambda i: (i, 0, 0, 0, 0))],
        out_shape=[jax.ShapeDtypeStruct((b, nh, hd, 1), F32), jax.ShapeDtypeStruct(st_t.shape, F32)],
        compiler_params=_cparams("arbitrary"),
        name="dil_decode",
    )(q[..., None], kv_new[..., None], st_t)
    return o[..., 0], jnp.transpose(so_t, (0, 4, 1, 2, 3))


def layer0_sample(x, mod, ln_mix, ln_ffn, ffn_w, pp, cache, page_table, state_win, state_h, state_conv):
    b, _, d = x.shape
    xs = x.reshape(1, b, d)
    q_bf, rows, _, win, _, xr, gr, gl = in_proj(
        xs, mod, 0, 1, ln_mix, pp["w"], pp["hg"], pp["norm_blocks"], pp["outs"], tm=b)
    n_gate = NSA_G * NSA_REP * 3
    gl3 = gl[0, :, :n_gate].reshape(b, NSA_G * NSA_REP, 3)
    o_nsa, win_out = nsa_decode(page_table, cache, q_bf.reshape(b, 1, -1), rows.reshape(b, 1, -1),
                                win.reshape(b, 1, -1), gl3, state_win, pp["cw"])
    o_rnn, h_new, conv_new = rglru_decode(xr[0], gr[0], state_conv.transpose(1, 0, 2), state_h, pp["rw"])
    y = out_ffn(o_nsa.reshape(1, b, -1), o_rnn[None], xs, mod, ln_ffn, pp["wo_a"], pp["wo_b"], *ffn_w, tm=b)
    state = (rows.reshape(b, 1, 4, NSA_G, HEAD_DIM), win_out, h_new, conv_new.transpose(1, 0, 2))
    return y.reshape(b, 1, d), state


def layer1_sample(x, mod, ln_mix, ln_ffn, ffn_w, pp, state_dil):
    b, _, d = x.shape
    xs = x.reshape(1, b, d)
    u, v, q, kv = in_proj(xs, mod, 0, 1, ln_mix, pp["w"], pp["hg"], pp["norm_blocks"], pp["outs"], tm=b)
    o_c, v_n = gmlp_decode(u[0], v[0], pp["v_gain"], pp["ws_diag"], pp["bs0"])
    heads = state_dil.shape[3]
    o_d, dil_out = dil_decode(q.reshape(b, heads, HEAD_DIM), kv.reshape(b, 2, heads, HEAD_DIM), state_dil)
    y = out_ffn(o_c[None], o_d.reshape(1, b, -1).astype(BF16), xs, mod, ln_ffn, pp["wo_a"], pp["wo_b"], *ffn_w,
                tm=b)
    return y.reshape(b, 1, d), (dil_out, v_n.reshape(b, 1, -1))


def kernel(x_prompt, x_sample, cache_nsa_kv, state_nsa_win, state_rglru_h, state_rglru_conv, state_dil_kv, page_table, c_prompt, c_sample, norm_mix_g, norm_ffn_g, w_ada, b_ada, w_ffn_gate, w_ffn_up, w_ffn_down, w_in_ab, w_out_ab, nsa_q_gain, nsa_k_gain, nsa_cmp_w1, nsa_cmp_w2, nsa_cmp_pos, rg_conv_w, rg_conv_b, rg_wa, rg_ba, rg_wx, rg_bx, rg_lambda, w_in_cd, w_out_cd, gmlp_v_gain, gmlp_ws, gmlp_bs, dil_q_gain, dil_k_gain):
    depth = norm_mix_g.shape[0]
    bp, bs = x_prompt.shape[0], x_sample.shape[0]
    pad = -(bp + bs) % 8
    c_all = jnp.concatenate([c_prompt, c_sample, jnp.zeros((pad, c_prompt.shape[1]), F32)], axis=0)
    mod_all = ada_mod(c_all, w_ada.astype(BF16), b_ada)
    yp, ys = x_prompt, x_sample
    kv_p, kv_s, win_p, win_s, h_p, h_s, conv_p, conv_s, dil_p, dil_s, gv_s = ([] for _ in range(11))
    for layer in range(depth):
        i = layer // 2
        mod_p = mod_all[layer, :bp, None, :]
        mod_s = mod_all[layer, None, bp:bp + bs, :]
        ffn_w = (w_ffn_gate[layer].astype(BF16), w_ffn_up[layer].astype(BF16), w_ffn_down[layer].astype(BF16))
        if layer % 2 == 0:
            pp = prep_layer0(dict(w_in=w_in_ab[i], w_out=w_out_ab[i], q_gain=nsa_q_gain[i], k_gain=nsa_k_gain[i],
                                  cmp_w1=nsa_cmp_w1[i], cmp_w2=nsa_cmp_w2[i], cmp_pos=nsa_cmp_pos[i],
                                  conv_w=rg_conv_w[i], conv_b=rg_conv_b[i], wa=rg_wa[i], ba=rg_ba[i],
                                  wx=rg_wx[i], bx=rg_bx[i], lam=rg_lambda[i]))
            yp, st = layer0_prompt(yp, mod_p, norm_mix_g[layer], norm_ffn_g[layer], ffn_w, pp)
            kv_p.append(st[0]); win_p.append(st[1]); h_p.append(st[2]); conv_p.append(st[3])
            ys, st = layer0_sample(ys, mod_s, norm_mix_g[layer], norm_ffn_g[layer], ffn_w, pp, cache_nsa_kv[i],
                                   page_table, state_nsa_win[i], state_rglru_h[i], state_rglru_conv[i])
            kv_s.append(st[0]); win_s.append(st[1]); h_s.append(st[2]); conv_s.append(st[3])
        else:
            pp = prep_layer1(dict(w_in=w_in_cd[i], w_out=w_out_cd[i], v_gain=gmlp_v_gain[i], ws=gmlp_ws[i],
                                  bs=gmlp_bs[i], q_gain=dil_q_gain[i], k_gain=dil_k_gain[i]))
            yp, st = layer1_prompt(yp, mod_p, norm_mix_g[layer], norm_ffn_g[layer], ffn_w, pp)
            dil_p.append(st)
            ys, st = layer1_sample(ys, mod_s, norm_mix_g[layer], norm_ffn_g[layer], ffn_w, pp, state_dil_kv[i])
            dil_s.append(st[0]); gv_s.append(st[1])
    return (yp, ys, jnp.stack(kv_p), jnp.stack(kv_s), jnp.stack(win_p), jnp.stack(win_s),
            jnp.stack(h_p), jnp.stack(h_s), jnp.stack(conv_p), jnp.stack(conv_s),
            jnp.stack(dil_p), jnp.stack(dil_s), jnp.stack(gv_s))
```

```python
import functools

import numpy as np
import jax
import jax.numpy as jnp
from jax import lax
from jax.experimental import pallas as pl
from jax.experimental.pallas import tpu as pltpu

F32 = jnp.float32
BF16 = jnp.bfloat16

LANES = 128
HEAD_DIM = 64
QBLK = 128
CMP_BLOCK = 32
SEL_BLOCK = 64
N_SEL = 16
NSA_WINDOW = 512
FORCE_SCORE = 1.0e4
DIL_PATTERNS = ((128, 1), (512, 4), (2048, 16))
DIL_MAX = 2048
RG_C = 8.0
RMS_EPS = 1e-6
NEG_INF = -1e30
LOG2_E = 1.4426950408889634
VMEM_LIMIT = 56 * 1024 * 1024


def _cparams(*sem):
    return pltpu.CompilerParams(dimension_semantics=sem, vmem_limit_bytes=VMEM_LIMIT)


def _dot(a, b):
    return jnp.dot(a, b, preferred_element_type=F32)


def _dot_nt(a, b):
    return lax.dot_general(a, b, (((1,), (1,)), ((), ())), preferred_element_type=F32)


def _gelu(x):
    return 0.5 * x * (1.0 + jnp.tanh(np.sqrt(2.0 / np.pi) * (x + 0.044715 * (x * x * x))))


def _sigmoid(x):
    return 1.0 / (1.0 + jnp.exp(-x))


def _head_norm(z, gain):
    lo = lax.broadcasted_iota(jnp.int32, z.shape, 1) < HEAD_DIM
    z2 = z * z
    s_lo = jnp.sum(jnp.where(lo, z2, 0.0), axis=-1, keepdims=True)
    s_hi = jnp.sum(jnp.where(lo, 0.0, z2), axis=-1, keepdims=True)
    inv = lax.rsqrt(jnp.where(lo, s_lo, s_hi) * (1.0 / HEAD_DIM) + RMS_EPS)
    return z * inv * gain


def _ada_kernel(c_ref, w_ref, b_ref, o_ref):
    c = c_ref[...]
    s = c * _sigmoid(c)
    o_ref[0] = _dot(s.astype(BF16), w_ref[0]) + b_ref[0]


def ada_mod(c_all, w_ada, b_ada):
    m, d = c_all.shape
    nl, _, n = w_ada.shape
    tn = 1536
    return pl.pallas_call(
        _ada_kernel,
        grid=(nl, n // tn),
        in_specs=[pl.BlockSpec((m, d), lambda l, j: (0, 0)),
                  pl.BlockSpec((1, d, tn), lambda l, j: (l, 0, j)),
                  pl.BlockSpec((1, 1, tn), lambda l, j: (l, 0, j))],
        out_specs=pl.BlockSpec((1, m, tn), lambda l, j: (l, 0, j)),
        out_shape=jax.ShapeDtypeStruct((nl, m, n), F32),
        compiler_params=_cparams("arbitrary", "arbitrary"),
        name="ada_mod",
    )(c_all, w_ada, b_ada.reshape(nl, 1, n))


def _in_proj_kernel(x_ref, sc_ref, sh_ref, g_ref, w_ref, hg_ref, *out_refs, norm_blocks, outs, touts):
    i = pl.program_id(1)
    x = x_ref[0]
    ms = jnp.mean(x * x, axis=-1, keepdims=True)
    h = x * lax.rsqrt(ms + RMS_EPS) * g_ref[...]
    h = h * (1.0 + sc_ref[0]) + sh_ref[0]
    z = _dot(h.astype(BF16), w_ref[...])
    nblk = z.shape[1] // LANES
    blocks = []
    for j in range(nblk):
        zb = z[:, j * LANES:(j + 1) * LANES]
        if j in norm_blocks:
            zb = _head_norm(zb, hg_ref[:, j * LANES:(j + 1) * LANES])
        blocks.append(zb)
    for o_ref, (c0, width, _) in zip(out_refs, outs):
        for j in range(width // LANES):
            o_ref[0, :, j * LANES:(j + 1) * LANES] = blocks[c0 // LANES + j].astype(o_ref.dtype)
    for o_ref, (c0, width, first) in zip(out_refs[len(outs):], touts):
        @pl.when(i >= first)
        def _(o_ref=o_ref, c0=c0, width=width):
            for j in range(width // LANES):
                o_ref[0, j * LANES:(j + 1) * LANES, :] = blocks[c0 // LANES + j].T


def in_proj(x, mod, sh_idx, sc_idx, g, w, head_gain, norm_blocks, outs, tm, touts=()):
    bm, t, d = x.shape
    r = mod.shape[1]
    n = w.shape[1]
    tm = min(tm, t)
    if r == 1:
        mod_spec = lambda k: pl.BlockSpec((1, 1, d), lambda b, i: (b, 0, k))
    else:
        assert r == t and tm == t
        mod_spec = lambda k: pl.BlockSpec((1, tm, d), lambda b, i: (b, 0, k))
    tfirst = []
    for (_, _, npos) in touts:
        assert npos % tm == 0 and npos <= t
        tfirst.append((t - npos) // tm)
    kern = functools.partial(_in_proj_kernel, norm_blocks=tuple(norm_blocks), outs=tuple(outs),
                             touts=tuple((c0, wd, f) for (c0, wd, _), f in zip(touts, tfirst)))
    return pl.pallas_call(
        kern,
        grid=(bm, t // tm),
        in_specs=[pl.BlockSpec((1, tm, d), lambda b, i: (b, i, 0)),
                  mod_spec(sc_idx), mod_spec(sh_idx),
                  pl.BlockSpec((1, d), lambda b, i: (0, 0)),
                  pl.BlockSpec((d, n), lambda b, i: (0, 0)),
                  pl.BlockSpec((1, n), lambda b, i: (0, 0))],
        out_specs=[pl.BlockSpec((1, tm, wd), lambda b, i: (b, i, 0)) for (_, wd, _) in outs]
        + [pl.BlockSpec((1, wd, tm), lambda b, i, f=f: (b, 0, jnp.maximum(i - f, 0)))
           for (_, wd, _), f in zip(touts, tfirst)],
        out_shape=[jax.ShapeDtypeStruct((bm, t, wd), dt) for (_, wd, dt) in outs]
        + [jax.ShapeDtypeStruct((bm, wd, npos), F32) for (_, wd, npos) in touts],
        compiler_params=_cparams("arbitrary", "arbitrary"),
        name="in_proj",
    )(x, mod, mod, g.reshape(1, d), w, head_gain)


def _out_ffn_kernel(ma_ref, mb_ref, x_ref, gm_ref, shf_ref, scf_ref, gf_ref, lnf_ref,
                    woa_ref, wob_ref, wg_ref, wu_ref, wd_ref, o_ref, *, hidden_chunk):
    mix = _dot(ma_ref[0], woa_ref[...]) + _dot(mb_ref[0], wob_ref[...])
    x1 = x_ref[0] + gm_ref[0] * mix
    ms = jnp.mean(x1 * x1, axis=-1, keepdims=True)
    hf = x1 * lax.rsqrt(ms + RMS_EPS) * lnf_ref[...]
    hf = (hf * (1.0 + scf_ref[0]) + shf_ref[0]).astype(BF16)
    hidden = wg_ref.shape[1]
    ffn = jnp.zeros(x1.shape, F32)
    for c0 in range(0, hidden, hidden_chunk):
        gt = _dot(hf, wg_ref[:, c0:c0 + hidden_chunk])
        up = _dot(hf, wu_ref[:, c0:c0 + hidden_chunk])
        act = (gt * _sigmoid(gt) * up).astype(BF16)
        ffn = ffn + _dot(act, wd_ref[c0:c0 + hidden_chunk, :])
    o_ref[0] = x1 + gf_ref[0] * ffn


def out_ffn(mix_a, mix_b, x, mod, ln_ffn, wo_a, wo_b, wg, wu, wd, tm):
    bm, t, d = x.shape
    r = mod.shape[1]
    tm = min(tm, t)
    ka, kb = mix_a.shape[2], mix_b.shape[2]
    hidden = wg.shape[1]
    if r == 1:
        mod_spec = lambda k: pl.BlockSpec((1, 1, d), lambda b, i: (b, 0, k))
    else:
        assert r == t and tm == t
        mod_spec = lambda k: pl.BlockSpec((1, tm, d), lambda b, i: (b, 0, k))
    const = lambda shape: pl.BlockSpec(shape, lambda b, i: (0,) * len(shape), pipeline_mode=pl.Buffered(1))
    kern = functools.partial(_out_ffn_kernel, hidden_chunk=hidden // 2)
    return pl.pallas_call(
        kern,
        grid=(bm, t // tm),
        in_specs=[pl.BlockSpec((1, tm, ka), lambda b, i: (b, i, 0)),
                  pl.BlockSpec((1, tm, kb), lambda b, i: (b, i, 0)),
                  pl.BlockSpec((1, tm, d), lambda b, i: (b, i, 0)),
                  mod_spec(2), mod_spec(3), mod_spec(4), mod_spec(5),
                  const((1, d)), const((ka, d)), const((kb, d)),
                  const((d, hidden)), const((d, hidden)), const((hidden, d))],
        out_specs=pl.BlockSpec((1, tm, d), lambda b, i: (b, i, 0)),
        out_shape=jax.ShapeDtypeStruct((bm, t, d), F32),
        compiler_params=_cparams("arbitrary", "arbitrary"),
        name="out_ffn",
    )(mix_a, mix_b, x, mod, mod, mod, mod, ln_ffn.reshape(1, d), wo_a, wo_b, wg, wu, wd)


def _compress_rows(load, nh, w1_ref, pos_ref, w2_ref):
    halves = []
    for parity in range(2):
        halves.append(jnp.concatenate(
            [load(parity * CMP_BLOCK + l, nh, 2 * CMP_BLOCK) for l in range(CMP_BLOCK)], axis=1))
    xs = (jnp.concatenate(halves, axis=0) + pos_ref[...]).astype(BF16)
    hid = _gelu(_dot(xs, w1_ref[...]))
    return _dot(hid.astype(BF16), w2_ref[...])


def _nsa_compress_kernel(rk_ref, rv_ref, w1k_ref, posk_ref, w2k_ref, w1v_ref, posv_ref, w2v_ref, kg_ref,
                         kc_ref, vc_ref):
    nh = kc_ref.shape[1] // 2
    ldk = lambda s, n, st: rk_ref[0, pl.ds(s, n, stride=st), :]
    ldv = lambda s, n, st: rv_ref[0, pl.ds(s, n, stride=st), :]
    kc = _compress_rows(ldk, nh, w1k_ref, posk_ref, w2k_ref)
    kc_ref[0] = _head_norm(kc, kg_ref[...]).astype(kc_ref.dtype)
    vc_ref[0] = _compress_rows(ldv, nh, w1v_ref, posv_ref, w2v_ref).astype(vc_ref.dtype)


def nsa_compress(rows, cw):
    b, t, _ = rows.shape
    nc = t // CMP_BLOCK
    const = lambda a: pl.BlockSpec(a.shape, lambda i: (0,) * a.ndim)
    ws = (cw["w1k"], cw["posk"], cw["w2k"], cw["w1v"], cw["posv"], cw["w2v"], cw["kg0"])
    return pl.pallas_call(
        _nsa_compress_kernel,
        grid=(b,),
        in_specs=[pl.BlockSpec((1, t, LANES), lambda i: (i, 0, 0)),
                  pl.BlockSpec((1, t, LANES), lambda i: (i, 0, 1))] + [const(a) for a in ws],
        out_specs=[pl.BlockSpec((1, nc, LANES), lambda i: (i, 0, 0))] * 2,
        out_shape=[jax.ShapeDtypeStruct((b, nc, LANES), BF16)] * 2,
        compiler_params=_cparams("arbitrary"),
        name="nsa_compress",
    )(rows, rows, *ws)


PICKED = -2.0


def _pick_top(work, blk_f):
    top = jnp.max(work, axis=0, keepdims=True)
    first = jnp.min(jnp.where(work == top, blk_f, 1e9), axis=0, keepdims=True)
    return jnp.where(blk_f == first, PICKED, work)


def _select_blocks_t(imp, blk_f, n_pick):
    work = imp
    for _ in range(n_pick):
        work = _pick_top(work, blk_f)
    return work == PICKED


NSA_SEL_CHUNK = 8
NSA_SUB = 256
NSA_DECODE_SEQS = 4

def _dot_tn(a, b):
    return lax.dot_general(a, b, (((0,), (0,)), ((), ())), preferred_element_type=F32)


def _nsa_prompt_kernel(q_ref, kc_ref, vc_ref, ks_ref, vs_ref, kw_ref, vw_ref, gl_ref, e_ref, o_ref,
                       s_ref, mx_ref, m_ref, l_ref, acc_ref):
    qi = pl.program_id(1)
    t0 = qi * QBLK
    nc = kc_ref.shape[1]
    nb = nc // 2
    scale = HEAD_DIM ** -0.5
    rep = q_ref.shape[2] // LANES
    nhead = 2 * rep
    ncol = nhead * QBLK

    lane = lax.broadcasted_iota(jnp.int32, (QBLK, LANES), 1)
    sub = lax.broadcasted_iota(jnp.int32, (QBLK, LANES), 0)
    tile8 = lambda x: jnp.concatenate([x] * nhead, axis=1)
    q = q_ref[0]
    qs = []
    for g in range(2):
        gm = (lane >= g * HEAD_DIM) & (lane < (g + 1) * HEAD_DIM)
        for r in range(rep):
            qs.append(jnp.where(gm, q[:, r * LANES:(r + 1) * LANES], jnp.zeros((), q.dtype)))
    qall = (jnp.concatenate(qs, axis=0).astype(F32) * (scale * LOG2_E)).astype(BF16)

    crow = lax.broadcasted_iota(jnp.int32, (nc, QBLK), 0)
    cidx = jnp.where(crow < nb, 2 * crow, 2 * (crow - nb) + 1)
    cvalid = ((cidx + 1) * CMP_BLOCK - 1) <= t0 + lax.broadcasted_iota(jnp.int32, (nc, QBLK), 1)
    cbias = tile8(jnp.where(cvalid, 0.0, NEG_INF))
    cone = tile8(jnp.where(cvalid, 1.0, 0.0))
    wkeys = NSA_WINDOW + QBLK
    wstart = jnp.maximum(qi - NSA_WINDOW // QBLK, 0) * QBLK
    dist = (t0 + lax.broadcasted_iota(jnp.int32, (wkeys, QBLK), 1)) \
        - (wstart + lax.broadcasted_iota(jnp.int32, (wkeys, QBLK), 0))
    wbias = tile8(jnp.where((dist >= 0) & (dist <= NSA_WINDOW), 0.0, NEG_INF))
    wrows = pl.ds(pl.multiple_of(wstart, QBLK), wkeys)
    sc = _dot_nt(kc_ref[0], qall) + cbias
    sw = _dot_nt(kw_ref[0, wrows, :], qall) + wbias
    mc = jnp.max(sc, axis=0, keepdims=True)
    mw = jnp.max(sw, axis=0, keepdims=True)
    ec = jnp.exp2(sc - mc) * cone
    ew = jnp.exp2(sw - mw)
    p = ec * (1.0 / jnp.maximum(jnp.sum(ec, axis=0, keepdims=True), 1e-30))
    lw = jnp.sum(ew, axis=0, keepdims=True)
    o_cmp = _dot_tn(vc_ref[0], p.astype(BF16))
    o_win = _dot_tn(vw_ref[0, wrows, :], ew.astype(BF16)) * (1.0 / lw)

    brow = lax.broadcasted_iota(jnp.int32, (nb, 2 * QBLK), 0)
    tb = t0 + lax.broadcasted_iota(jnp.int32, (nb, 2 * QBLK), 1) % QBLK
    imps = []
    for g in range(2):
        ps = p[:, (g * rep) * QBLK:(g * rep + 1) * QBLK]
        for r in range(1, rep):
            ps = ps + p[:, (g * rep + r) * QBLK:(g * rep + r + 1) * QBLK]
        imps.append(ps[:nb] + ps[nb:])
    imp = jnp.concatenate(imps, axis=1)
    imp = jnp.where((brow == 0) | (brow == tb // SEL_BLOCK), FORCE_SCORE, imp)
    imp = jnp.where(brow * SEL_BLOCK <= tb, imp, -1.0)
    sel = _select_blocks_t(imp, brow.astype(F32), min(N_SEL, nb))
    bias_t = jnp.where(sel, 0.0, NEG_INF)
    if nb < LANES:
        bias_t = jnp.concatenate([bias_t, jnp.zeros((LANES - nb, 2 * QBLK), F32)], axis=0)
    qbias = []
    for g in range(2):
        qbias += [bias_t[:, g * QBLK:(g + 1) * QBLK].T.astype(BF16)] * rep
    qaug = jnp.concatenate([qall, jnp.concatenate(qbias, axis=0)], axis=1)

    chunk = NSA_SEL_CHUNK * QBLK
    nsub = chunk // NSA_SUB
    sub_rows = lambda kc, j: pl.ds(pl.multiple_of(kc * chunk + j * NSA_SUB, NSA_SUB), NSA_SUB)
    neg_row = jnp.full((1, ncol), NEG_INF, F32)

    def scores(kc, j):
        r = sub_rows(kc, j)
        return _dot_nt(jnp.concatenate([ks_ref[0, r, :], e_ref[r, :]], axis=1), qaug)

    def rescale(mx):
        m_old = m_ref[...]
        m_new = jnp.maximum(m_old, mx)
        alpha = jnp.exp2(m_old - m_new)
        m_ref[...] = m_new
        l_ref[...] = alpha * l_ref[...]
        acc_ref[...] = alpha * acc_ref[...]
        return m_new

    def consume(s, v, m_new):
        e = jnp.exp2(s - m_new)
        l_ref[...] += jnp.sum(e, axis=0, keepdims=True)
        acc_ref[...] += _dot_tn(v, e.astype(BF16))

    def stage(kc, cur, nxt):
        m_new = rescale(mx_ref[cur])
        mx = neg_row
        for j in range(nsub):
            s_next = scores(kc + 1, j)
            s_ref[nxt, j * NSA_SUB:(j + 1) * NSA_SUB, :] = s_next
            mx = jnp.maximum(mx, jnp.max(s_next, axis=0, keepdims=True))
            consume(s_ref[cur, j * NSA_SUB:(j + 1) * NSA_SUB, :], vs_ref[0, sub_rows(kc, j), :], m_new)
        mx_ref[nxt] = mx

    m_ref[...] = neg_row
    l_ref[...] = jnp.zeros((1, ncol), F32)
    acc_ref[...] = jnp.zeros((LANES, ncol), F32)
    mx = neg_row
    for j in range(nsub):
        s0 = scores(0, j)
        s_ref[0, j * NSA_SUB:(j + 1) * NSA_SUB, :] = s0
        mx = jnp.maximum(mx, jnp.max(s0, axis=0, keepdims=True))
    mx_ref[0] = mx

    last = qi // NSA_SEL_CHUNK

    def stage_pair(pair, carry):
        stage(2 * pair, 0, 1)
        stage(2 * pair + 1, 1, 0)
        return carry

    lax.fori_loop(0, last // 2, stage_pair, 0)

    @pl.when(last % 2 == 1)
    def _():
        stage(last - 1, 0, 1)

    slot = last % 2
    kpos = last * chunk + lax.broadcasted_iota(jnp.int32, (chunk, QBLK), 0)
    causal = jnp.where(kpos <= t0 + lax.broadcasted_iota(jnp.int32, (chunk, QBLK), 1), 0.0, NEG_INF)
    s_last = [s_ref[slot, j * NSA_SUB:(j + 1) * NSA_SUB, :] + tile8(causal[j * NSA_SUB:(j + 1) * NSA_SUB])
              for j in range(nsub)]
    mx = neg_row
    for s in s_last:
        mx = jnp.maximum(mx, jnp.max(s, axis=0, keepdims=True))
    m_new = rescale(mx)
    q_end = (qi % NSA_SEL_CHUNK + 1) * QBLK
    consume(s_last[0], vs_ref[0, sub_rows(last, 0), :], m_new)
    for j in range(1, nsub):
        @pl.when(j * NSA_SUB < q_end)
        def _(j=j):
            consume(s_last[j], vs_ref[0, sub_rows(last, j), :], m_new)
    o_sel = acc_ref[...] * (1.0 / l_ref[...])

    gate = _sigmoid(gl_ref[0]).T
    for r in range(rep):
        og = []
        for g in range(2):
            c = (g * rep + r) * 3
            cols = slice((g * rep + r) * QBLK, (g * rep + r + 1) * QBLK)
            og.append(gate[c:c + 1] * o_cmp[:, cols] + gate[c + 1:c + 2] * o_sel[:, cols]
                      + gate[c + 2:c + 3] * o_win[:, cols])
        o_ref[0, :, r * LANES:(r + 1) * LANES] = jnp.where(sub < HEAD_DIM, og[0], og[1]).T.astype(o_ref.dtype)


def nsa_prompt_attn(q_bf, kc, vc, rows_bf, win_bf, gl):
    b, t, qw = q_bf.shape
    nc = kc.shape[1]
    nq = t // QBLK
    assert t % (NSA_SEL_CHUNK * QBLK) == 0 and t >= NSA_WINDOW + QBLK and t // SEL_BLOCK <= LANES
    ncol = 2 * (qw // LANES) * QBLK
    full = lambda k: pl.BlockSpec((1, t, LANES), lambda i, j: (i, 0, k))
    member = (jnp.arange(t)[:, None] // SEL_BLOCK == jnp.arange(LANES)[None, :]).astype(BF16)
    return pl.pallas_call(
        _nsa_prompt_kernel,
        grid=(b, nq),
        in_specs=[pl.BlockSpec((1, QBLK, qw), lambda i, j: (i, j, 0)),
                  pl.BlockSpec((1, nc, LANES), lambda i, j: (i, 0, 0)),
                  pl.BlockSpec((1, nc, LANES), lambda i, j: (i, 0, 0)),
                  full(2), full(3), full(0), full(1),
                  pl.BlockSpec((1, QBLK, LANES), lambda i, j: (i, j, 0)),
                  pl.BlockSpec((t, LANES), lambda i, j: (0, 0))],
        out_specs=pl.BlockSpec((1, QBLK, qw), lambda i, j: (i, j, 0)),
        out_shape=jax.ShapeDtypeStruct((b, t, qw), BF16),
        scratch_shapes=[pltpu.VMEM((2, NSA_SEL_CHUNK * QBLK, ncol), F32),
                        pltpu.VMEM((2, 1, ncol), F32),
                        pltpu.VMEM((1, ncol), F32), pltpu.VMEM((1, ncol), F32),
                        pltpu.VMEM((LANES, ncol), F32)],
        compiler_params=_cparams("arbitrary", "arbitrary"),
        name="nsa_prompt_attn",
    )(q_bf, kc, vc, rows_bf, rows_bf, win_bf, win_bf, gl, member)


def _rglru_gates(xc, wa_ref, ba_ref, wx_ref, bx_ref, lam_ref):
    xb = xc.astype(BF16)
    r = _sigmoid(_dot(xb, wa_ref[...]) + ba_ref[...])
    i = _sigmoid(_dot(xb, wx_ref[...]) + bx_ref[...])
    nl = -lam_ref[...]
    softplus = jnp.maximum(nl, 0.0) + jnp.log1p(jnp.exp(-jnp.abs(nl)))
    log_a = -RG_C * r * softplus
    a = jnp.exp(log_a)
    u = jnp.sqrt(-jnp.tanh(log_a) * (a * a + 1.0)) * (i * xc)
    return a, u


def _rglru_prompt_kernel(xr_ref, gr_ref, cw_ref, cb_ref, wa_ref, ba_ref, wx_ref, bx_ref, lam_ref,
                         o_ref, hl_ref, cl_ref, hcar, xcar):
    i = pl.program_id(1)
    tm, c = xr_ref.shape[1], xr_ref.shape[2]

    @pl.when(i == 0)
    def _():
        hcar[...] = jnp.zeros(hcar.shape, F32)
        xcar[...] = jnp.zeros(xcar.shape, F32)

    x = xr_ref[0]
    prev = xcar[...]
    row = lax.broadcasted_iota(jnp.int32, (tm, c), 0)
    xc = cb_ref[...] + cw_ref[3:4, :] * x
    for k in range(1, 4):
        cur = pltpu.roll(x, k, axis=0)
        old = jnp.tile(pltpu.roll(prev, k, axis=0), (tm // 8, 1))
        xc = xc + cw_ref[3 - k:4 - k, :] * jnp.where(row < k, old, cur)
    a, u = _rglru_gates(xc, wa_ref, ba_ref, wx_ref, bx_ref, lam_ref)

    s = 1
    while s < 8:
        keep = (row % 8) >= s
        a_sh = jnp.where(keep, pltpu.roll(a, s, axis=0), 1.0)
        u_sh = jnp.where(keep, pltpu.roll(u, s, axis=0), 0.0)
        u = a * u_sh + u
        a = a * a_sh
        s *= 2
    carry = hcar[0:1, :]
    groups = []
    for g in range(tm // 8):
        hg = a[g * 8:(g + 1) * 8] * carry + u[g * 8:(g + 1) * 8]
        groups.append(hg)
        carry = hg[7:8, :]
    h = jnp.concatenate(groups, axis=0)
    o_ref[0] = (h * _gelu(gr_ref[0])).astype(o_ref.dtype)
    hcar[...] = jnp.broadcast_to(carry, hcar.shape)
    xcar[...] = x[tm - 8:tm, :]
    hl_ref[0] = hcar[...]
    cl_ref[0] = xcar[...]


def rglru_prompt(xr, gr, rw, tm):
    b, t, c = xr.shape
    tm = min(tm, t)
    const = lambda a: pl.BlockSpec(a.shape, lambda i, j: (0,) * a.ndim)
    ws = (rw["conv_w"], rw["conv_b"], rw["wa"], rw["ba"], rw["wx"], rw["bx"], rw["lam"])
    tile = pl.BlockSpec((1, tm, c), lambda i, j: (i, j, 0))
    last = pl.BlockSpec((1, 8, c), lambda i, j: (i, 0, 0))
    return pl.pallas_call(
        _rglru_prompt_kernel,
        grid=(b, t // tm),
        in_specs=[tile, tile] + [const(a) for a in ws],
        out_specs=[tile, last, last],
        out_shape=[jax.ShapeDtypeStruct((b, t, c), BF16),
                   jax.ShapeDtypeStruct((b, 8, c), F32),
                   jax.ShapeDtypeStruct((b, 8, c), F32)],
        scratch_shapes=[pltpu.VMEM((8, c), F32), pltpu.VMEM((8, c), F32)],
        compiler_params=_cparams("arbitrary", "arbitrary"),
        name="rglru_prompt",
    )(xr, gr, *ws)


def _gmlp_v(v_raw, vg_ref):
    v = _gelu(v_raw)
    ms = jnp.mean(v * v, axis=-1, keepdims=True)
    return v * lax.rsqrt(ms + RMS_EPS) * vg_ref[...]


GMLP_CHUNKS = 4


def _gmlp_prompt_kernel(u_ref, v_ref, vg_ref, ws_ref, bs_ref, o_ref):
    lc = ws_ref.shape[1]
    tril = (lax.broadcasted_iota(jnp.int32, (lc, lc), 0) >= lax.broadcasted_iota(jnp.int32, (lc, lc), 1))
    lane = lax.broadcasted_iota(jnp.int32, (lc, LANES), 1)
    wts = [jnp.where(tril, ws_ref[g], 0.0).astype(BF16) for g in range(ws_ref.shape[0])]
    for ci in range(u_ref.shape[1] // lc):
        rows = slice(ci * lc, (ci + 1) * lc)
        v = _gmlp_v(v_ref[0, rows, :], vg_ref).astype(BF16)
        parts = []
        for j in range(v.shape[1] // LANES):
            vj = v[:, j * LANES:(j + 1) * LANES]
            parts.append(jnp.where(lane < HEAD_DIM, _dot(wts[2 * j], vj), _dot(wts[2 * j + 1], vj)))
        mixed = jnp.concatenate(parts, axis=1) + bs_ref[...]
        o_ref[0, rows, :] = (_gelu(u_ref[0, rows, :]) * mixed).astype(o_ref.dtype)


def gmlp_prompt(u, v, v_gain, ws, bs_exp):
    b, t, c = u.shape
    lc = ws.shape[1]
    rows = min(GMLP_CHUNKS * lc, t)
    tile = pl.BlockSpec((1, rows, c), lambda i, j: (i, j, 0))
    const = lambda a: pl.BlockSpec(a.shape, lambda i, j: (0,) * a.ndim)
    return pl.pallas_call(
        _gmlp_prompt_kernel,
        grid=(b, t // rows),
        in_specs=[tile, tile, const(v_gain), const(ws), const(bs_exp)],
        out_specs=tile,
        out_shape=jax.ShapeDtypeStruct((b, t, c), BF16),
        compiler_params=_cparams("arbitrary", "arbitrary"),
        name="gmlp_prompt",
    )(u, v, v_gain, ws, bs_exp)


DIL_SPAN = DIL_MAX
DIL_LOCKSTEP = 4


def _dil_prompt_kernel(q_ref, k_ref, v_ref, o_ref, acc_ref, m_ref, l_ref):
    span = q_ref.shape[1]
    base = pl.program_id(2) * span
    scale = HEAD_DIM ** -0.5
    lane = lax.broadcasted_iota(jnp.int32, (QBLK, LANES), 1)
    row = lax.broadcasted_iota(jnp.int32, (QBLK, LANES), 0)
    lo = lane < HEAD_DIM
    band1 = jnp.concatenate([jnp.where(lane >= row, 0.0, NEG_INF), jnp.where(lane <= row, 0.0, NEG_INF)], axis=1)
    band = jnp.concatenate([band1, band1], axis=0)
    prev_cols = lax.broadcasted_iota(jnp.int32, (2 * QBLK, 2 * QBLK), 1) < QBLK
    zero = jnp.zeros((), BF16)

    for p, (window, dil) in enumerate(DIL_PATTERNS):
        assert window // dil == QBLK and span % (QBLK * dil) == 0

        def tiles(step, carry, p=p, dil=dil):
            rows, q2, kcat, vcat, bias = [], [], [], [], []
            for u in range(DIL_LOCKSTEP):
                idx = step * DIL_LOCKSTEP + u
                start = (idx // dil) * (QBLK * dil) + idx % dil
                g0 = base + start
                has_prev = g0 >= QBLK * dil
                d_rows = pl.ds(g0, QBLK, stride=dil)
                p_rows = pl.ds(jnp.where(has_prev, g0 - QBLK * dil, g0), QBLK, stride=dil)
                rows.append(pl.ds(start, QBLK, stride=dil))
                q = (q_ref[0, rows[u], :] * scale).astype(BF16)
                q2.append(jnp.concatenate([jnp.where(lo, q, zero), jnp.where(lo, zero, q)], axis=0))
                kcat.append(jnp.concatenate([k_ref[0, p_rows, :], k_ref[0, d_rows, :]], axis=0).astype(BF16))
                vcat.append(jnp.concatenate([v_ref[0, p_rows, :], v_ref[0, d_rows, :]], axis=0).astype(BF16))
                bias.append(band + jnp.where(prev_cols, jnp.where(has_prev, 0.0, NEG_INF), 0.0))
            lock = range(DIL_LOCKSTEP)
            s = [_dot_nt(q2[u], kcat[u]) + bias[u] for u in lock]
            m = [jnp.max(jnp.maximum(s[u][:, :QBLK], s[u][:, QBLK:]), axis=-1, keepdims=True) for u in lock]
            e = [jnp.exp(s[u] - m[u]) for u in lock]
            l = [jnp.sum(e[u][:, :QBLK] + e[u][:, QBLK:], axis=-1, keepdims=True) for u in lock]
            acc = [_dot(e[u].astype(BF16), vcat[u]) for u in lock]
            for u in lock:
                acc_ref[p, rows[u], :] = jnp.where(lo, acc[u][:QBLK], acc[u][QBLK:])
                m_ref[p, rows[u], :] = jnp.where(lo, m[u][:QBLK], m[u][QBLK:])
                l_ref[p, rows[u], :] = jnp.where(lo, l[u][:QBLK], l[u][QBLK:])
            return carry

        lax.fori_loop(0, span // QBLK // DIL_LOCKSTEP, tiles, 0)

    def combine(c, carry):
        rows = pl.ds(pl.multiple_of(c * QBLK, QBLK), QBLK)
        ms = [m_ref[p, rows, :] for p in range(len(DIL_PATTERNS))]
        mx = functools.reduce(jnp.maximum, ms)
        num, den = 0.0, 0.0
        for p, mp in enumerate(ms):
            w = jnp.exp(mp - mx)
            num = num + w * acc_ref[p, rows, :]
            den = den + w * l_ref[p, rows, :]
        o_ref[0, rows, :] = (num * (1.0 / den)).astype(o_ref.dtype)
        return carry

    lax.fori_loop(0, span // QBLK, combine, 0)


def dil_prompt_attn(q, kv):
    b, t, w = q.shape
    npair = w // LANES
    span = min(DIL_SPAN, t)
    n_pat = len(DIL_PATTERNS)
    return pl.pallas_call(
        _dil_prompt_kernel,
        grid=(b, npair, t // span),
        in_specs=[pl.BlockSpec((1, span, LANES), lambda i, p, j: (i, j, p)),
                  pl.BlockSpec((1, t, LANES), lambda i, p, j: (i, 0, p)),
                  pl.BlockSpec((1, t, LANES), lambda i, p, j: (i, 0, npair + p))],
        out_specs=pl.BlockSpec((1, span, LANES), lambda i, p, j: (i, j, p)),
        out_shape=jax.ShapeDtypeStruct((b, t, w), BF16),
        scratch_shapes=[pltpu.VMEM((n_pat, span, LANES), F32)] * 3,
        compiler_params=_cparams("arbitrary", "arbitrary", "arbitrary"),
        name="dil_prompt_attn",
    )(q, kv, kv)


NSA_G, NSA_REP = 2, 4
NSA_QW = NSA_G * NSA_REP * HEAD_DIM
QPERM = np.arange(NSA_QW).reshape(NSA_G, NSA_REP, HEAD_DIM).transpose(1, 0, 2).reshape(-1)


def _block_diag(blocks):
    n, a, b = blocks.shape
    out = jnp.zeros((n, a, n, b), blocks.dtype)
    out = out.at[jnp.arange(n), :, jnp.arange(n), :].set(blocks)
    return out.reshape(n * a, n * b)


def prep_layer0(p):
    w_in = p["w_in"]
    d = w_in.shape[0]
    n_gate = NSA_G * NSA_REP * 3
    kv_w = 6 * NSA_G * HEAD_DIM
    c_gl = NSA_QW + kv_w
    c_xr = c_gl + n_gate
    d_rnn = (w_in.shape[1] - c_xr) // 2
    cols = np.concatenate([QPERM, np.arange(NSA_QW, c_gl), np.arange(c_xr, c_xr + 2 * d_rnn),
                           np.arange(c_gl, c_xr)])
    w = jnp.concatenate([w_in[:, cols], jnp.zeros((d, LANES - n_gate), w_in.dtype)], axis=1).astype(BF16)
    n = w.shape[1]
    hg = jnp.ones((n,), F32)
    hg = hg.at[0:NSA_QW].set(jnp.tile(p["q_gain"], NSA_QW // HEAD_DIM))
    hg = hg.at[NSA_QW + 2 * LANES:NSA_QW + 3 * LANES].set(jnp.tile(p["k_gain"][1], 2))
    hg = hg.at[NSA_QW + 4 * LANES:NSA_QW + 5 * LANES].set(jnp.tile(p["k_gain"][2], 2))
    qb = NSA_QW // LANES
    c_rows, c_win, c_x, c_g, c_l = NSA_QW, NSA_QW + 4 * LANES, NSA_QW + 6 * LANES, NSA_QW + 6 * LANES + d_rnn, \
        NSA_QW + 6 * LANES + 2 * d_rnn
    outs = ((0, NSA_QW, BF16), (c_rows, 4 * LANES, F32), (c_rows, 4 * LANES, BF16), (c_win, 2 * LANES, F32),
            (c_win, 2 * LANES, BF16), (c_x, d_rnn, F32), (c_g, d_rnn, F32), (c_l, LANES, F32))
    cw = {}
    for c, nm in enumerate("kv"):
        w1 = p["cmp_w1"][c]
        big = jnp.zeros((CMP_BLOCK, NSA_G, HEAD_DIM, NSA_G, HEAD_DIM), F32)
        for g in range(NSA_G):
            big = big.at[:, g, :, g, :].set(w1)
        cw["w1" + nm] = big.reshape(CMP_BLOCK * LANES, LANES).astype(BF16)
        cw["pos" + nm] = jnp.tile(p["cmp_pos"][c], (1, NSA_G)).reshape(1, CMP_BLOCK * LANES)
        cw["w2" + nm] = _block_diag(jnp.stack([p["cmp_w2"][c]] * NSA_G)).astype(BF16)
    cw["kg0"] = jnp.tile(p["k_gain"][0], 2).reshape(1, LANES)
    rw = dict(conv_w=p["conv_w"], conv_b=p["conv_b"].reshape(1, -1),
              wa=_block_diag(p["wa"]).astype(BF16), ba=p["ba"].reshape(1, -1),
              wx=_block_diag(p["wx"]).astype(BF16), bx=p["bx"].reshape(1, -1), lam=p["lam"].reshape(1, -1))
    w_out = p["w_out"]
    return dict(w=w, hg=hg.reshape(1, n), norm_blocks=tuple(range(qb)) + (qb + 2, qb + 4), outs=outs,
                cw=cw, rw=rw, wo_a=w_out[QPERM].astype(BF16), wo_b=w_out[NSA_QW:].astype(BF16))


def prep_layer1(p):
    w = p["w_in"].astype(BF16)
    n = w.shape[1]
    c_w = p["v_gain"].shape[0]
    dil_w = (n - 2 * c_w) // 3
    hg = jnp.ones((n,), F32)
    hg = hg.at[2 * c_w:2 * c_w + dil_w].set(jnp.tile(p["q_gain"], dil_w // HEAD_DIM))
    hg = hg.at[2 * c_w + dil_w:2 * c_w + 2 * dil_w].set(jnp.tile(p["k_gain"], dil_w // HEAD_DIM))
    b0 = 2 * c_w // LANES
    nbq = dil_w // LANES
    outs = ((0, c_w, F32), (c_w, c_w, F32), (2 * c_w, dil_w, F32), (2 * c_w + dil_w, 2 * dil_w, F32))
    w_out = p["w_out"]
    gw = c_w // p["ws"].shape[0]
    return dict(w=w, hg=hg.reshape(1, n), norm_blocks=tuple(range(b0, b0 + 2 * nbq)), outs=outs,
                v_gain=p["v_gain"].reshape(1, c_w), ws=p["ws"], bs_exp=jnp.repeat(p["bs"].T, gw, axis=1),
                ws_diag=jnp.repeat(p["ws"][:, 0, 0], gw).reshape(1, c_w),
                bs0=jnp.repeat(p["bs"][:, 0], gw).reshape(1, c_w),
                wo_a=w_out[:c_w].astype(BF16), wo_b=w_out[c_w:].astype(BF16))


def layer0_prompt(x, mod, ln_mix, ln_ffn, ffn_w, pp):
    b, t, _ = x.shape
    nwin = min(NSA_WINDOW, t)
    rows_spec, win_spec = pp["outs"][1], pp["outs"][3]
    q_bf, rows, rows_bf, _, win_bf, xr, gr, gl, rows_t, win_t = in_proj(
        x, mod, 0, 1, ln_mix, pp["w"], pp["hg"], pp["norm_blocks"], pp["outs"], tm=512,
        touts=((rows_spec[0], rows_spec[1], t), (win_spec[0], win_spec[1], nwin)))
    kc, vc = nsa_compress(rows, pp["cw"])
    o_nsa = nsa_prompt_attn(q_bf, kc, vc, rows_bf, win_bf, gl)
    o_rnn, h_last, conv_last = rglru_prompt(xr, gr, pp["rw"], tm=256)
    y = out_ffn(o_nsa, o_rnn, x, mod, ln_ffn, pp["wo_a"], pp["wo_b"], *ffn_w, tm=512)
    state = (jnp.transpose(rows_t.reshape(b, 4, NSA_G, HEAD_DIM, t), (0, 4, 1, 2, 3)),
             jnp.transpose(win_t.reshape(b, 2, NSA_G, HEAD_DIM, nwin), (0, 4, 1, 2, 3)),
             h_last[:, 0], conv_last[:, 5:8])
    return y, state


def layer1_prompt(x, mod, ln_mix, ln_ffn, ffn_w, pp):
    b, t, _ = x.shape
    nkv = min(DIL_MAX, t)
    kv_spec = pp["outs"][3]
    u, v, q, kv, kv_t = in_proj(x, mod, 0, 1, ln_mix, pp["w"], pp["hg"], pp["norm_blocks"], pp["outs"], tm=512,
                                touts=((kv_spec[0], kv_spec[1], nkv),))
    o_c = gmlp_prompt(u, v, pp["v_gain"], pp["ws"], pp["bs_exp"])
    o_d = dil_prompt_attn(q, kv)
    y = out_ffn(o_c, o_d, x, mod, ln_ffn, pp["wo_a"], pp["wo_b"], *ffn_w, tm=512)
    heads = kv.shape[2] // (2 * HEAD_DIM)
    return y, jnp.transpose(kv_t.reshape(b, 2, heads, HEAD_DIM, nkv), (0, 4, 1, 2, 3))


def _row_softmax_parts(s_list, mask_list, s_new):
    m = s_new[0]
    for sn in s_new[1:]:
        m = jnp.maximum(m, sn)
    for s, mk in zip(s_list, mask_list):
        sm = s if mk is None else jnp.where(mk, s, NEG_INF)
        m = jnp.maximum(m, jnp.max(sm, axis=-1, keepdims=True))
    es, den = [], 0.0
    for s, mk in zip(s_list, mask_list):
        e = jnp.exp(s - m)
        if mk is not None:
            e = jnp.where(mk, e, 0.0)
        es.append(e)
        den = den + jnp.sum(e, axis=-1, keepdims=True)
    en = [jnp.exp(sn - m) for sn in s_new]
    for e in en:
        den = den + e
    return es, en, 1.0 / den


def _nsa_decode_kernel(pt_ref, cache_ref, q_ref, rn_ref, wn_ref, wnc_ref, gl_ref, sw_ref,
                       w1k_ref, posk_ref, w2k_ref, w1v_ref, posv_ref, w2v_ref, kg_ref, ex_ref,
                       o_ref, wo_ref, buf, xk, xv, sem):
    b = pl.program_id(0)
    nbatch = pl.num_programs(0)
    slot = b % 2
    n_pages = pt_ref.shape[1]
    page = cache_ref.shape[3]
    past = n_pages * page
    nc = past // CMP_BLOCK
    nb = past // SEL_BLOCK
    scale = HEAD_DIM ** -0.5
    rep = q_ref.shape[2] // LANES
    nh = 2 * rep

    nseq = q_ref.shape[0]

    def copies(step, sl):
        return [pltpu.make_async_copy(cache_ref.at[pt_ref[step * nseq + u, j]], buf.at[sl, u, j], sem.at[sl, u, j])
                for u in range(nseq) for j in range(n_pages)]

    @pl.when(b == 0)
    def _():
        for c in copies(0, 0):
            c.start()

    @pl.when(b + 1 < nbatch)
    def _():
        for c in copies(b + 1, 1 - slot):
            c.start()

    for c in copies(b, slot):
        c.wait()

    lane = lax.broadcasted_iota(jnp.int32, (nh, LANES), 1)
    hrow = lax.broadcasted_iota(jnp.int32, (nh, LANES), 0)
    blk_t = lax.broadcasted_iota(jnp.int32, (LANES, LANES), 0)
    wb = sw_ref.shape[3]
    last = lax.broadcasted_iota(jnp.int32, (LANES, wb), 1) == wb - 1

    def padded(c2):
        z = jnp.zeros((LANES - nc // 2, LANES), F32)
        return jnp.concatenate([c2[:nc // 2], z, c2[nc // 2:], z], axis=0)

    def one_sequence(u):
        q = q_ref[u]
        qrows = []
        for g in range(2):
            for r in range(rep):
                qrows.append(q[:, r * LANES:(r + 1) * LANES])
        qm = jnp.concatenate(qrows, axis=0)
        qm = jnp.where((lane // HEAD_DIM) == (hrow // rep), qm, jnp.zeros((), qm.dtype))
        qf = qm.astype(F32)

        kwt, vwt = sw_ref[u, 0], sw_ref[u, 1]
        wn = wn_ref[u]
        s_w = _dot(qm, kwt.astype(BF16)) * scale
        s_w_new = jnp.sum(qf * wn[:, :LANES], axis=-1, keepdims=True) * scale
        yield

        for j in range(n_pages):
            xk[u, pl.ds(j * page, page), :] = buf[slot, u, j, 0].T
            xv[u, pl.ds(j * page, page), :] = buf[slot, u, j, 1].T
            yield
        ldk = lambda s, n, st: xk[u, pl.ds(s, n, stride=st), :]
        ldv = lambda s, n, st: xv[u, pl.ds(s, n, stride=st), :]
        kc = _head_norm(_compress_rows(ldk, nc // 2, w1k_ref, posk_ref, w2k_ref), kg_ref[...])
        yield
        vc = _compress_rows(ldv, nc // 2, w1v_ref, posv_ref, w2v_ref)
        yield
        kcp = padded(kc).astype(BF16)
        vcp = padded(vc).astype(BF16)
        s_c = _dot_nt(qm, kcp) * scale
        yield
        cl = lax.broadcasted_iota(jnp.int32, (nh, 2 * LANES), 1)
        cmask = (cl % LANES) < nc // 2
        s_c = jnp.where(cmask, s_c, NEG_INF)
        e_c = jnp.where(cmask, jnp.exp(s_c - jnp.max(s_c, axis=-1, keepdims=True)), 0.0)
        yield
        p_c = e_c * (1.0 / jnp.maximum(jnp.sum(e_c, axis=-1, keepdims=True), 1e-30))
        o_cmp = _dot(p_c.astype(BF16), vcp)
        yield

        pp = p_c[:, :LANES] + p_c[:, LANES:]
        imps = []
        for g in range(2):
            ig = pp[g * rep:g * rep + 1]
            for r in range(1, rep):
                ig = ig + pp[g * rep + r:g * rep + r + 1]
            imps += [ig] * rep
        imp = jnp.concatenate(imps + [jnp.zeros((LANES - nh, LANES), F32)], axis=0)
        imp_t = imp.T
        imp_t = jnp.where((blk_t == 0) | (blk_t == nb), FORCE_SCORE, imp_t)
        imp_t = jnp.where(blk_t <= nb, imp_t, -3.0)
        yield

        rn = rn_ref[u]
        kst = jnp.concatenate([buf[slot, u, j, 2] for j in range(n_pages)], axis=1).astype(BF16)
        s_sel = _dot(qm, kst) * scale
        s_sel_new = jnp.sum(qf * rn[:, 2 * LANES:3 * LANES], axis=-1, keepdims=True) * scale
        yield
        (e_w,), (e_wn,), inv_w = _row_softmax_parts([s_w], [None], [s_w_new])
        o_win = (_dot_nt(e_w.astype(BF16), vwt.astype(BF16)) + e_wn * wn[:, LANES:]) * inv_w
        yield

        work = imp_t
        for _ in range(min(N_SEL, nb + 1)):
            work = _pick_top(work, blk_t.astype(F32))
            yield
        sel_t = jnp.where(work == PICKED, 1.0, 0.0)
        selm = _dot(sel_t.T[:nh].astype(BF16), ex_ref[...]) > 0.5
        yield
        (e_s,), (e_sn,), inv_s = _row_softmax_parts([s_sel], [selm], [s_sel_new])
        yield
        vst = jnp.concatenate([buf[slot, u, j, 3] for j in range(n_pages)], axis=1).astype(BF16)
        o_sel = (_dot_nt(e_s.astype(BF16), vst) + e_sn * rn[:, 3 * LANES:4 * LANES]) * inv_s
        yield

        gate = _sigmoid(gl_ref[u])
        o = gate[:, 0:1] * o_cmp + gate[:, 1:2] * o_sel + gate[:, 2:3] * o_win
        l1 = lax.broadcasted_iota(jnp.int32, (1, LANES), 1)
        o_ref[u] = jnp.concatenate([jnp.where(l1 < HEAD_DIM, o[r:r + 1], o[rep + r:rep + r + 1])
                                    for r in range(rep)], axis=1).astype(o_ref.dtype)
        wnc = wnc_ref[u]
        wo_ref[u, 0] = jnp.where(last, wnc[:, 0:1], pltpu.roll(kwt, wb - 1, axis=1))
        wo_ref[u, 1] = jnp.where(last, wnc[:, 1:2], pltpu.roll(vwt, wb - 1, axis=1))

    phases = [one_sequence(u) for u in range(nseq)]
    while phases:
        phases = [g for g in phases if next(g, True) is None]


def nsa_decode(page_table, cache, q_bf, rows_new, win_new, gl3, state_win, cw):
    b, n_pages = page_table.shape
    n_pool, page, ncomp, ng, hd = cache.shape
    past = n_pages * page
    nb = past // SEL_BLOCK
    assert past % SEL_BLOCK == 0 and nb < LANES and past // CMP_BLOCK <= 2 * LANES and ng * hd == LANES
    wb = state_win.shape[1]
    assert wb <= NSA_WINDOW
    cache_t = jnp.transpose(cache, (0, 2, 3, 4, 1)).reshape(n_pool, ncomp, LANES, page)
    win_t = jnp.transpose(state_win, (0, 2, 3, 4, 1)).reshape(b, 2, LANES, wb)
    win_col = jnp.transpose(win_new.reshape(b, 2, LANES), (0, 2, 1))
    expand = (jnp.arange(LANES)[:, None] == (jnp.arange(past) // SEL_BLOCK)[None, :]).astype(BF16)
    ws = (cw["w1k"], cw["posk"], cw["w2k"], cw["w1v"], cw["posv"], cw["w2v"], cw["kg0"], expand)
    ns = NSA_DECODE_SEQS
    assert b % ns == 0
    per_b = lambda a: pl.BlockSpec((ns,) + a.shape[1:], lambda i, pt: (i,) + (0,) * (a.ndim - 1))
    const = lambda a: pl.BlockSpec(a.shape, lambda i, pt: (0,) * a.ndim)
    grid_spec = pltpu.PrefetchScalarGridSpec(
        num_scalar_prefetch=1,
        grid=(b // ns,),
        in_specs=[pl.BlockSpec(memory_space=pl.ANY), per_b(q_bf), per_b(rows_new), per_b(win_new), per_b(win_col),
                  per_b(gl3), per_b(win_t)] + [const(a) for a in ws],
        out_specs=[pl.BlockSpec((ns, 1, q_bf.shape[2]), lambda i, pt: (i, 0, 0)),
                   pl.BlockSpec((ns, 2, LANES, wb), lambda i, pt: (i, 0, 0, 0))],
        scratch_shapes=[pltpu.VMEM((2, ns, n_pages, ncomp, LANES, page), F32), pltpu.VMEM((ns, past, LANES), F32),
                        pltpu.VMEM((ns, past, LANES), F32), pltpu.SemaphoreType.DMA((2, ns, n_pages))],
    )
    o, win_out_t = pl.pallas_call(
        _nsa_decode_kernel,
        grid_spec=grid_spec,
        out_shape=[jax.ShapeDtypeStruct((b, 1, q_bf.shape[2]), BF16),
                   jax.ShapeDtypeStruct(win_t.shape, F32)],
        compiler_params=_cparams("arbitrary"),
        name="nsa_decode",
    )(page_table, cache_t, q_bf, rows_new, win_new, win_col, gl3, win_t, *ws)
    return o, jnp.transpose(win_out_t.reshape(b, 2, ng, hd, wb), (0, 4, 1, 2, 3))


def _rglru_decode_kernel(xr_ref, gr_ref, cs_ref, h0_ref, cw_ref, cb_ref, wa_ref, ba_ref, wx_ref, bx_ref, lam_ref,
                         o_ref, h_ref, cn_ref):
    x = xr_ref[...]
    xc = cb_ref[...] + cw_ref[3:4, :] * x
    for k in range(3):
        xc = xc + cw_ref[k:k + 1, :] * cs_ref[k]
    a, u = _rglru_gates(xc, wa_ref, ba_ref, wx_ref, bx_ref, lam_ref)
    h = a * h0_ref[...] + u
    h_ref[...] = h
    o_ref[...] = (h * _gelu(gr_ref[...])).astype(o_ref.dtype)
    cn_ref[0] = cs_ref[1]
    cn_ref[1] = cs_ref[2]
    cn_ref[2] = x


def rglru_decode(xr, gr, conv_t, h0, rw):
    ws = (rw["conv_w"], rw["conv_b"], rw["wa"], rw["ba"], rw["wx"], rw["bx"], rw["lam"])
    return pl.pallas_call(
        _rglru_decode_kernel,
        out_shape=[jax.ShapeDtypeStruct(xr.shape, BF16), jax.ShapeDtypeStruct(xr.shape, F32),
                   jax.ShapeDtypeStruct(conv_t.shape, F32)],
        compiler_params=pltpu.CompilerParams(vmem_limit_bytes=VMEM_LIMIT),
        name="rglru_decode",
    )(xr, gr, conv_t, h0, *ws)


def _gmlp_decode_kernel(u_ref, v_ref, vg_ref, wd_ref, b0_ref, o_ref, vn_ref):
    v = _gmlp_v(v_ref[...], vg_ref)
    vn_ref[...] = v
    o_ref[...] = (_gelu(u_ref[...]) * (wd_ref[...] * v + b0_ref[...])).astype(o_ref.dtype)


def gmlp_decode(u, v, v_gain, ws_diag, bs0):
    return pl.pallas_call(
        _gmlp_decode_kernel,
        out_shape=[jax.ShapeDtypeStruct(u.shape, BF16), jax.ShapeDtypeStruct(u.shape, F32)],
        compiler_params=pltpu.CompilerParams(vmem_limit_bytes=VMEM_LIMIT),
        name="gmlp_decode",
    )(u, v, v_gain, ws_diag, bs0)


def _dil_decode_kernel(q_ref, kvn_ref, st_ref, o_ref, so_ref):
    nh, hd, wb = st_ref.shape[2], st_ref.shape[3], st_ref.shape[4]
    scale = HEAD_DIM ** -0.5
    n_pat = len(DIL_PATTERNS)
    lane1 = lax.broadcasted_iota(jnp.int32, (1, wb), 1)
    dist = wb - lane1
    cnt = jnp.zeros((1, wb), F32)
    for window, dil in DIL_PATTERNS:
        cnt = cnt + jnp.where((dist <= window) & ((dist & (dil - 1)) == 0), 1.0, 0.0)
    last = lax.broadcasted_iota(jnp.int32, (hd, wb), 1) == wb - 1

    def shifted(x, new_col):
        return jnp.where(last, new_col, pltpu.roll(x, wb - 1, axis=1))

    qt = q_ref[0] * scale
    kvn = kvn_ref[0]
    s_rows, s_new_rows = [], []
    for h in range(nh):
        kh = st_ref[0, 0, h]
        qh = qt[:, h:h + 1]
        kn = kvn[:, h:h + 1]
        s_rows.append(jnp.sum(kh * qh, axis=0, keepdims=True))
        s_new_rows.append(jnp.sum(kn * qh, axis=0, keepdims=True))
        so_ref[0, 0, h] = shifted(kh, kn)
    s = jnp.concatenate(s_rows, axis=0)
    s_new = jnp.concatenate(s_new_rows, axis=0)
    s = jnp.where(cnt > 0.0, s, NEG_INF)
    m = jnp.maximum(jnp.max(s, axis=-1, keepdims=True), s_new)
    e = cnt * jnp.exp(s - m)
    e_new = n_pat * jnp.exp(s_new - m)
    inv = 1.0 / (jnp.sum(e, axis=-1, keepdims=True) + e_new)
    for h in range(nh):
        vh = st_ref[0, 1, h]
        vn = kvn[:, nh + h:nh + h + 1]
        acc = jnp.sum(vh * e[h:h + 1, :], axis=1, keepdims=True) + e_new[h:h + 1, :] * vn
        o_ref[0, :, h:h + 1] = acc * inv[h:h + 1, :]
        so_ref[0, 1, h] = shifted(vh, vn)


def dil_decode(q, kv_new, state):
    b, wb, _, nh, hd = state.shape
    assert wb == DIL_MAX
    st_t = jnp.transpose(state, (0, 2, 3, 4, 1))
    q_t = jnp.transpose(q, (0, 2, 1))
    kvn_t = jnp.transpose(kv_new.reshape(b, 2 * nh, hd), (0, 2, 1))
    o_t, so_t = pl.pallas_call(
        _dil_decode_kernel,
        grid=(b,),
        in_specs=[pl.BlockSpec((1, hd, nh), lambda i: (i, 0, 0)),
                  pl.BlockSpec((1, hd, 2 * nh), lambda i: (i, 0, 0)),
                  pl.BlockSpec((1, 2, nh, hd, wb), lambda i: (i, 0, 0, 0, 0))],
        out_specs=[pl.BlockSpec((1, hd, nh), lambda i: (i, 0, 0)),
                   pl.BlockSpec((1, 2, nh, hd, wb), lambda i: (i, 0, 0, 0, 0))],
        out_shape=[jax.ShapeDtypeStruct((b, hd, nh), F32), jax.ShapeDtypeStruct(st_t.shape, F32)],
        compiler_params=_cparams("arbitrary"),
        name="dil_decode",
    )(q_t, kvn_t, st_t)
    return jnp.transpose(o_t, (0, 2, 1)), jnp.transpose(so_t, (0, 4, 1, 2, 3))


def layer0_sample(x, mod, ln_mix, ln_ffn, ffn_w, pp, cache, page_table, state_win, state_h, state_conv):
    b, _, d = x.shape
    xs = x.reshape(1, b, d)
    q_bf, rows, _, win, _, xr, gr, gl = in_proj(
        xs, mod, 0, 1, ln_mix, pp["w"], pp["hg"], pp["norm_blocks"], pp["outs"], tm=b)
    n_gate = NSA_G * NSA_REP * 3
    gl3 = gl[0, :, :n_gate].reshape(b, NSA_G * NSA_REP, 3)
    o_nsa, win_out = nsa_decode(page_table, cache, q_bf.reshape(b, 1, -1), rows.reshape(b, 1, -1),
                                win.reshape(b, 1, -1), gl3, state_win, pp["cw"])
    o_rnn, h_new, conv_new = rglru_decode(xr[0], gr[0], state_conv.transpose(1, 0, 2), state_h, pp["rw"])
    y = out_ffn(o_nsa.reshape(1, b, -1), o_rnn[None], xs, mod, ln_ffn, pp["wo_a"], pp["wo_b"], *ffn_w, tm=b)
    state = (rows.reshape(b, 1, 4, NSA_G, HEAD_DIM), win_out, h_new, conv_new.transpose(1, 0, 2))
    return y.reshape(b, 1, d), state


def layer1_sample(x, mod, ln_mix, ln_ffn, ffn_w, pp, state_dil):
    b, _, d = x.shape
    xs = x.reshape(1, b, d)
    u, v, q, kv = in_proj(xs, mod, 0, 1, ln_mix, pp["w"], pp["hg"], pp["norm_blocks"], pp["outs"], tm=b)
    o_c, v_n = gmlp_decode(u[0], v[0], pp["v_gain"], pp["ws_diag"], pp["bs0"])
    heads = state_dil.shape[3]
    o_d, dil_out = dil_decode(q.reshape(b, heads, HEAD_DIM), kv.reshape(b, 2, heads, HEAD_DIM), state_dil)
    y = out_ffn(o_c[None], o_d.reshape(1, b, -1).astype(BF16), xs, mod, ln_ffn, pp["wo_a"], pp["wo_b"], *ffn_w,
                tm=b)
    return y.reshape(b, 1, d), (dil_out, v_n.reshape(b, 1, -1))


def kernel(x_prompt, x_sample, cache_nsa_kv, state_nsa_win, state_rglru_h, state_rglru_conv, state_dil_kv, page_table, c_prompt, c_sample, norm_mix_g, norm_ffn_g, w_ada, b_ada, w_ffn_gate, w_ffn_up, w_ffn_down, w_in_ab, w_out_ab, nsa_q_gain, nsa_k_gain, nsa_cmp_w1, nsa_cmp_w2, nsa_cmp_pos, rg_conv_w, rg_conv_b, rg_wa, rg_ba, rg_wx, rg_bx, rg_lambda, w_in_cd, w_out_cd, gmlp_v_gain, gmlp_ws, gmlp_bs, dil_q_gain, dil_k_gain):
    depth = norm_mix_g.shape[0]
    bp, bs = x_prompt.shape[0], x_sample.shape[0]
    pad = -(bp + bs) % 8
    c_all = jnp.concatenate([c_prompt, c_sample, jnp.zeros((pad, c_prompt.shape[1]), F32)], axis=0)
    mod_all = ada_mod(c_all, w_ada.astype(BF16), b_ada)
    yp, ys = x_prompt, x_sample
    kv_p, kv_s, win_p, win_s, h_p, h_s, conv_p, conv_s, dil_p, dil_s, gv_s = ([] for _ in range(11))
    for layer in range(depth):
        i = layer // 2
        mod_p = mod_all[layer, :bp, None, :]
        mod_s = mod_all[layer, None, bp:bp + bs, :]
        ffn_w = (w_ffn_gate[layer].astype(BF16), w_ffn_up[layer].astype(BF16), w_ffn_down[layer].astype(BF16))
        if layer % 2 == 0:
            pp = prep_layer0(dict(w_in=w_in_ab[i], w_out=w_out_ab[i], q_gain=nsa_q_gain[i], k_gain=nsa_k_gain[i],
                                  cmp_w1=nsa_cmp_w1[i], cmp_w2=nsa_cmp_w2[i], cmp_pos=nsa_cmp_pos[i],
                                  conv_w=rg_conv_w[i], conv_b=rg_conv_b[i], wa=rg_wa[i], ba=rg_ba[i],
                                  wx=rg_wx[i], bx=rg_bx[i], lam=rg_lambda[i]))
            yp, st = layer0_prompt(yp, mod_p, norm_mix_g[layer], norm_ffn_g[layer], ffn_w, pp)
            kv_p.append(st[0]); win_p.append(st[1]); h_p.append(st[2]); conv_p.append(st[3])
            ys, st = layer0_sample(ys, mod_s, norm_mix_g[layer], norm_ffn_g[layer], ffn_w, pp, cache_nsa_kv[i],
                                   page_table, state_nsa_win[i], state_rglru_h[i], state_rglru_conv[i])
            kv_s.append(st[0]); win_s.append(st[1]); h_s.append(st[2]); conv_s.append(st[3])
        else:
            pp = prep_layer1(dict(w_in=w_in_cd[i], w_out=w_out_cd[i], v_gain=gmlp_v_gain[i], ws=gmlp_ws[i],
                                  bs=gmlp_bs[i], q_gain=dil_q_gain[i], k_gain=dil_k_gain[i]))
            yp, st = layer1_prompt(yp, mod_p, norm_mix_g[layer], norm_ffn_g[layer], ffn_w, pp)
            dil_p.append(st)
            ys, st = layer1_sample(ys, mod_s, norm_mix_g[layer], norm_ffn_g[layer], ffn_w, pp, state_dil_kv[i])
            dil_s.append(st[0]); gv_s.append(st[1])
    return (yp, ys, jnp.stack(kv_p), jnp.stack(kv_s), jnp.stack(win_p), jnp.stack(win_s),
            jnp.stack(h_p), jnp.stack(h_s), jnp.stack(conv_p), jnp.stack(conv_s),
            jnp.stack(dil_p), jnp.stack(dil_s), jnp.stack(gv_s))
```

```python
import functools

import numpy as np
import jax
import jax.numpy as jnp
from jax import lax
from jax.experimental import pallas as pl
from jax.experimental.pallas import tpu as pltpu

F32 = jnp.float32
BF16 = jnp.bfloat16

LANES = 128
HEAD_DIM = 64
QBLK = 128
CMP_BLOCK = 32
SEL_BLOCK = 64
N_SEL = 16
NSA_WINDOW = 512
FORCE_SCORE = 1.0e4
DIL_PATTERNS = ((128, 1), (512, 4), (2048, 16))
DIL_MAX = 2048
RG_C = 8.0
RMS_EPS = 1e-6
NEG_INF = -1e30
LOG2_E = 1.4426950408889634
VMEM_LIMIT = 56 * 1024 * 1024


def _cparams(*sem):
    return pltpu.CompilerParams(dimension_semantics=sem, vmem_limit_bytes=VMEM_LIMIT)


def _dot(a, b):
    return jnp.dot(a, b, preferred_element_type=F32)


def _dot_nt(a, b):
    return lax.dot_general(a, b, (((1,), (1,)), ((), ())), preferred_element_type=F32)


def _gelu(x):
    return 0.5 * x * (1.0 + jnp.tanh(np.sqrt(2.0 / np.pi) * (x + 0.044715 * (x * x * x))))


def _sigmoid(x):
    return 1.0 / (1.0 + jnp.exp(-x))


def _head_norm(z, gain):
    lo = lax.broadcasted_iota(jnp.int32, z.shape, 1) < HEAD_DIM
    z2 = z * z
    s_lo = jnp.sum(jnp.where(lo, z2, 0.0), axis=-1, keepdims=True)
    s_hi = jnp.sum(jnp.where(lo, 0.0, z2), axis=-1, keepdims=True)
    inv = lax.rsqrt(jnp.where(lo, s_lo, s_hi) * (1.0 / HEAD_DIM) + RMS_EPS)
    return z * inv * gain


def _ada_kernel(c_ref, w_ref, b_ref, o_ref):
    c = c_ref[...]
    s = c * _sigmoid(c)
    o_ref[0] = _dot(s.astype(BF16), w_ref[0].astype(BF16)) + b_ref[0]


def ada_mod(c_all, w_ada, b_ada):
    m, d = c_all.shape
    nl, _, n = w_ada.shape
    tn = 1536
    return pl.pallas_call(
        _ada_kernel,
        grid=(nl, n // tn),
        in_specs=[pl.BlockSpec((m, d), lambda l, j: (0, 0)),
                  pl.BlockSpec((1, d, tn), lambda l, j: (l, 0, j)),
                  pl.BlockSpec((1, 1, tn), lambda l, j: (l, 0, j))],
        out_specs=pl.BlockSpec((1, m, tn), lambda l, j: (l, 0, j)),
        out_shape=jax.ShapeDtypeStruct((nl, m, n), F32),
        compiler_params=_cparams("arbitrary", "arbitrary"),
        name="ada_mod",
    )(c_all, w_ada, b_ada.reshape(nl, 1, n))


def _in_proj_kernel(x_ref, sc_ref, sh_ref, g_ref, w_ref, hg_ref, *out_refs, norm_blocks, outs, touts):
    i = pl.program_id(1)
    x = x_ref[0]
    ms = jnp.mean(x * x, axis=-1, keepdims=True)
    h = x * lax.rsqrt(ms + RMS_EPS) * g_ref[...]
    h = h * (1.0 + sc_ref[0]) + sh_ref[0]
    z = _dot(h.astype(BF16), w_ref[...])
    nblk = z.shape[1] // LANES
    blocks = []
    for j in range(nblk):
        zb = z[:, j * LANES:(j + 1) * LANES]
        if j in norm_blocks:
            zb = _head_norm(zb, hg_ref[:, j * LANES:(j + 1) * LANES])
        blocks.append(zb)
    for o_ref, (c0, width, _) in zip(out_refs, outs):
        for j in range(width // LANES):
            o_ref[0, :, j * LANES:(j + 1) * LANES] = blocks[c0 // LANES + j].astype(o_ref.dtype)
    for o_ref, (c0, width, first) in zip(out_refs[len(outs):], touts):
        @pl.when(i >= first)
        def _(o_ref=o_ref, c0=c0, width=width):
            for j in range(width // LANES):
                o_ref[0, j * LANES:(j + 1) * LANES, :] = blocks[c0 // LANES + j].T


def in_proj(x, mod, sh_idx, sc_idx, g, w, head_gain, norm_blocks, outs, tm, touts=()):
    bm, t, d = x.shape
    r = mod.shape[1]
    n = w.shape[1]
    tm = min(tm, t)
    if r == 1:
        mod_spec = lambda k: pl.BlockSpec((1, 1, d), lambda b, i: (b, 0, k))
    else:
        assert r == t and tm == t
        mod_spec = lambda k: pl.BlockSpec((1, tm, d), lambda b, i: (b, 0, k))
    tfirst = []
    for (_, _, npos) in touts:
        assert npos % tm == 0 and npos <= t
        tfirst.append((t - npos) // tm)
    kern = functools.partial(_in_proj_kernel, norm_blocks=tuple(norm_blocks), outs=tuple(outs),
                             touts=tuple((c0, wd, f) for (c0, wd, _), f in zip(touts, tfirst)))
    return pl.pallas_call(
        kern,
        grid=(bm, t // tm),
        in_specs=[pl.BlockSpec((1, tm, d), lambda b, i: (b, i, 0)),
                  mod_spec(sc_idx), mod_spec(sh_idx),
                  pl.BlockSpec((1, d), lambda b, i: (0, 0)),
                  pl.BlockSpec((d, n), lambda b, i: (0, 0)),
                  pl.BlockSpec((1, n), lambda b, i: (0, 0))],
        out_specs=[pl.BlockSpec((1, tm, wd), lambda b, i: (b, i, 0)) for (_, wd, _) in outs]
        + [pl.BlockSpec((1, wd, tm), lambda b, i, f=f: (b, 0, jnp.maximum(i - f, 0)))
           for (_, wd, _), f in zip(touts, tfirst)],
        out_shape=[jax.ShapeDtypeStruct((bm, t, wd), dt) for (_, wd, dt) in outs]
        + [jax.ShapeDtypeStruct((bm, wd, npos), F32) for (_, wd, npos) in touts],
        compiler_params=_cparams("arbitrary", "arbitrary"),
        name="in_proj",
    )(x, mod, mod, g.reshape(1, d), w, head_gain)


def _out_ffn_kernel(ma_ref, mb_ref, x_ref, gm_ref, shf_ref, scf_ref, gf_ref, lnf_ref,
                    woa_ref, wob_ref, wg_ref, wu_ref, wd_ref, o_ref, *, hidden_chunk):
    mix = _dot(ma_ref[0], woa_ref[...]) + _dot(mb_ref[0], wob_ref[...])
    x1 = x_ref[0] + gm_ref[0] * mix
    ms = jnp.mean(x1 * x1, axis=-1, keepdims=True)
    hf = x1 * lax.rsqrt(ms + RMS_EPS) * lnf_ref[...]
    hf = (hf * (1.0 + scf_ref[0]) + shf_ref[0]).astype(BF16)
    hidden = wg_ref.shape[1]
    ffn = jnp.zeros(x1.shape, F32)
    for c0 in range(0, hidden, hidden_chunk):
        gt = _dot(hf, wg_ref[:, c0:c0 + hidden_chunk])
        up = _dot(hf, wu_ref[:, c0:c0 + hidden_chunk])
        act = (gt * _sigmoid(gt) * up).astype(BF16)
        ffn = ffn + _dot(act, wd_ref[c0:c0 + hidden_chunk, :])
    o_ref[0] = x1 + gf_ref[0] * ffn


def out_ffn(mix_a, mix_b, x, mod, ln_ffn, wo_a, wo_b, wg, wu, wd, tm):
    bm, t, d = x.shape
    r = mod.shape[1]
    tm = min(tm, t)
    ka, kb = mix_a.shape[2], mix_b.shape[2]
    hidden = wg.shape[1]
    if r == 1:
        mod_spec = lambda k: pl.BlockSpec((1, 1, d), lambda b, i: (b, 0, k))
    else:
        assert r == t and tm == t
        mod_spec = lambda k: pl.BlockSpec((1, tm, d), lambda b, i: (b, 0, k))
    const = lambda shape: pl.BlockSpec(shape, lambda b, i: (0,) * len(shape), pipeline_mode=pl.Buffered(1))
    kern = functools.partial(_out_ffn_kernel, hidden_chunk=hidden // 2)
    return pl.pallas_call(
        kern,
        grid=(bm, t // tm),
        in_specs=[pl.BlockSpec((1, tm, ka), lambda b, i: (b, i, 0)),
                  pl.BlockSpec((1, tm, kb), lambda b, i: (b, i, 0)),
                  pl.BlockSpec((1, tm, d), lambda b, i: (b, i, 0)),
                  mod_spec(2), mod_spec(3), mod_spec(4), mod_spec(5),
                  const((1, d)), const((ka, d)), const((kb, d)),
                  const((d, hidden)), const((d, hidden)), const((hidden, d))],
        out_specs=pl.BlockSpec((1, tm, d), lambda b, i: (b, i, 0)),
        out_shape=jax.ShapeDtypeStruct((bm, t, d), F32),
        compiler_params=_cparams("arbitrary", "arbitrary"),
        name="out_ffn",
    )(mix_a, mix_b, x, mod, mod, mod, mod, ln_ffn.reshape(1, d), wo_a, wo_b, wg, wu, wd)


def _compress_rows(load, nh, w1_ref, pos_ref, w2_ref):
    halves = []
    for parity in range(2):
        halves.append(jnp.concatenate(
            [load(parity * CMP_BLOCK + l, nh, 2 * CMP_BLOCK) for l in range(CMP_BLOCK)], axis=1))
    xs = (jnp.concatenate(halves, axis=0) + pos_ref[...]).astype(BF16)
    hid = _gelu(_dot(xs, w1_ref[...]))
    return _dot(hid.astype(BF16), w2_ref[...])


def _nsa_compress_kernel(rk_ref, rv_ref, w1k_ref, posk_ref, w2k_ref, w1v_ref, posv_ref, w2v_ref, kg_ref,
                         kc_ref, vc_ref):
    nh = kc_ref.shape[1] // 2
    ldk = lambda s, n, st: rk_ref[0, pl.ds(s, n, stride=st), :]
    ldv = lambda s, n, st: rv_ref[0, pl.ds(s, n, stride=st), :]
    kc = _compress_rows(ldk, nh, w1k_ref, posk_ref, w2k_ref)
    kc_ref[0] = _head_norm(kc, kg_ref[...]).astype(kc_ref.dtype)
    vc_ref[0] = _compress_rows(ldv, nh, w1v_ref, posv_ref, w2v_ref).astype(vc_ref.dtype)


def nsa_compress(rows, cw):
    b, t, _ = rows.shape
    nc = t // CMP_BLOCK
    const = lambda a: pl.BlockSpec(a.shape, lambda i: (0,) * a.ndim)
    ws = (cw["w1k"], cw["posk"], cw["w2k"], cw["w1v"], cw["posv"], cw["w2v"], cw["kg0"])
    return pl.pallas_call(
        _nsa_compress_kernel,
        grid=(b,),
        in_specs=[pl.BlockSpec((1, t, LANES), lambda i: (i, 0, 0)),
                  pl.BlockSpec((1, t, LANES), lambda i: (i, 0, 1))] + [const(a) for a in ws],
        out_specs=[pl.BlockSpec((1, nc, LANES), lambda i: (i, 0, 0))] * 2,
        out_shape=[jax.ShapeDtypeStruct((b, nc, LANES), BF16)] * 2,
        compiler_params=_cparams("arbitrary"),
        name="nsa_compress",
    )(rows, rows, *ws)


PICKED = -2.0


def _pick_top(work, blk_f):
    top = jnp.max(work, axis=0, keepdims=True)
    first = jnp.min(jnp.where(work == top, blk_f, 1e9), axis=0, keepdims=True)
    return jnp.where(blk_f == first, PICKED, work)


def _select_blocks_t(imp, blk_f, n_pick):
    work = imp
    for _ in range(n_pick):
        work = _pick_top(work, blk_f)
    return work == PICKED


NSA_SEL_CHUNK = 8
NSA_SUB = 256
NSA_DECODE_SEQS = 4

def _dot_tn(a, b):
    return lax.dot_general(a, b, (((0,), (0,)), ((), ())), preferred_element_type=F32)


def _nsa_prompt_kernel(q_ref, kc_ref, vc_ref, ks_ref, vs_ref, kw_ref, vw_ref, gl_ref, e_ref, o_ref,
                       s_ref, mx_ref, m_ref, l_ref, acc_ref):
    qi = pl.program_id(1)
    t0 = qi * QBLK
    nc = kc_ref.shape[1]
    nb = nc // 2
    scale = HEAD_DIM ** -0.5
    rep = q_ref.shape[2] // LANES
    nhead = 2 * rep
    ncol = nhead * QBLK

    lane = lax.broadcasted_iota(jnp.int32, (QBLK, LANES), 1)
    sub = lax.broadcasted_iota(jnp.int32, (QBLK, LANES), 0)
    tile8 = lambda x: jnp.concatenate([x] * nhead, axis=1)
    q = q_ref[0]
    qs = []
    for g in range(2):
        gm = (lane >= g * HEAD_DIM) & (lane < (g + 1) * HEAD_DIM)
        for r in range(rep):
            qs.append(jnp.where(gm, q[:, r * LANES:(r + 1) * LANES], jnp.zeros((), q.dtype)))
    qall = (jnp.concatenate(qs, axis=0).astype(F32) * (scale * LOG2_E)).astype(BF16)

    crow = lax.broadcasted_iota(jnp.int32, (nc, QBLK), 0)
    cidx = jnp.where(crow < nb, 2 * crow, 2 * (crow - nb) + 1)
    cvalid = ((cidx + 1) * CMP_BLOCK - 1) <= t0 + lax.broadcasted_iota(jnp.int32, (nc, QBLK), 1)
    cbias = tile8(jnp.where(cvalid, 0.0, NEG_INF))
    cone = tile8(jnp.where(cvalid, 1.0, 0.0))
    wkeys = NSA_WINDOW + QBLK
    wstart = jnp.maximum(qi - NSA_WINDOW // QBLK, 0) * QBLK
    dist = (t0 + lax.broadcasted_iota(jnp.int32, (wkeys, QBLK), 1)) \
        - (wstart + lax.broadcasted_iota(jnp.int32, (wkeys, QBLK), 0))
    wbias = tile8(jnp.where((dist >= 0) & (dist <= NSA_WINDOW), 0.0, NEG_INF))
    wrows = pl.ds(pl.multiple_of(wstart, QBLK), wkeys)
    sc = _dot_nt(kc_ref[0], qall) + cbias
    sw = _dot_nt(kw_ref[0, wrows, :], qall) + wbias
    mc = jnp.max(sc, axis=0, keepdims=True)
    mw = jnp.max(sw, axis=0, keepdims=True)
    ec = jnp.exp2(sc - mc) * cone
    ew = jnp.exp2(sw - mw)
    p = ec * (1.0 / jnp.maximum(jnp.sum(ec, axis=0, keepdims=True), 1e-30))
    lw = jnp.sum(ew, axis=0, keepdims=True)
    o_cmp = _dot_tn(vc_ref[0], p.astype(BF16))
    o_win = _dot_tn(vw_ref[0, wrows, :], ew.astype(BF16)) * (1.0 / lw)

    brow = lax.broadcasted_iota(jnp.int32, (nb, 2 * QBLK), 0)
    tb = t0 + lax.broadcasted_iota(jnp.int32, (nb, 2 * QBLK), 1) % QBLK
    imps = []
    for g in range(2):
        ps = p[:, (g * rep) * QBLK:(g * rep + 1) * QBLK]
        for r in range(1, rep):
            ps = ps + p[:, (g * rep + r) * QBLK:(g * rep + r + 1) * QBLK]
        imps.append(ps[:nb] + ps[nb:])
    imp = jnp.concatenate(imps, axis=1)
    imp = jnp.where((brow == 0) | (brow == tb // SEL_BLOCK), FORCE_SCORE, imp)
    imp = jnp.where(brow * SEL_BLOCK <= tb, imp, -1.0)
    sel = _select_blocks_t(imp, brow.astype(F32), min(N_SEL, nb))
    bias_t = jnp.where(sel, 0.0, NEG_INF)
    if nb < LANES:
        bias_t = jnp.concatenate([bias_t, jnp.zeros((LANES - nb, 2 * QBLK), F32)], axis=0)
    qbias = []
    for g in range(2):
        qbias += [bias_t[:, g * QBLK:(g + 1) * QBLK].T.astype(BF16)] * rep
    qaug = jnp.concatenate([qall, jnp.concatenate(qbias, axis=0)], axis=1)

    chunk = NSA_SEL_CHUNK * QBLK
    nsub = chunk // NSA_SUB
    sub_rows = lambda kc, j: pl.ds(pl.multiple_of(kc * chunk + j * NSA_SUB, NSA_SUB), NSA_SUB)
    neg_row = jnp.full((1, ncol), NEG_INF, F32)

    def scores(kc, j):
        r = sub_rows(kc, j)
        return _dot_nt(jnp.concatenate([ks_ref[0, r, :], e_ref[r, :]], axis=1), qaug)

    def rescale(mx):
        m_old = m_ref[...]
        m_new = jnp.maximum(m_old, mx)
        alpha = jnp.exp2(m_old - m_new)
        m_ref[...] = m_new
        l_ref[...] = alpha * l_ref[...]
        acc_ref[...] = alpha * acc_ref[...]
        return m_new

    def consume(s, v, m_new):
        e = jnp.exp2(s - m_new)
        l_ref[...] += jnp.sum(e, axis=0, keepdims=True)
        acc_ref[...] += _dot_tn(v, e.astype(BF16))

    def stage(kc, cur, nxt):
        m_new = rescale(mx_ref[cur])
        mx = neg_row
        for j in range(nsub):
            s_next = scores(kc + 1, j)
            s_ref[nxt, j * NSA_SUB:(j + 1) * NSA_SUB, :] = s_next
            mx = jnp.maximum(mx, jnp.max(s_next, axis=0, keepdims=True))
            consume(s_ref[cur, j * NSA_SUB:(j + 1) * NSA_SUB, :], vs_ref[0, sub_rows(kc, j), :], m_new)
        mx_ref[nxt] = mx

    m_ref[...] = neg_row
    l_ref[...] = jnp.zeros((1, ncol), F32)
    acc_ref[...] = jnp.zeros((LANES, ncol), F32)
    mx = neg_row
    for j in range(nsub):
        s0 = scores(0, j)
        s_ref[0, j * NSA_SUB:(j + 1) * NSA_SUB, :] = s0
        mx = jnp.maximum(mx, jnp.max(s0, axis=0, keepdims=True))
    mx_ref[0] = mx

    last = qi // NSA_SEL_CHUNK

    def stage_pair(pair, carry):
        stage(2 * pair, 0, 1)
        stage(2 * pair + 1, 1, 0)
        return carry

    lax.fori_loop(0, last // 2, stage_pair, 0)

    @pl.when(last % 2 == 1)
    def _():
        stage(last - 1, 0, 1)

    slot = last % 2
    kpos = last * chunk + lax.broadcasted_iota(jnp.int32, (chunk, QBLK), 0)
    causal = jnp.where(kpos <= t0 + lax.broadcasted_iota(jnp.int32, (chunk, QBLK), 1), 0.0, NEG_INF)
    s_last = [s_ref[slot, j * NSA_SUB:(j + 1) * NSA_SUB, :] + tile8(causal[j * NSA_SUB:(j + 1) * NSA_SUB])
              for j in range(nsub)]
    mx = neg_row
    for s in s_last:
        mx = jnp.maximum(mx, jnp.max(s, axis=0, keepdims=True))
    m_new = rescale(mx)
    q_end = (qi % NSA_SEL_CHUNK + 1) * QBLK
    consume(s_last[0], vs_ref[0, sub_rows(last, 0), :], m_new)
    for j in range(1, nsub):
        @pl.when(j * NSA_SUB < q_end)
        def _(j=j):
            consume(s_last[j], vs_ref[0, sub_rows(last, j), :], m_new)
    o_sel = acc_ref[...] * (1.0 / l_ref[...])

    gate = _sigmoid(gl_ref[0]).T
    for r in range(rep):
        og = []
        for g in range(2):
            c = (g * rep + r) * 3
            cols = slice((g * rep + r) * QBLK, (g * rep + r + 1) * QBLK)
            og.append(gate[c:c + 1] * o_cmp[:, cols] + gate[c + 1:c + 2] * o_sel[:, cols]
                      + gate[c + 2:c + 3] * o_win[:, cols])
        o_ref[0, :, r * LANES:(r + 1) * LANES] = jnp.where(sub < HEAD_DIM, og[0], og[1]).T.astype(o_ref.dtype)


def nsa_prompt_attn(q_bf, kc, vc, rows_bf, win_bf, gl):
    b, t, qw = q_bf.shape
    nc = kc.shape[1]
    nq = t // QBLK
    assert t % (NSA_SEL_CHUNK * QBLK) == 0 and t >= NSA_WINDOW + QBLK and t // SEL_BLOCK <= LANES
    ncol = 2 * (qw // LANES) * QBLK
    full = lambda k: pl.BlockSpec((1, t, LANES), lambda i, j: (i, 0, k))
    member = (jnp.arange(t)[:, None] // SEL_BLOCK == jnp.arange(LANES)[None, :]).astype(BF16)
    return pl.pallas_call(
        _nsa_prompt_kernel,
        grid=(b, nq),
        in_specs=[pl.BlockSpec((1, QBLK, qw), lambda i, j: (i, j, 0)),
                  pl.BlockSpec((1, nc, LANES), lambda i, j: (i, 0, 0)),
                  pl.BlockSpec((1, nc, LANES), lambda i, j: (i, 0, 0)),
                  full(2), full(3), full(0), full(1),
                  pl.BlockSpec((1, QBLK, LANES), lambda i, j: (i, j, 0)),
                  pl.BlockSpec((t, LANES), lambda i, j: (0, 0))],
        out_specs=pl.BlockSpec((1, QBLK, qw), lambda i, j: (i, j, 0)),
        out_shape=jax.ShapeDtypeStruct((b, t, qw), BF16),
        scratch_shapes=[pltpu.VMEM((2, NSA_SEL_CHUNK * QBLK, ncol), F32),
                        pltpu.VMEM((2, 1, ncol), F32),
                        pltpu.VMEM((1, ncol), F32), pltpu.VMEM((1, ncol), F32),
                        pltpu.VMEM((LANES, ncol), F32)],
        compiler_params=_cparams("arbitrary", "arbitrary"),
        name="nsa_prompt_attn",
    )(q_bf, kc, vc, rows_bf, rows_bf, win_bf, win_bf, gl, member)


def _rglru_gates(xc, wa_ref, ba_ref, wx_ref, bx_ref, lam_ref):
    xb = xc.astype(BF16)
    r = _sigmoid(_dot(xb, wa_ref[...]) + ba_ref[...])
    i = _sigmoid(_dot(xb, wx_ref[...]) + bx_ref[...])
    nl = -lam_ref[...]
    softplus = jnp.maximum(nl, 0.0) + jnp.log1p(jnp.exp(-jnp.abs(nl)))
    log_a = -RG_C * r * softplus
    a = jnp.exp(log_a)
    u = jnp.sqrt(-jnp.tanh(log_a) * (a * a + 1.0)) * (i * xc)
    return a, u


def _rglru_prompt_kernel(xr_ref, gr_ref, cw_ref, cb_ref, wa_ref, ba_ref, wx_ref, bx_ref, lam_ref,
                         o_ref, hl_ref, cl_ref, hcar, xcar):
    i = pl.program_id(1)
    tm, c = xr_ref.shape[1], xr_ref.shape[2]

    @pl.when(i == 0)
    def _():
        hcar[...] = jnp.zeros(hcar.shape, F32)
        xcar[...] = jnp.zeros(xcar.shape, F32)

    x = xr_ref[0]
    prev = xcar[...]
    row = lax.broadcasted_iota(jnp.int32, (tm, c), 0)
    xc = cb_ref[...] + cw_ref[3:4, :] * x
    for k in range(1, 4):
        cur = pltpu.roll(x, k, axis=0)
        old = jnp.tile(pltpu.roll(prev, k, axis=0), (tm // 8, 1))
        xc = xc + cw_ref[3 - k:4 - k, :] * jnp.where(row < k, old, cur)
    a, u = _rglru_gates(xc, wa_ref, ba_ref, wx_ref, bx_ref, lam_ref)

    s = 1
    while s < 8:
        keep = (row % 8) >= s
        a_sh = jnp.where(keep, pltpu.roll(a, s, axis=0), 1.0)
        u_sh = jnp.where(keep, pltpu.roll(u, s, axis=0), 0.0)
        u = a * u_sh + u
        a = a * a_sh
        s *= 2
    carry = hcar[0:1, :]
    groups = []
    for g in range(tm // 8):
        hg = a[g * 8:(g + 1) * 8] * carry + u[g * 8:(g + 1) * 8]
        groups.append(hg)
        carry = hg[7:8, :]
    h = jnp.concatenate(groups, axis=0)
    o_ref[0] = (h * _gelu(gr_ref[0])).astype(o_ref.dtype)
    hcar[...] = jnp.broadcast_to(carry, hcar.shape)
    xcar[...] = x[tm - 8:tm, :]
    hl_ref[0] = hcar[...]
    cl_ref[0] = xcar[...]


def rglru_prompt(xr, gr, rw, tm):
    b, t, c = xr.shape
    tm = min(tm, t)
    const = lambda a: pl.BlockSpec(a.shape, lambda i, j: (0,) * a.ndim)
    ws = (rw["conv_w"], rw["conv_b"], rw["wa"], rw["ba"], rw["wx"], rw["bx"], rw["lam"])
    tile = pl.BlockSpec((1, tm, c), lambda i, j: (i, j, 0))
    last = pl.BlockSpec((1, 8, c), lambda i, j: (i, 0, 0))
    return pl.pallas_call(
        _rglru_prompt_kernel,
        grid=(b, t // tm),
        in_specs=[tile, tile] + [const(a) for a in ws],
        out_specs=[tile, last, last],
        out_shape=[jax.ShapeDtypeStruct((b, t, c), BF16),
                   jax.ShapeDtypeStruct((b, 8, c), F32),
                   jax.ShapeDtypeStruct((b, 8, c), F32)],
        scratch_shapes=[pltpu.VMEM((8, c), F32), pltpu.VMEM((8, c), F32)],
        compiler_params=_cparams("arbitrary", "arbitrary"),
        name="rglru_prompt",
    )(xr, gr, *ws)


def _gmlp_v(v_raw, vg_ref):
    v = _gelu(v_raw)
    ms = jnp.mean(v * v, axis=-1, keepdims=True)
    return v * lax.rsqrt(ms + RMS_EPS) * vg_ref[...]


GMLP_CHUNKS = 4


def _gmlp_prompt_kernel(u_ref, v_ref, vg_ref, ws_ref, bs_ref, o_ref):
    lc = ws_ref.shape[1]
    tril = (lax.broadcasted_iota(jnp.int32, (lc, lc), 0) >= lax.broadcasted_iota(jnp.int32, (lc, lc), 1))
    lane = lax.broadcasted_iota(jnp.int32, (lc, LANES), 1)
    wts = [jnp.where(tril, ws_ref[g], 0.0).astype(BF16) for g in range(ws_ref.shape[0])]
    for ci in range(u_ref.shape[1] // lc):
        rows = slice(ci * lc, (ci + 1) * lc)
        v = _gmlp_v(v_ref[0, rows, :], vg_ref).astype(BF16)
        parts = []
        for j in range(v.shape[1] // LANES):
            vj = v[:, j * LANES:(j + 1) * LANES]
            parts.append(jnp.where(lane < HEAD_DIM, _dot(wts[2 * j], vj), _dot(wts[2 * j + 1], vj)))
        mixed = jnp.concatenate(parts, axis=1) + bs_ref[...]
        o_ref[0, rows, :] = (_gelu(u_ref[0, rows, :]) * mixed).astype(o_ref.dtype)


def gmlp_prompt(u, v, v_gain, ws, bs_exp):
    b, t, c = u.shape
    lc = ws.shape[1]
    rows = min(GMLP_CHUNKS * lc, t)
    tile = pl.BlockSpec((1, rows, c), lambda i, j: (i, j, 0))
    const = lambda a: pl.BlockSpec(a.shape, lambda i, j: (0,) * a.ndim)
    return pl.pallas_call(
        _gmlp_prompt_kernel,
        grid=(b, t // rows),
        in_specs=[tile, tile, const(v_gain), const(ws), const(bs_exp)],
        out_specs=tile,
        out_shape=jax.ShapeDtypeStruct((b, t, c), BF16),
        compiler_params=_cparams("arbitrary", "arbitrary"),
        name="gmlp_prompt",
    )(u, v, v_gain, ws, bs_exp)


DIL_SPAN = DIL_MAX
DIL_LOCKSTEP = 8


def _dil_prompt_kernel(q_ref, k_ref, v_ref, o_ref, acc_ref, m_ref, l_ref):
    span = q_ref.shape[1]
    base = pl.program_id(2) * span
    scale = HEAD_DIM ** -0.5
    lane = lax.broadcasted_iota(jnp.int32, (QBLK, LANES), 1)
    row = lax.broadcasted_iota(jnp.int32, (QBLK, LANES), 0)
    lo = lane < HEAD_DIM
    band1 = jnp.concatenate([jnp.where(lane >= row, 0.0, NEG_INF), jnp.where(lane <= row, 0.0, NEG_INF)], axis=1)
    band = jnp.concatenate([band1, band1], axis=0)
    prev_cols = lax.broadcasted_iota(jnp.int32, (2 * QBLK, 2 * QBLK), 1) < QBLK
    zero = jnp.zeros((), BF16)

    for p, (window, dil) in enumerate(DIL_PATTERNS):
        assert window // dil == QBLK and span % (QBLK * dil) == 0

        def tiles(step, carry, p=p, dil=dil):
            rows, q2, kcat, vcat, bias = [], [], [], [], []
            for u in range(DIL_LOCKSTEP):
                idx = step * DIL_LOCKSTEP + u
                start = (idx // dil) * (QBLK * dil) + idx % dil
                g0 = base + start
                has_prev = g0 >= QBLK * dil
                d_rows = pl.ds(g0, QBLK, stride=dil)
                p_rows = pl.ds(jnp.where(has_prev, g0 - QBLK * dil, g0), QBLK, stride=dil)
                rows.append(pl.ds(start, QBLK, stride=dil))
                q = (q_ref[0, rows[u], :] * scale).astype(BF16)
                q2.append(jnp.concatenate([jnp.where(lo, q, zero), jnp.where(lo, zero, q)], axis=0))
                kcat.append(jnp.concatenate([k_ref[0, p_rows, :], k_ref[0, d_rows, :]], axis=0).astype(BF16))
                vcat.append(jnp.concatenate([v_ref[0, p_rows, :], v_ref[0, d_rows, :]], axis=0).astype(BF16))
                bias.append(band + jnp.where(prev_cols, jnp.where(has_prev, 0.0, NEG_INF), 0.0))
            lock = range(DIL_LOCKSTEP)
            s = [_dot_nt(q2[u], kcat[u]) + bias[u] for u in lock]
            m = [jnp.max(jnp.maximum(s[u][:, :QBLK], s[u][:, QBLK:]), axis=-1, keepdims=True) for u in lock]
            e = [jnp.exp(s[u] - m[u]) for u in lock]
            l = [jnp.sum(e[u][:, :QBLK] + e[u][:, QBLK:], axis=-1, keepdims=True) for u in lock]
            acc = [_dot(e[u].astype(BF16), vcat[u]) for u in lock]
            for u in lock:
                acc_ref[p, rows[u], :] = jnp.where(lo, acc[u][:QBLK], acc[u][QBLK:])
                m_ref[p, rows[u], :] = jnp.where(lo, m[u][:QBLK], m[u][QBLK:])
                l_ref[p, rows[u], :] = jnp.where(lo, l[u][:QBLK], l[u][QBLK:])
            return carry

        lax.fori_loop(0, span // QBLK // DIL_LOCKSTEP, tiles, 0)

    def combine(c, carry):
        rows = pl.ds(pl.multiple_of(c * QBLK, QBLK), QBLK)
        ms = [m_ref[p, rows, :] for p in range(len(DIL_PATTERNS))]
        mx = functools.reduce(jnp.maximum, ms)
        num, den = 0.0, 0.0
        for p, mp in enumerate(ms):
            w = jnp.exp(mp - mx)
            num = num + w * acc_ref[p, rows, :]
            den = den + w * l_ref[p, rows, :]
        o_ref[0, rows, :] = (num * (1.0 / den)).astype(o_ref.dtype)
        return carry

    lax.fori_loop(0, span // QBLK, combine, 0)


def dil_prompt_attn(q, kv):
    b, t, w = q.shape
    npair = w // LANES
    span = min(DIL_SPAN, t)
    n_pat = len(DIL_PATTERNS)
    return pl.pallas_call(
        _dil_prompt_kernel,
        grid=(b, npair, t // span),
        in_specs=[pl.BlockSpec((1, span, LANES), lambda i, p, j: (i, j, p)),
                  pl.BlockSpec((1, t, LANES), lambda i, p, j: (i, 0, p)),
                  pl.BlockSpec((1, t, LANES), lambda i, p, j: (i, 0, npair + p))],
        out_specs=pl.BlockSpec((1, span, LANES), lambda i, p, j: (i, j, p)),
        out_shape=jax.ShapeDtypeStruct((b, t, w), BF16),
        scratch_shapes=[pltpu.VMEM((n_pat, span, LANES), F32)] * 3,
        compiler_params=_cparams("arbitrary", "arbitrary", "arbitrary"),
        name="dil_prompt_attn",
    )(q, kv, kv)


NSA_G, NSA_REP = 2, 4
NSA_QW = NSA_G * NSA_REP * HEAD_DIM
QPERM = np.arange(NSA_QW).reshape(NSA_G, NSA_REP, HEAD_DIM).transpose(1, 0, 2).reshape(-1)


def _block_diag(blocks):
    n, a, b = blocks.shape
    out = jnp.zeros((n, a, n, b), blocks.dtype)
    out = out.at[jnp.arange(n), :, jnp.arange(n), :].set(blocks)
    return out.reshape(n * a, n * b)


def prep_layer0(p):
    w_in = p["w_in"]
    d = w_in.shape[0]
    n_gate = NSA_G * NSA_REP * 3
    kv_w = 6 * NSA_G * HEAD_DIM
    c_gl = NSA_QW + kv_w
    c_xr = c_gl + n_gate
    d_rnn = (w_in.shape[1] - c_xr) // 2
    cols = np.concatenate([QPERM, np.arange(NSA_QW, c_gl), np.arange(c_xr, c_xr + 2 * d_rnn),
                           np.arange(c_gl, c_xr)])
    w = jnp.concatenate([w_in[:, cols], jnp.zeros((d, LANES - n_gate), w_in.dtype)], axis=1).astype(BF16)
    n = w.shape[1]
    hg = jnp.ones((n,), F32)
    hg = hg.at[0:NSA_QW].set(jnp.tile(p["q_gain"], NSA_QW // HEAD_DIM))
    hg = hg.at[NSA_QW + 2 * LANES:NSA_QW + 3 * LANES].set(jnp.tile(p["k_gain"][1], 2))
    hg = hg.at[NSA_QW + 4 * LANES:NSA_QW + 5 * LANES].set(jnp.tile(p["k_gain"][2], 2))
    qb = NSA_QW // LANES
    c_rows, c_win, c_x, c_g, c_l = NSA_QW, NSA_QW + 4 * LANES, NSA_QW + 6 * LANES, NSA_QW + 6 * LANES + d_rnn, \
        NSA_QW + 6 * LANES + 2 * d_rnn
    outs = ((0, NSA_QW, BF16), (c_rows, 4 * LANES, F32), (c_rows, 4 * LANES, BF16), (c_win, 2 * LANES, F32),
            (c_win, 2 * LANES, BF16), (c_x, d_rnn, F32), (c_g, d_rnn, F32), (c_l, LANES, F32))
    cw = {}
    for c, nm in enumerate("kv"):
        w1 = p["cmp_w1"][c]
        big = jnp.zeros((CMP_BLOCK, NSA_G, HEAD_DIM, NSA_G, HEAD_DIM), F32)
        for g in range(NSA_G):
            big = big.at[:, g, :, g, :].set(w1)
        cw["w1" + nm] = big.reshape(CMP_BLOCK * LANES, LANES).astype(BF16)
        cw["pos" + nm] = jnp.tile(p["cmp_pos"][c], (1, NSA_G)).reshape(1, CMP_BLOCK * LANES)
        cw["w2" + nm] = _block_diag(jnp.stack([p["cmp_w2"][c]] * NSA_G)).astype(BF16)
    cw["kg0"] = jnp.tile(p["k_gain"][0], 2).reshape(1, LANES)
    rw = dict(conv_w=p["conv_w"], conv_b=p["conv_b"].reshape(1, -1),
              wa=_block_diag(p["wa"]).astype(BF16), ba=p["ba"].reshape(1, -1),
              wx=_block_diag(p["wx"]).astype(BF16), bx=p["bx"].reshape(1, -1), lam=p["lam"].reshape(1, -1))
    w_out = p["w_out"]
    return dict(w=w, hg=hg.reshape(1, n), norm_blocks=tuple(range(qb)) + (qb + 2, qb + 4), outs=outs,
                cw=cw, rw=rw, wo_a=w_out[QPERM].astype(BF16), wo_b=w_out[NSA_QW:].astype(BF16))


def prep_layer1(p):
    w = p["w_in"].astype(BF16)
    n = w.shape[1]
    c_w = p["v_gain"].shape[0]
    dil_w = (n - 2 * c_w) // 3
    hg = jnp.ones((n,), F32)
    hg = hg.at[2 * c_w:2 * c_w + dil_w].set(jnp.tile(p["q_gain"], dil_w // HEAD_DIM))
    hg = hg.at[2 * c_w + dil_w:2 * c_w + 2 * dil_w].set(jnp.tile(p["k_gain"], dil_w // HEAD_DIM))
    b0 = 2 * c_w // LANES
    nbq = dil_w // LANES
    outs = ((0, c_w, F32), (c_w, c_w, F32), (2 * c_w, dil_w, F32), (2 * c_w + dil_w, 2 * dil_w, F32))
    w_out = p["w_out"]
    gw = c_w // p["ws"].shape[0]
    return dict(w=w, hg=hg.reshape(1, n), norm_blocks=tuple(range(b0, b0 + 2 * nbq)), outs=outs,
                v_gain=p["v_gain"].reshape(1, c_w), ws=p["ws"], bs_exp=jnp.repeat(p["bs"].T, gw, axis=1),
                ws_diag=jnp.repeat(p["ws"][:, 0, 0], gw).reshape(1, c_w),
                bs0=jnp.repeat(p["bs"][:, 0], gw).reshape(1, c_w),
                wo_a=w_out[:c_w].astype(BF16), wo_b=w_out[c_w:].astype(BF16))


def layer0_prompt(x, mod, ln_mix, ln_ffn, ffn_w, pp):
    b, t, _ = x.shape
    nwin = min(NSA_WINDOW, t)
    rows_spec, win_spec = pp["outs"][1], pp["outs"][3]
    q_bf, rows, rows_bf, _, win_bf, xr, gr, gl, rows_t, win_t = in_proj(
        x, mod, 0, 1, ln_mix, pp["w"], pp["hg"], pp["norm_blocks"], pp["outs"], tm=512,
        touts=((rows_spec[0], rows_spec[1], t), (win_spec[0], win_spec[1], nwin)))
    kc, vc = nsa_compress(rows, pp["cw"])
    o_nsa = nsa_prompt_attn(q_bf, kc, vc, rows_bf, win_bf, gl)
    o_rnn, h_last, conv_last = rglru_prompt(xr, gr, pp["rw"], tm=256)
    y = out_ffn(o_nsa, o_rnn, x, mod, ln_ffn, pp["wo_a"], pp["wo_b"], *ffn_w, tm=512)
    state = (jnp.transpose(rows_t.reshape(b, 4, NSA_G, HEAD_DIM, t), (0, 4, 1, 2, 3)),
             jnp.transpose(win_t.reshape(b, 2, NSA_G, HEAD_DIM, nwin), (0, 4, 1, 2, 3)),
             h_last[:, 0], conv_last[:, 5:8])
    return y, state


def layer1_prompt(x, mod, ln_mix, ln_ffn, ffn_w, pp):
    b, t, _ = x.shape
    nkv = min(DIL_MAX, t)
    kv_spec = pp["outs"][3]
    u, v, q, kv, kv_t = in_proj(x, mod, 0, 1, ln_mix, pp["w"], pp["hg"], pp["norm_blocks"], pp["outs"], tm=512,
                                touts=((kv_spec[0], kv_spec[1], nkv),))
    o_c = gmlp_prompt(u, v, pp["v_gain"], pp["ws"], pp["bs_exp"])
    o_d = dil_prompt_attn(q, kv)
    y = out_ffn(o_c, o_d, x, mod, ln_ffn, pp["wo_a"], pp["wo_b"], *ffn_w, tm=512)
    heads = kv.shape[2] // (2 * HEAD_DIM)
    return y, jnp.transpose(kv_t.reshape(b, 2, heads, HEAD_DIM, nkv), (0, 4, 1, 2, 3))


def _row_softmax_parts(s_list, mask_list, s_new):
    m = s_new[0]
    for sn in s_new[1:]:
        m = jnp.maximum(m, sn)
    for s, mk in zip(s_list, mask_list):
        sm = s if mk is None else jnp.where(mk, s, NEG_INF)
        m = jnp.maximum(m, jnp.max(sm, axis=-1, keepdims=True))
    es, den = [], 0.0
    for s, mk in zip(s_list, mask_list):
        e = jnp.exp(s - m)
        if mk is not None:
            e = jnp.where(mk, e, 0.0)
        es.append(e)
        den = den + jnp.sum(e, axis=-1, keepdims=True)
    en = [jnp.exp(sn - m) for sn in s_new]
    for e in en:
        den = den + e
    return es, en, 1.0 / den


def _nsa_decode_kernel(pt_ref, cache_ref, q_ref, rn_ref, wn_ref, wnc_ref, gl_ref, sw_ref,
                       w1k_ref, posk_ref, w2k_ref, w1v_ref, posv_ref, w2v_ref, kg_ref, ex_ref,
                       o_ref, wo_ref, buf, xk, xv, sem):
    b = pl.program_id(0)
    nbatch = pl.num_programs(0)
    slot = b % 2
    n_pages = pt_ref.shape[1]
    page = cache_ref.shape[3]
    past = n_pages * page
    nc = past // CMP_BLOCK
    nb = past // SEL_BLOCK
    scale = HEAD_DIM ** -0.5
    rep = q_ref.shape[2] // LANES
    nh = 2 * rep

    nseq = q_ref.shape[0]

    def copies(step, sl):
        return [pltpu.make_async_copy(cache_ref.at[pt_ref[step * nseq + u, j]], buf.at[sl, u, j], sem.at[sl, u, j])
                for u in range(nseq) for j in range(n_pages)]

    @pl.when(b == 0)
    def _():
        for c in copies(0, 0):
            c.start()

    @pl.when(b + 1 < nbatch)
    def _():
        for c in copies(b + 1, 1 - slot):
            c.start()

    for c in copies(b, slot):
        c.wait()

    lane = lax.broadcasted_iota(jnp.int32, (nh, LANES), 1)
    hrow = lax.broadcasted_iota(jnp.int32, (nh, LANES), 0)
    blk_t = lax.broadcasted_iota(jnp.int32, (LANES, LANES), 0)
    wb = sw_ref.shape[3]
    last = lax.broadcasted_iota(jnp.int32, (LANES, wb), 1) == wb - 1

    def padded(c2):
        z = jnp.zeros((LANES - nc // 2, LANES), F32)
        return jnp.concatenate([c2[:nc // 2], z, c2[nc // 2:], z], axis=0)

    def one_sequence(u):
        q = q_ref[u]
        qrows = []
        for g in range(2):
            for r in range(rep):
                qrows.append(q[:, r * LANES:(r + 1) * LANES])
        qm = jnp.concatenate(qrows, axis=0)
        qm = jnp.where((lane // HEAD_DIM) == (hrow // rep), qm, jnp.zeros((), qm.dtype))
        qf = qm.astype(F32)

        kwt, vwt = sw_ref[u, 0], sw_ref[u, 1]
        wn = wn_ref[u]
        s_w = _dot(qm, kwt.astype(BF16)) * scale
        s_w_new = jnp.sum(qf * wn[:, :LANES], axis=-1, keepdims=True) * scale
        yield

        for j in range(n_pages):
            xk[u, pl.ds(j * page, page), :] = buf[slot, u, j, 0].T
            xv[u, pl.ds(j * page, page), :] = buf[slot, u, j, 1].T
            yield
        ldk = lambda s, n, st: xk[u, pl.ds(s, n, stride=st), :]
        ldv = lambda s, n, st: xv[u, pl.ds(s, n, stride=st), :]
        kc = _head_norm(_compress_rows(ldk, nc // 2, w1k_ref, posk_ref, w2k_ref), kg_ref[...])
        yield
        vc = _compress_rows(ldv, nc // 2, w1v_ref, posv_ref, w2v_ref)
        yield
        kcp = padded(kc).astype(BF16)
        vcp = padded(vc).astype(BF16)
        s_c = _dot_nt(qm, kcp) * scale
        yield
        cl = lax.broadcasted_iota(jnp.int32, (nh, 2 * LANES), 1)
        cmask = (cl % LANES) < nc // 2
        s_c = jnp.where(cmask, s_c, NEG_INF)
        e_c = jnp.where(cmask, jnp.exp(s_c - jnp.max(s_c, axis=-1, keepdims=True)), 0.0)
        yield
        p_c = e_c * (1.0 / jnp.maximum(jnp.sum(e_c, axis=-1, keepdims=True), 1e-30))
        o_cmp = _dot(p_c.astype(BF16), vcp)
        yield

        pp = p_c[:, :LANES] + p_c[:, LANES:]
        imps = []
        for g in range(2):
            ig = pp[g * rep:g * rep + 1]
            for r in range(1, rep):
                ig = ig + pp[g * rep + r:g * rep + r + 1]
            imps += [ig] * rep
        imp = jnp.concatenate(imps + [jnp.zeros((LANES - nh, LANES), F32)], axis=0)
        imp_t = imp.T
        imp_t = jnp.where((blk_t == 0) | (blk_t == nb), FORCE_SCORE, imp_t)
        imp_t = jnp.where(blk_t <= nb, imp_t, -3.0)
        yield

        rn = rn_ref[u]
        kst = jnp.concatenate([buf[slot, u, j, 2] for j in range(n_pages)], axis=1).astype(BF16)
        s_sel = _dot(qm, kst) * scale
        s_sel_new = jnp.sum(qf * rn[:, 2 * LANES:3 * LANES], axis=-1, keepdims=True) * scale
        yield
        (e_w,), (e_wn,), inv_w = _row_softmax_parts([s_w], [None], [s_w_new])
        o_win = (_dot_nt(e_w.astype(BF16), vwt.astype(BF16)) + e_wn * wn[:, LANES:]) * inv_w
        yield

        work = imp_t
        for _ in range(min(N_SEL, nb + 1)):
            work = _pick_top(work, blk_t.astype(F32))
            yield
        sel_t = jnp.where(work == PICKED, 1.0, 0.0)
        selm = _dot(sel_t.T[:nh].astype(BF16), ex_ref[...]) > 0.5
        yield
        (e_s,), (e_sn,), inv_s = _row_softmax_parts([s_sel], [selm], [s_sel_new])
        yield
        vst = jnp.concatenate([buf[slot, u, j, 3] for j in range(n_pages)], axis=1).astype(BF16)
        o_sel = (_dot_nt(e_s.astype(BF16), vst) + e_sn * rn[:, 3 * LANES:4 * LANES]) * inv_s
        yield

        gate = _sigmoid(gl_ref[u])
        o = gate[:, 0:1] * o_cmp + gate[:, 1:2] * o_sel + gate[:, 2:3] * o_win
        l1 = lax.broadcasted_iota(jnp.int32, (1, LANES), 1)
        o_ref[u] = jnp.concatenate([jnp.where(l1 < HEAD_DIM, o[r:r + 1], o[rep + r:rep + r + 1])
                                    for r in range(rep)], axis=1).astype(o_ref.dtype)
        wnc = wnc_ref[u]
        wo_ref[u, 0] = jnp.where(last, wnc[:, 0:1], pltpu.roll(kwt, wb - 1, axis=1))
        wo_ref[u, 1] = jnp.where(last, wnc[:, 1:2], pltpu.roll(vwt, wb - 1, axis=1))

    phases = [one_sequence(u) for u in range(nseq)]
    while phases:
        phases = [g for g in phases if next(g, True) is None]


def nsa_decode(page_table, cache, q_bf, rows_new, win_new, gl3, state_win, cw):
    b, n_pages = page_table.shape
    n_pool, page, ncomp, ng, hd = cache.shape
    past = n_pages * page
    nb = past // SEL_BLOCK
    assert past % SEL_BLOCK == 0 and nb < LANES and past // CMP_BLOCK <= 2 * LANES and ng * hd == LANES
    wb = state_win.shape[1]
    assert wb <= NSA_WINDOW
    cache_t = jnp.transpose(cache, (0, 2, 3, 4, 1)).reshape(n_pool, ncomp, LANES, page)
    win_t = jnp.transpose(state_win, (0, 2, 3, 4, 1)).reshape(b, 2, LANES, wb)
    win_col = jnp.transpose(win_new.reshape(b, 2, LANES), (0, 2, 1))
    expand = (jnp.arange(LANES)[:, None] == (jnp.arange(past) // SEL_BLOCK)[None, :]).astype(BF16)
    ws = (cw["w1k"], cw["posk"], cw["w2k"], cw["w1v"], cw["posv"], cw["w2v"], cw["kg0"], expand)
    ns = NSA_DECODE_SEQS
    assert b % ns == 0
    per_b = lambda a: pl.BlockSpec((ns,) + a.shape[1:], lambda i, pt: (i,) + (0,) * (a.ndim - 1))
    const = lambda a: pl.BlockSpec(a.shape, lambda i, pt: (0,) * a.ndim)
    grid_spec = pltpu.PrefetchScalarGridSpec(
        num_scalar_prefetch=1,
        grid=(b // ns,),
        in_specs=[pl.BlockSpec(memory_space=pl.ANY), per_b(q_bf), per_b(rows_new), per_b(win_new), per_b(win_col),
                  per_b(gl3), per_b(win_t)] + [const(a) for a in ws],
        out_specs=[pl.BlockSpec((ns, 1, q_bf.shape[2]), lambda i, pt: (i, 0, 0)),
                   pl.BlockSpec((ns, 2, LANES, wb), lambda i, pt: (i, 0, 0, 0))],
        scratch_shapes=[pltpu.VMEM((2, ns, n_pages, ncomp, LANES, page), F32), pltpu.VMEM((ns, past, LANES), F32),
                        pltpu.VMEM((ns, past, LANES), F32), pltpu.SemaphoreType.DMA((2, ns, n_pages))],
    )
    o, win_out_t = pl.pallas_call(
        _nsa_decode_kernel,
        grid_spec=grid_spec,
        out_shape=[jax.ShapeDtypeStruct((b, 1, q_bf.shape[2]), BF16),
                   jax.ShapeDtypeStruct(win_t.shape, F32)],
        compiler_params=_cparams("arbitrary"),
        name="nsa_decode",
    )(page_table, cache_t, q_bf, rows_new, win_new, win_col, gl3, win_t, *ws)
    return o, jnp.transpose(win_out_t.reshape(b, 2, ng, hd, wb), (0, 4, 1, 2, 3))


def _rglru_decode_kernel(xr_ref, gr_ref, cs_ref, h0_ref, cw_ref, cb_ref, wa_ref, ba_ref, wx_ref, bx_ref, lam_ref,
                         o_ref, h_ref, cn_ref):
    x = xr_ref[...]
    xc = cb_ref[...] + cw_ref[3:4, :] * x
    for k in range(3):
        xc = xc + cw_ref[k:k + 1, :] * cs_ref[k]
    a, u = _rglru_gates(xc, wa_ref, ba_ref, wx_ref, bx_ref, lam_ref)
    h = a * h0_ref[...] + u
    h_ref[...] = h
    o_ref[...] = (h * _gelu(gr_ref[...])).astype(o_ref.dtype)
    cn_ref[0] = cs_ref[1]
    cn_ref[1] = cs_ref[2]
    cn_ref[2] = x


def rglru_decode(xr, gr, conv_t, h0, rw):
    ws = (rw["conv_w"], rw["conv_b"], rw["wa"], rw["ba"], rw["wx"], rw["bx"], rw["lam"])
    return pl.pallas_call(
        _rglru_decode_kernel,
        out_shape=[jax.ShapeDtypeStruct(xr.shape, BF16), jax.ShapeDtypeStruct(xr.shape, F32),
                   jax.ShapeDtypeStruct(conv_t.shape, F32)],
        compiler_params=pltpu.CompilerParams(vmem_limit_bytes=VMEM_LIMIT),
        name="rglru_decode",
    )(xr, gr, conv_t, h0, *ws)


def _gmlp_decode_kernel(u_ref, v_ref, vg_ref, wd_ref, b0_ref, o_ref, vn_ref):
    v = _gmlp_v(v_ref[...], vg_ref)
    vn_ref[...] = v
    o_ref[...] = (_gelu(u_ref[...]) * (wd_ref[...] * v + b0_ref[...])).astype(o_ref.dtype)


def gmlp_decode(u, v, v_gain, ws_diag, bs0):
    return pl.pallas_call(
        _gmlp_decode_kernel,
        out_shape=[jax.ShapeDtypeStruct(u.shape, BF16), jax.ShapeDtypeStruct(u.shape, F32)],
        compiler_params=pltpu.CompilerParams(vmem_limit_bytes=VMEM_LIMIT),
        name="gmlp_decode",
    )(u, v, v_gain, ws_diag, bs0)


def _dil_decode_kernel(q_ref, kvn_ref, st_ref, o_ref, so_ref):
    nh, hd, wb = st_ref.shape[2], st_ref.shape[3], st_ref.shape[4]
    scale = HEAD_DIM ** -0.5
    n_pat = len(DIL_PATTERNS)
    lane1 = lax.broadcasted_iota(jnp.int32, (1, wb), 1)
    dist = wb - lane1
    cnt = jnp.zeros((1, wb), F32)
    for window, dil in DIL_PATTERNS:
        cnt = cnt + jnp.where((dist <= window) & ((dist & (dil - 1)) == 0), 1.0, 0.0)
    last = lax.broadcasted_iota(jnp.int32, (hd, wb), 1) == wb - 1

    def shifted(x, new_col):
        return jnp.where(last, new_col, pltpu.roll(x, wb - 1, axis=1))

    qt = q_ref[0] * scale
    kvn = kvn_ref[0]
    s_rows, s_new_rows = [], []
    for h in range(nh):
        kh = st_ref[0, 0, h]
        qh = qt[:, h:h + 1]
        kn = kvn[:, h:h + 1]
        s_rows.append(jnp.sum(kh * qh, axis=0, keepdims=True))
        s_new_rows.append(jnp.sum(kn * qh, axis=0, keepdims=True))
        so_ref[0, 0, h] = shifted(kh, kn)
    s = jnp.concatenate(s_rows, axis=0)
    s_new = jnp.concatenate(s_new_rows, axis=0)
    s = jnp.where(cnt > 0.0, s, NEG_INF)
    m = jnp.maximum(jnp.max(s, axis=-1, keepdims=True), s_new)
    e = cnt * jnp.exp(s - m)
    e_new = n_pat * jnp.exp(s_new - m)
    inv = 1.0 / (jnp.sum(e, axis=-1, keepdims=True) + e_new)
    for h in range(nh):
        vh = st_ref[0, 1, h]
        vn = kvn[:, nh + h:nh + h + 1]
        acc = jnp.sum(vh * e[h:h + 1, :], axis=1, keepdims=True) + e_new[h:h + 1, :] * vn
        o_ref[0, :, h:h + 1] = acc * inv[h:h + 1, :]
        so_ref[0, 1, h] = shifted(vh, vn)


def dil_decode(q, kv_new, state):
    b, wb, _, nh, hd = state.shape
    assert wb == DIL_MAX
    st_t = jnp.transpose(state, (0, 2, 3, 4, 1))
    q_t = jnp.transpose(q, (0, 2, 1))
    kvn_t = jnp.transpose(kv_new.reshape(b, 2 * nh, hd), (0, 2, 1))
    o_t, so_t = pl.pallas_call(
        _dil_decode_kernel,
        grid=(b,),
        in_specs=[pl.BlockSpec((1, hd, nh), lambda i: (i, 0, 0)),
                  pl.BlockSpec((1, hd, 2 * nh), lambda i: (i, 0, 0)),
                  pl.BlockSpec((1, 2, nh, hd, wb), lambda i: (i, 0, 0, 0, 0))],
        out_specs=[pl.BlockSpec((1, hd, nh), lambda i: (i, 0, 0)),
                   pl.BlockSpec((1, 2, nh, hd, wb), lambda i: (i, 0, 0, 0, 0))],
        out_shape=[jax.ShapeDtypeStruct((b, hd, nh), F32), jax.ShapeDtypeStruct(st_t.shape, F32)],
        compiler_params=_cparams("arbitrary"),
        name="dil_decode",
    )(q_t, kvn_t, st_t)
    return jnp.transpose(o_t, (0, 2, 1)), jnp.transpose(so_t, (0, 4, 1, 2, 3))


def layer0_sample(x, mod, ln_mix, ln_ffn, ffn_w, pp, cache, page_table, state_win, state_h, state_conv):
    b, _, d = x.shape
    xs = x.reshape(1, b, d)
    q_bf, rows, _, win, _, xr, gr, gl = in_proj(
        xs, mod, 0, 1, ln_mix, pp["w"], pp["hg"], pp["norm_blocks"], pp["outs"], tm=b)
    n_gate = NSA_G * NSA_REP * 3
    gl3 = gl[0, :, :n_gate].reshape(b, NSA_G * NSA_REP, 3)
    o_nsa, win_out = nsa_decode(page_table, cache, q_bf.reshape(b, 1, -1), rows.reshape(b, 1, -1),
                                win.reshape(b, 1, -1), gl3, state_win, pp["cw"])
    o_rnn, h_new, conv_new = rglru_decode(xr[0], gr[0], state_conv.transpose(1, 0, 2), state_h, pp["rw"])
    y = out_ffn(o_nsa.reshape(1, b, -1), o_rnn[None], xs, mod, ln_ffn, pp["wo_a"], pp["wo_b"], *ffn_w, tm=b)
    state = (rows.reshape(b, 1, 4, NSA_G, HEAD_DIM), win_out, h_new, conv_new.transpose(1, 0, 2))
    return y.reshape(b, 1, d), state


def layer1_sample(x, mod, ln_mix, ln_ffn, ffn_w, pp, state_dil):
    b, _, d = x.shape
    xs = x.reshape(1, b, d)
    u, v, q, kv = in_proj(xs, mod, 0, 1, ln_mix, pp["w"], pp["hg"], pp["norm_blocks"], pp["outs"], tm=b)
    o_c, v_n = gmlp_decode(u[0], v[0], pp["v_gain"], pp["ws_diag"], pp["bs0"])
    heads = state_dil.shape[3]
    o_d, dil_out = dil_decode(q.reshape(b, heads, HEAD_DIM), kv.reshape(b, 2, heads, HEAD_DIM), state_dil)
    y = out_ffn(o_c[None], o_d.reshape(1, b, -1).astype(BF16), xs, mod, ln_ffn, pp["wo_a"], pp["wo_b"], *ffn_w,
                tm=b)
    return y.reshape(b, 1, d), (dil_out, v_n.reshape(b, 1, -1))


def kernel(x_prompt, x_sample, cache_nsa_kv, state_nsa_win, state_rglru_h, state_rglru_conv, state_dil_kv, page_table, c_prompt, c_sample, norm_mix_g, norm_ffn_g, w_ada, b_ada, w_ffn_gate, w_ffn_up, w_ffn_down, w_in_ab, w_out_ab, nsa_q_gain, nsa_k_gain, nsa_cmp_w1, nsa_cmp_w2, nsa_cmp_pos, rg_conv_w, rg_conv_b, rg_wa, rg_ba, rg_wx, rg_bx, rg_lambda, w_in_cd, w_out_cd, gmlp_v_gain, gmlp_ws, gmlp_bs, dil_q_gain, dil_k_gain):
    depth = norm_mix_g.shape[0]
    bp, bs = x_prompt.shape[0], x_sample.shape[0]
    pad = -(bp + bs) % 8
    c_all = jnp.concatenate([c_prompt, c_sample, jnp.zeros((pad, c_prompt.shape[1]), F32)], axis=0)
    mod_all = ada_mod(c_all, w_ada, b_ada)
    yp, ys = x_prompt, x_sample
    kv_p, kv_s, win_p, win_s, h_p, h_s, conv_p, conv_s, dil_p, dil_s, gv_s = ([] for _ in range(11))
    for layer in range(depth):
        i = layer // 2
        mod_p = mod_all[layer, :bp, None, :]
        mod_s = mod_all[layer, None, bp:bp + bs, :]
        ffn_w = (w_ffn_gate[layer].astype(BF16), w_ffn_up[layer].astype(BF16), w_ffn_down[layer].astype(BF16))
        if layer % 2 == 0:
            pp = prep_layer0(dict(w_in=w_in_ab[i], w_out=w_out_ab[i], q_gain=nsa_q_gain[i], k_gain=nsa_k_gain[i],
                                  cmp_w1=nsa_cmp_w1[i], cmp_w2=nsa_cmp_w2[i], cmp_pos=nsa_cmp_pos[i],
                                  conv_w=rg_conv_w[i], conv_b=rg_conv_b[i], wa=rg_wa[i], ba=rg_ba[i],
                                  wx=rg_wx[i], bx=rg_bx[i], lam=rg_lambda[i]))
            yp, st = layer0_prompt(yp, mod_p, norm_mix_g[layer], norm_ffn_g[layer], ffn_w, pp)
            kv_p.append(st[0]); win_p.append(st[1]); h_p.append(st[2]); conv_p.append(st[3])
            ys, st = layer0_sample(ys, mod_s, norm_mix_g[layer], norm_ffn_g[layer], ffn_w, pp, cache_nsa_kv[i],
                                   page_table, state_nsa_win[i], state_rglru_h[i], state_rglru_conv[i])
            kv_s.append(st[0]); win_s.append(st[1]); h_s.append(st[2]); conv_s.append(st[3])
        else:
            pp = prep_layer1(dict(w_in=w_in_cd[i], w_out=w_out_cd[i], v_gain=gmlp_v_gain[i], ws=gmlp_ws[i],
                                  bs=gmlp_bs[i], q_gain=dil_q_gain[i], k_gain=dil_k_gain[i]))
            yp, st = layer1_prompt(yp, mod_p, norm_mix_g[layer], norm_ffn_g[layer], ffn_w, pp)
            dil_p.append(st)
            ys, st = layer1_sample(ys, mod_s, norm_mix_g[layer], norm_ffn_g[layer], ffn_w, pp, state_dil_kv[i])
            dil_s.append(st[0]); gv_s.append(st[1])
    return (yp, ys, jnp.stack(kv_p), jnp.stack(kv_s), jnp.stack(win_p), jnp.stack(win_s),
            jnp.stack(h_p), jnp.stack(h_s), jnp.stack(conv_p), jnp.stack(conv_s),
            jnp.stack(dil_p), jnp.stack(dil_s), jnp.stack(gv_s))
```
